```python
import math
import jax, jax.numpy as jnp
from jax import lax
import numpy as np

D_MODEL = 2048
BATCH = 1
SEQ = 8192
DEPTH = 1
DEC_BATCH = 128
DEC_SEQ = 1
PAST_LEN = 2048
PAGE_SIZE = 128

N_GROUPS = 3
DILATIONS = (1, 4, 16)
N_BACK = 128
WINDOWS = (128, 512, 2048)
N_SLOTS = 8
HEAD_DIM = 128
D_ATTN = N_SLOTS * HEAD_DIM
QKV_COLS = N_GROUPS * N_SLOTS * HEAD_DIM
D_CONV = D_MODEL // 2
CONV_W = 3
BLOCK = N_BACK
NUM_BUCKETS = 32
MAX_EXACT = NUM_BUCKETS // 2
MAX_DISTANCE = 2048
IN_COLS = 3 * QKV_COLS + D_ATTN + 4 * D_CONV + 2 * D_MODEL
EPS = 1e-6

kernel_name = 'hybrid_dilated_attn_shortconv_decode_step'


def rms_norm(x, g):
    xf = x.astype(jnp.float32)
    r = lax.rsqrt(jnp.mean(xf * xf, axis=-1, keepdims=True) + EPS)
    return (xf * r).astype(x.dtype) * g


def t5_bucket(dist):
    large = MAX_EXACT + (jnp.log(jnp.maximum(dist, 1).astype(jnp.float32) / MAX_EXACT)
                         / math.log(MAX_DISTANCE / MAX_EXACT) * (NUM_BUCKETS - MAX_EXACT)).astype(jnp.int32)
    large = jnp.minimum(large, NUM_BUCKETS - 1)
    return jnp.where(dist < MAX_EXACT, dist, large)


def layer_inputs(x, norm_gain, w_in, q_norm_gain, k_norm_gain):
    bsz, t, _ = x.shape
    xn = rms_norm(x, norm_gain)
    proj = jnp.einsum('btd,de->bte', xn, w_in)
    sizes = [QKV_COLS] * 3 + [D_ATTN, D_CONV, D_CONV, D_CONV, D_CONV, D_MODEL, D_MODEL]
    points = [int(p) for p in np.cumsum(sizes)[:-1]]
    q, k, v, gate_a, b, c, xin, gate_b, merge_a, merge_b = jnp.split(proj, points, axis=-1)
    shape5 = (bsz, t, N_GROUPS, N_SLOTS, HEAD_DIM)
    q = rms_norm(q.reshape(shape5), q_norm_gain)
    k = rms_norm(k.reshape(shape5), k_norm_gain)
    v = v.reshape(shape5)
    return q, k, v, gate_a, b, c * xin, gate_b, merge_a, merge_b


def short_conv(hp, conv_w):
    t = hp.shape[1] - (CONV_W - 1)
    return conv_w[0] * hp[:, :t] + conv_w[1] * hp[:, 1:t + 1] + conv_w[2] * hp[:, 2:t + 2]


def prompt_group(q, k, v, bias_tab, dil):
    bsz, s, h, dh = q.shape
    ls = s // dil
    nb = -(-ls // BLOCK)
    lp = nb * BLOCK

    def to_sub(t):
        t = t.reshape(bsz, ls, dil, h, dh).transpose(0, 2, 1, 3, 4)
        return jnp.pad(t, ((0, 0), (0, 0), (0, lp - ls), (0, 0), (0, 0)))

    def key_blocks(t):
        t = jnp.pad(to_sub(t), ((0, 0), (0, 0), (BLOCK, 0), (0, 0), (0, 0)))
        t = t.reshape(bsz, dil, nb + 1, BLOCK, h, dh)
        return jnp.concatenate([t[:, :, :-1], t[:, :, 1:]], axis=3)

    qs = to_sub(q).reshape(bsz, dil, nb, BLOCK, h, dh)
    kb, vb = key_blocks(k), key_blocks(v)
    qq = jnp.arange(BLOCK)[:, None]
    kk = jnp.arange(2 * BLOCK)[None, :]
    delta = BLOCK + qq - kk
    blk = jnp.arange(nb)[:, None, None]
    valid = ((delta >= 0) & (delta <= N_BACK))[None] & ((blk - 1) * BLOCK + kk[None] >= 0)
    bias = bias_tab[t5_bucket(dil * jnp.maximum(delta, 0))].transpose(2, 0, 1)
    sc = jnp.einsum('brnqhd,brnkhd->brnhqk', qs, kb).astype(jnp.float32) * (HEAD_DIM ** -0.5)
    sc = jnp.where(valid[None, None, :, None], sc + bias.astype(jnp.float32), -jnp.inf)
    m = jnp.max(sc, axis=-1)
    p = jnp.exp(sc - m[..., None])
    l = jnp.sum(p, axis=-1)
    o = jnp.einsum('brnhqk,brnkhd->brnqhd', p.astype(v.dtype), vb).astype(jnp.float32)
    o = o / l.transpose(0, 1, 2, 4, 3)[..., None]
    lse = (m + jnp.log(l)).transpose(0, 1, 2, 4, 3)

    def from_sub(t):
        t = t.reshape((bsz, dil, lp) + t.shape[4:])[:, :, :ls]
        t = jnp.swapaxes(t, 1, 2)
        return t.reshape((bsz, s) + t.shape[3:])

    return from_sub(o), from_sub(lse)


def sample_group(q, k_new, v_new, cache_kv, bias_tab, dil):
    t = q.shape[1]
    lc = cache_kv.shape[1]
    j = jnp.arange(t)[:, None]
    kk = jnp.arange(N_BACK + 1)[None, :]
    idx = lc + j - dil * kk
    from_new = idx >= lc
    valid = idx >= 0
    kv_c = cache_kv[:, jnp.clip(idx, 0, lc - 1)]
    kv_n = jnp.stack([k_new, v_new], axis=2)[:, jnp.clip(idx - lc, 0, t - 1)]
    kv = jnp.where(from_new[None, :, :, None, None, None], kv_n, kv_c.astype(kv_n.dtype))
    kg, vg = kv[:, :, :, 0], kv[:, :, :, 1]
    bias = bias_tab[t5_bucket(dil * kk[0])]
    sc = jnp.einsum('bthd,btkhd->bhtk', q, kg).astype(jnp.float32) * (HEAD_DIM ** -0.5)
    sc = jnp.where(valid[None, None], sc + bias.T.astype(jnp.float32)[None, :, None, :], -jnp.inf)
    m = jnp.max(sc, axis=-1)
    p = jnp.exp(sc - m[..., None])
    l = jnp.sum(p, axis=-1)
    o = jnp.einsum('bhtk,btkhd->bthd', p.astype(vg.dtype), vg).astype(jnp.float32)
    o = o / jnp.swapaxes(l, 1, 2)[..., None]
    lse = jnp.swapaxes(m + jnp.log(l), 1, 2)
    return o, lse


def layer_output(x, outs, lses, gate_a, b, z, gate_b, merge_a, merge_b, w_out_a, w_out_b, w_o):
    wts = jax.nn.softmax(jnp.stack(lses, axis=0), axis=0)[..., None]
    o = jnp.sum(wts * jnp.stack(outs, axis=0), axis=0)
    o = o.reshape(o.shape[0], o.shape[1], D_ATTN).astype(x.dtype)
    ya = jnp.einsum('btc,cd->btd', o * jax.nn.silu(gate_a), w_out_a)
    yb = jnp.einsum('btc,cd->btd', b * z * jax.nn.silu(gate_b), w_out_b)
    merged = jax.nn.sigmoid(merge_a) * ya + jax.nn.sigmoid(merge_b) * yb
    return x + jnp.einsum('btd,de->bte', merged, w_o)


def setup_inputs(seed: int = 0) -> dict:
    key = jax.random.key(seed)
    ks = jax.random.split(key, 16)
    nrm = jax.random.normal
    kv_shape = lambda w: (DEC_BATCH, min(w, PAST_LEN), 2, N_SLOTS, HEAD_DIM)
    return {
        'x_prompt': nrm(ks[0], (BATCH, SEQ, D_MODEL), jnp.float32),
        'x_sample': nrm(ks[1], (DEC_BATCH, DEC_SEQ, D_MODEL), jnp.float32),
        'cache_kv_w128': nrm(ks[2], kv_shape(WINDOWS[0]), jnp.float32),
        'cache_kv_w512': nrm(ks[3], kv_shape(WINDOWS[1]), jnp.float32),
        'cache_kv_w2048': nrm(ks[4], kv_shape(WINDOWS[2]), jnp.float32),
        'state_conv': nrm(ks[5], (DEC_BATCH, CONV_W - 1, D_CONV), jnp.float32),
        'norm_gain': 1.0 + 0.1 * nrm(ks[6], (D_MODEL,), jnp.float32),
        'w_in': nrm(ks[7], (D_MODEL, IN_COLS), jnp.float32) * D_MODEL ** -0.5,
        'q_norm_gain': 1.0 + 0.1 * nrm(ks[8], (HEAD_DIM,), jnp.float32),
        'k_norm_gain': 1.0 + 0.1 * nrm(ks[9], (HEAD_DIM,), jnp.float32),
        'rel_bias': 0.2 * nrm(ks[10], (NUM_BUCKETS, N_GROUPS, N_SLOTS), jnp.float32),
        'conv_w': nrm(ks[11], (CONV_W, D_CONV), jnp.float32) * CONV_W ** -0.5,
        'w_out_a': nrm(ks[12], (D_ATTN, D_MODEL), jnp.float32) * D_ATTN ** -0.5,
        'w_out_b': nrm(ks[13], (D_CONV, D_MODEL), jnp.float32) * D_CONV ** -0.5,
        'w_o': nrm(ks[14], (D_MODEL, D_MODEL), jnp.float32) * D_MODEL ** -0.5,
    }


def reference(x_prompt, x_sample, cache_kv_w128, cache_kv_w512, cache_kv_w2048, state_conv,
              norm_gain, w_in, q_norm_gain, k_norm_gain, rel_bias, conv_w, w_out_a, w_out_b, w_o):
    caches = (cache_kv_w128, cache_kv_w512, cache_kv_w2048)
    for _layer in range(DEPTH):
        q, k, v, ga, b, h, gb, ma, mb = layer_inputs(x_prompt, norm_gain, w_in, q_norm_gain, k_norm_gain)
        outs, lses = [], []
        for g in range(N_GROUPS):
            o_g, l_g = prompt_group(q[:, :, g], k[:, :, g], v[:, :, g], rel_bias[:, g], DILATIONS[g])
            outs.append(o_g)
            lses.append(l_g)
        hp = jnp.pad(h, ((0, 0), (CONV_W - 1, 0), (0, 0)))
        z = short_conv(hp, conv_w)
        y_prompt = layer_output(x_prompt, outs, lses, ga, b, z, gb, ma, mb, w_out_a, w_out_b, w_o)
        s_len = x_prompt.shape[1]
        kv_p = [jnp.stack([k[:, :, g], v[:, :, g]], axis=2)[:, s_len - min(WINDOWS[g], s_len):]
                for g in range(N_GROUPS)]
        conv_p = h[:, s_len - (CONV_W - 1):]

        qs, ks_, vs, gas, bs, hs, gbs, mas, mbs = layer_inputs(x_sample, norm_gain, w_in, q_norm_gain, k_norm_gain)
        outs_s, lses_s = [], []
        for g in range(N_GROUPS):
            o_g, l_g = sample_group(qs[:, :, g], ks_[:, :, g], vs[:, :, g], caches[g], rel_bias[:, g], DILATIONS[g])
            outs_s.append(o_g)
            lses_s.append(l_g)
        hps = jnp.concatenate([state_conv.astype(hs.dtype), hs], axis=1)
        zs = short_conv(hps, conv_w)
        y_sample = layer_output(x_sample, outs_s, lses_s, gas, bs, zs, gbs, mas, mbs, w_out_a, w_out_b, w_o)
        kv_s = [jnp.stack([ks_[:, :, g], vs[:, :, g]], axis=2) for g in range(N_GROUPS)]
        conv_s = hps[:, hps.shape[1] - (CONV_W - 1):]
    return (y_prompt, y_sample, kv_p[0], kv_p[1], kv_p[2], conv_p, kv_s[0], kv_s[1], kv_s[2], conv_s)
```

```python
import functools
import math

import numpy as np
import jax
import jax.numpy as jnp
from jax import lax
from jax.experimental import pallas as pl
from jax.experimental.pallas import tpu as pltpu

D_MODEL = 2048
N_GROUPS = 3
DILATIONS = (1, 4, 16)
N_BACK = 128
WINDOWS = (128, 512, 2048)
N_SLOTS = 8
HEAD_DIM = 128
D_ATTN = N_SLOTS * HEAD_DIM
QKV_COLS = N_GROUPS * D_ATTN
D_CONV = D_MODEL // 2
CONV_W = 3
BLOCK = N_BACK
NUM_BUCKETS = 32
MAX_EXACT = NUM_BUCKETS // 2
MAX_DISTANCE = 2048
EPS = 1e-6
SCALE = HEAD_DIM ** -0.5

COL_K = QKV_COLS
COL_V = 2 * QKV_COLS
COL_GATE_A = 3 * QKV_COLS
COL_B = COL_GATE_A + D_ATTN
COL_C = COL_B + D_CONV
COL_XIN = COL_C + D_CONV
COL_GATE_B = COL_XIN + D_CONV
COL_MERGE_A = COL_GATE_B + D_CONV
COL_MERGE_B = COL_MERGE_A + D_MODEL

SUBLANES = 8
LANES = 128
VMEM_LIMIT_BYTES = 56 * 1024 * 1024

BF16 = jnp.bfloat16
F32 = jnp.float32
NEG_INF = float("-inf")


def _params(semantics):
    return pltpu.CompilerParams(dimension_semantics=semantics, vmem_limit_bytes=VMEM_LIMIT_BYTES)


def _norm_kernel(x_ref, g_ref, o_ref):
    x = x_ref[...]
    r = lax.rsqrt(jnp.mean(x * x, axis=-1, keepdims=True) + EPS)
    o_ref[...] = ((x * r) * g_ref[...]).astype(o_ref.dtype)


def _pre_norm(x, gain):
    rows = x.shape[0]
    tm = min(rows, 512)
    return pl.pallas_call(
        _norm_kernel,
        grid=(rows // tm,),
        in_specs=[pl.BlockSpec((tm, D_MODEL), lambda m: (m, 0)),
                  pl.BlockSpec((1, D_MODEL), lambda m: (0, 0))],
        out_specs=pl.BlockSpec((tm, D_MODEL), lambda m: (m, 0)),
        out_shape=jax.ShapeDtypeStruct((rows, D_MODEL), BF16),
        compiler_params=_params(("arbitrary",)),
        name="pre_norm",
    )(x, gain.reshape(1, D_MODEL))


def _qkv_kernel(xn_ref, w_ref, gain_ref, o_ref, tail_ref, wbf_ref, *, tn, m_tail0):
    n = pl.program_id(0)
    m = pl.program_id(1)
    n_q = COL_K // tn
    n_qk = COL_V // tn

    @pl.when(m == 0)
    def _():
        wbf_ref[...] = w_ref[...].astype(BF16)

    p = jnp.dot(xn_ref[...], wbf_ref[...], preferred_element_type=F32)
    in_tail = jnp.logical_and(n >= n_q, m >= m_tail0)

    @pl.when(n < n_qk)
    def _():
        for hh in range(tn // HEAD_DIM):
            sl = slice(hh * HEAD_DIM, (hh + 1) * HEAD_DIM)
            ph = p[:, sl]
            r = lax.rsqrt(jnp.mean(ph * ph, axis=-1, keepdims=True) + EPS)
            res = (ph * r) * gain_ref[:, sl]
            o_ref[:, sl] = res.astype(o_ref.dtype)

            @pl.when(in_tail)
            def _():
                tail_ref[:, sl] = res

    @pl.when(n >= n_qk)
    def _():
        o_ref[...] = p.astype(o_ref.dtype)

        @pl.when(in_tail)
        def _():
            tail_ref[...] = p


def _qkv_proj(xn, w_in, qk_gain_row, out_dtype, tail_rows):
    rows = xn.shape[0]
    tm = min(rows, 1024)
    tn = 1024
    n_q = COL_K // tn
    m_tail0 = (rows - tail_rows) // tm

    def tail_map(n, m):
        is_kv = (n >= n_q).astype(jnp.int32)
        return (jnp.maximum(m - m_tail0, 0) * is_kv, jnp.maximum(n - n_q, 0))

    return pl.pallas_call(
        functools.partial(_qkv_kernel, tn=tn, m_tail0=m_tail0),
        grid=(3 * QKV_COLS // tn, rows // tm),
        in_specs=[pl.BlockSpec((tm, D_MODEL), lambda n, m: (m, 0)),
                  pl.BlockSpec((D_MODEL, tn), lambda n, m: (0, n)),
                  pl.BlockSpec((1, tn), lambda n, m: (0, jnp.minimum(n, COL_V // tn - 1)))],
        out_specs=[pl.BlockSpec((tm, tn), lambda n, m: (m, n)),
                   pl.BlockSpec((tm, tn), tail_map)],
        out_shape=[jax.ShapeDtypeStruct((rows, 3 * QKV_COLS), out_dtype),
                   jax.ShapeDtypeStruct((tail_rows, 2 * QKV_COLS), F32)],
        scratch_shapes=[pltpu.VMEM((D_MODEL, tn), BF16)],
        compiler_params=_params(("arbitrary", "arbitrary")),
        name="qkv_proj",
    )(xn, w_in, qk_gain_row)


def _branch_b_kernel(*refs, tm, per_row_state):
    if per_row_state:
        (xn_ref, wb_ref, wc_ref, wx_ref, wg_ref, cw_ref, hm2_ref, hm1_ref,
         y_ref, h_ref, wbf_ref) = refs
    else:
        xn_ref, wb_ref, wc_ref, wx_ref, wg_ref, cw_ref, y_ref, h_ref, wbf_ref, hbuf_ref = refs
    m = pl.program_id(1)

    @pl.when(m == 0)
    def _():
        for i, w_ref in enumerate((wb_ref, wc_ref, wx_ref, wg_ref)):
            wbf_ref[i] = w_ref[...].astype(BF16)
        if not per_row_state:
            hbuf_ref[0:SUBLANES, :] = jnp.zeros((SUBLANES, hbuf_ref.shape[1]), F32)

    xn = xn_ref[...]
    pb = jnp.dot(xn, wbf_ref[0], preferred_element_type=F32)
    pc = jnp.dot(xn, wbf_ref[1], preferred_element_type=F32)
    px = jnp.dot(xn, wbf_ref[2], preferred_element_type=F32)
    pg = jnp.dot(xn, wbf_ref[3], preferred_element_type=F32)
    h = pc * px
    if per_row_state:
        hm2 = hm2_ref[...]
        hm1 = hm1_ref[...]
        h_ref[...] = h
    else:
        hbuf_ref[SUBLANES:SUBLANES + tm, :] = h
        hm2 = hbuf_ref[pl.ds(SUBLANES - 2, tm), :]
        hm1 = hbuf_ref[pl.ds(SUBLANES - 1, tm), :]
        last = hbuf_ref[pl.ds(tm, SUBLANES), :]
        h_ref[...] = last
        hbuf_ref[0:SUBLANES, :] = last
    z = cw_ref[0:1, :] * hm2 + cw_ref[1:2, :] * hm1 + cw_ref[2:3, :] * h
    y_ref[...] = (pb * z * (pg * jax.nn.sigmoid(pg))).astype(y_ref.dtype)


def _branch_b(xn, w_in, conv_w, state=None):
    rows = xn.shape[0]
    tm = min(rows, 1024)
    tnb = 256
    per_row_state = state is not None

    def wspec(col0):
        return pl.BlockSpec((D_MODEL, tnb), lambda j, m: (0, col0 // tnb + j))

    in_specs = [pl.BlockSpec((tm, D_MODEL), lambda j, m: (m, 0)),
                wspec(COL_B), wspec(COL_C), wspec(COL_XIN), wspec(COL_GATE_B),
                pl.BlockSpec((CONV_W, tnb), lambda j, m: (0, j))]
    args = [xn, w_in, w_in, w_in, w_in, conv_w]
    scratch = [pltpu.VMEM((4, D_MODEL, tnb), BF16)]
    if per_row_state:
        st = state.reshape(rows, (CONV_W - 1) * D_CONV)
        in_specs += [pl.BlockSpec((tm, tnb), lambda j, m: (m, j)),
                     pl.BlockSpec((tm, tnb), lambda j, m: (m, D_CONV // tnb + j))]
        args += [st, st]
        h_spec = pl.BlockSpec((tm, tnb), lambda j, m: (m, j))
        h_shape = jax.ShapeDtypeStruct((rows, D_CONV), F32)
    else:
        scratch.append(pltpu.VMEM((tm + SUBLANES, tnb), F32))
        h_spec = pl.BlockSpec((SUBLANES, tnb), lambda j, m: (0, j))
        h_shape = jax.ShapeDtypeStruct((SUBLANES, D_CONV), F32)
    return pl.pallas_call(
        functools.partial(_branch_b_kernel, tm=tm, per_row_state=per_row_state),
        grid=(D_CONV // tnb, rows // tm),
        in_specs=in_specs,
        out_specs=[pl.BlockSpec((tm, tnb), lambda j, m: (m, j)), h_spec],
        out_shape=[jax.ShapeDtypeStruct((rows, D_CONV), BF16), h_shape],
        scratch_shapes=scratch,
        compiler_params=_params(("arbitrary", "arbitrary")),
        name="branch_b",
    )(*args)


def _gates_kernel(xn_ref, w_ref, o_ref, wbf_ref, *, n_silu):
    n = pl.program_id(0)
    m = pl.program_id(1)

    @pl.when(m == 0)
    def _():
        wbf_ref[...] = w_ref[...].astype(BF16)

    p = jnp.dot(xn_ref[...], wbf_ref[...], preferred_element_type=F32)
    s = jax.nn.sigmoid(p)

    @pl.when(n == n_silu)
    def _():
        o_ref[...] = p * s

    @pl.when(n != n_silu)
    def _():
        o_ref[...] = s


def _gates(xn, w_in):
    rows = xn.shape[0]
    tm = min(rows, 1024)
    tn = 1024
    n_tiles = (D_ATTN + 2 * D_MODEL) // tn
    n_silu = n_tiles - 1

    def wmap(n, m):
        return (0, jnp.where(n == n_silu, COL_GATE_A // tn, COL_MERGE_A // tn + n))

    return pl.pallas_call(
        functools.partial(_gates_kernel, n_silu=n_silu),
        grid=(n_tiles, rows // tm),
        in_specs=[pl.BlockSpec((tm, D_MODEL), lambda n, m: (m, 0)),
                  pl.BlockSpec((D_MODEL, tn), wmap)],
        out_specs=pl.BlockSpec((tm, tn), lambda n, m: (m, n)),
        out_shape=jax.ShapeDtypeStruct((rows, n_tiles * tn), F32),
        scratch_shapes=[pltpu.VMEM((D_MODEL, tn), BF16)],
        compiler_params=_params(("arbitrary", "arbitrary")),
        name="gates",
    )(xn, w_in)


def _t5_bucket(dist):
    d = np.maximum(dist, 1).astype(np.float32)
    large = MAX_EXACT + (np.log(d / np.float32(MAX_EXACT)) / np.float32(math.log(MAX_DISTANCE / MAX_EXACT))
                         * np.float32(NUM_BUCKETS - MAX_EXACT)).astype(np.int32)
    large = np.minimum(large, NUM_BUCKETS - 1)
    return np.where(dist < MAX_EXACT, dist, large)


def _prompt_bias(rel_bias):
    qq = np.arange(BLOCK)[:, None]
    kk = np.arange(2 * BLOCK)[None, :]
    delta = BLOCK + qq - kk
    valid = (delta >= 0) & (delta <= N_BACK)
    valid_first = valid & (kk >= BLOCK)
    idx = np.stack([_t5_bucket(d * np.maximum(delta, 0)) for d in DILATIONS])
    g_idx = np.arange(N_GROUPS)[:, None, None]
    bias = rel_bias[idx, g_idx]
    bias = jnp.transpose(bias, (0, 3, 1, 2)).astype(F32)
    mask = np.stack([valid_first, valid])[None, :, None]
    return jnp.where(mask, bias[:, None], NEG_INF)


def _sample_bias(rel_bias):
    kk = N_BACK - np.arange(N_BACK)
    idx = np.stack([_t5_bucket(d * kk) for d in DILATIONS])
    g_idx = np.arange(N_GROUPS)[:, None]
    cache_bias = jnp.transpose(rel_bias[idx, g_idx], (0, 2, 1)).astype(F32)
    new_bias = rel_bias[0][:, :, None].astype(F32)
    return cache_bias, new_bias


def _attn_prompt_kernel(q_ref, kp_ref, kc_ref, vp_ref, vc_ref, bias_ref, o_ref, lse_ref):
    nt = (((1,), (1,)), ((), ()))
    lane = lax.broadcasted_iota(jnp.int32, (BLOCK, LANES), 1)
    lse_all = jnp.zeros((BLOCK, LANES), F32)
    for h in range(N_SLOTS):
        sl = slice(h * HEAD_DIM, (h + 1) * HEAD_DIM)
        q = q_ref[:, sl]
        sc_p = lax.dot_general(q, kp_ref[:, sl], nt, preferred_element_type=F32) * SCALE + bias_ref[0, 0, h, :, :BLOCK]
        sc_c = lax.dot_general(q, kc_ref[:, sl], nt, preferred_element_type=F32) * SCALE + bias_ref[0, 0, h, :, BLOCK:]
        mx = jnp.maximum(jnp.max(sc_p, axis=-1, keepdims=True), jnp.max(sc_c, axis=-1, keepdims=True))
        p_p = jnp.exp(sc_p - mx)
        p_c = jnp.exp(sc_c - mx)
        l = jnp.sum(p_p, axis=-1, keepdims=True) + jnp.sum(p_c, axis=-1, keepdims=True)
        o = (jnp.dot(p_p.astype(BF16), vp_ref[:, sl], preferred_element_type=F32)
             + jnp.dot(p_c.astype(BF16), vc_ref[:, sl], preferred_element_type=F32))
        o_ref[:, sl] = o / l
        lse_all = jnp.where(lane == h, mx + jnp.log(l), lse_all)
    lse_ref[...] = lse_all


def _attn_prompt(qkv, bias, g):
    s = qkv.shape[0]
    dil = DILATIONS[g]
    ls = s // dil
    nb = ls // BLOCK
    cols = 3 * QKV_COLS // D_ATTN
    view = qkv.reshape(ls, dil * 3 * QKV_COLS)

    def spec(col_block, prev):
        if prev:
            return pl.BlockSpec((BLOCK, D_ATTN), lambda r, i: (jnp.maximum(i - 1, 0), r * cols + col_block))
        return pl.BlockSpec((BLOCK, D_ATTN), lambda r, i: (i, r * cols + col_block))

    o, lse = pl.pallas_call(
        _attn_prompt_kernel,
        grid=(dil, nb),
        in_specs=[spec(g, False),
                  spec(N_GROUPS + g, True), spec(N_GROUPS + g, False),
                  spec(2 * N_GROUPS + g, True), spec(2 * N_GROUPS + g, False),
                  pl.BlockSpec((1, 1, N_SLOTS, BLOCK, 2 * BLOCK), lambda r, i: (g, jnp.minimum(i, 1), 0, 0, 0))],
        out_specs=[pl.BlockSpec((BLOCK, D_ATTN), lambda r, i: (i, r)),
                   pl.BlockSpec((BLOCK, LANES), lambda r, i: (i, r))],
        out_shape=[jax.ShapeDtypeStruct((ls, dil * D_ATTN), F32),
                   jax.ShapeDtypeStruct((ls, dil * LANES), F32)],
        compiler_params=_params(("arbitrary", "arbitrary")),
        name=f"attn_prompt_g{g}",
    )(view, view, view, view, view, bias)
    return o.reshape(s, D_ATTN), lse.reshape(s, LANES)


def _attn_sample_kernel(qkv_ref, c0_ref, c1_ref, c2_ref, cbias_ref, nbias_ref, o_ref, *, tb):
    step = pl.program_id(0)
    nt = (((1,), (1,)), ((), ()))
    head_of_lane = lax.broadcasted_iota(jnp.int32, (N_SLOTS, D_ATTN), 1) // HEAD_DIM
    head_row = lax.broadcasted_iota(jnp.int32, (N_SLOTS, D_ATTN), 0)
    diag = head_of_lane == head_row
    for t in range(tb):
        row = step * tb + t
        outs, lses = [], []
        for g, c_ref in enumerate((c0_ref, c1_ref, c2_ref)):
            q = qkv_ref[pl.ds(row, 1), g * D_ATTN:(g + 1) * D_ATTN]
            k_new = qkv_ref[pl.ds(row, 1), COL_K + g * D_ATTN:COL_K + (g + 1) * D_ATTN]
            v_new = qkv_ref[pl.ds(row, 1), COL_V + g * D_ATTN:COL_V + (g + 1) * D_ATTN]
            q_bd = jnp.where(diag, jnp.broadcast_to(q, (N_SLOTS, D_ATTN)), 0.0).astype(BF16)
            k_c = c_ref[t, :, 0:D_ATTN].astype(BF16)
            v_c = c_ref[t, :, D_ATTN:2 * D_ATTN].astype(BF16)
            k_n = jnp.broadcast_to(k_new, (N_SLOTS, D_ATTN)).astype(BF16)
            s_c = lax.dot_general(q_bd, k_c, nt, preferred_element_type=F32)
            s_n = lax.dot_general(q_bd, k_n, nt, preferred_element_type=F32)[:, 0:1]
            sc_c = s_c * SCALE + cbias_ref[g]
            sc_n = s_n * SCALE + nbias_ref[g]
            mx = jnp.maximum(jnp.max(sc_c, axis=-1, keepdims=True), sc_n)
            p_c = jnp.exp(sc_c - mx)
            p_n = jnp.exp(sc_n - mx)
            l = jnp.sum(p_c, axis=-1, keepdims=True) + p_n
            v_n = jnp.broadcast_to(v_new, (N_SLOTS, D_ATTN)).astype(BF16).astype(F32)
            o = jnp.dot(p_c.astype(BF16), v_c, preferred_element_type=F32) + p_n.astype(BF16).astype(F32) * v_n
            outs.append(o / l)
            lses.append(mx + jnp.log(l))
        top = jnp.maximum(jnp.maximum(lses[0], lses[1]), lses[2])
        es = [jnp.exp(x - top) for x in lses]
        tot = es[0] + es[1] + es[2]
        comb = (es[0] / tot) * outs[0] + (es[1] / tot) * outs[1] + (es[2] / tot) * outs[2]
        o_ref[pl.ds(row, 1), :] = jnp.sum(jnp.where(diag, comb, 0.0), axis=0, keepdims=True)


def _attn_sample(qkv_s, caches, cbias, nbias):
    db = qkv_s.shape[0]
    tb = 4
    views, specs = [], []
    for g, c in enumerate(caches):
        dil = DILATIONS[g]
        lc = c.shape[1]
        views.append(c.reshape(db, lc // dil, dil * 2 * D_ATTN))
        specs.append(pl.BlockSpec((tb, N_BACK, 2 * D_ATTN), lambda s: (s, 0, 0)))
    return pl.pallas_call(
        functools.partial(_attn_sample_kernel, tb=tb),
        grid=(db // tb,),
        in_specs=[pl.BlockSpec((db, 3 * QKV_COLS), lambda s: (0, 0))] + specs
                 + [pl.BlockSpec((N_GROUPS, N_SLOTS, N_BACK), lambda s: (0, 0, 0)),
                    pl.BlockSpec((N_GROUPS, N_SLOTS, 1), lambda s: (0, 0, 0))],
        out_specs=pl.BlockSpec((db, D_ATTN), lambda s: (0, 0)),
        out_shape=jax.ShapeDtypeStruct((db, D_ATTN), F32),
        compiler_params=_params(("arbitrary",)),
        name="attn_sample",
    )(qkv_s, *views, cbias, nbias)


def _out_kernel(*refs, n_groups):
    o_refs = refs[:n_groups]
    lse_refs = refs[n_groups:2 * n_groups] if n_groups > 1 else ()
    rest = refs[len(o_refs) + len(lse_refs):]
    sga_ref, sma_ref, smb_ref, yb_ref, x_ref, woa_ref, wob_ref, wo_ref, y_ref, a_ref = rest

    if n_groups > 1:
        lses = [r[...] for r in lse_refs]
        top = functools.reduce(jnp.maximum, lses)
        es = [jnp.exp(x - top) for x in lses]
        tot = functools.reduce(lambda a, b: a + b, es)
        ws = [e / tot for e in es]
        for h in range(N_SLOTS):
            sl = slice(h * HEAD_DIM, (h + 1) * HEAD_DIM)
            o = functools.reduce(lambda a, b: a + b,
                                 [w[:, h:h + 1] * r[:, sl] for w, r in zip(ws, o_refs)])
            a_ref[:, sl] = (o * sga_ref[:, sl]).astype(BF16)
    else:
        a_ref[...] = (o_refs[0][...] * sga_ref[...]).astype(BF16)
    ya = jnp.dot(a_ref[...], woa_ref[...], preferred_element_type=F32)
    yb = jnp.dot(yb_ref[...], wob_ref[...], preferred_element_type=F32)
    merged = sma_ref[...] * ya + smb_ref[...] * yb
    y_ref[...] = x_ref[...] + jnp.dot(merged.astype(BF16), wo_ref[...], preferred_element_type=F32)


def _out_proj(os_, lses, gates, yb_in, x, w_out_a, w_out_b, w_o):
    rows = x.shape[0]
    tm = min(rows, 256)
    n_groups = len(os_)

    def row_spec(width, col_block=0):
        return pl.BlockSpec((tm, width), lambda m: (m, col_block))

    def const_spec(shape):
        return pl.BlockSpec(shape, lambda m: (0, 0), pipeline_mode=pl.Buffered(1))

    in_specs = [row_spec(D_ATTN) for _ in os_] + [row_spec(LANES) for _ in lses]
    in_specs += [row_spec(D_ATTN, 2 * D_MODEL // D_ATTN),
                 row_spec(D_MODEL, 0), row_spec(D_MODEL, 1),
                 row_spec(D_CONV), row_spec(D_MODEL),
                 const_spec((D_ATTN, D_MODEL)), const_spec((D_CONV, D_MODEL)), const_spec((D_MODEL, D_MODEL))]
    sga = sma = smb = gates
    return pl.pallas_call(
        functools.partial(_out_kernel, n_groups=n_groups),
        grid=(rows // tm,),
        in_specs=in_specs,
        out_specs=row_spec(D_MODEL),
        out_shape=jax.ShapeDtypeStruct((rows, D_MODEL), F32),
        scratch_shapes=[pltpu.VMEM((tm, D_ATTN), BF16)],
        compiler_params=_params(("arbitrary",)),
        name="out_proj",
    )(*os_, *lses, sga, sma, smb, yb_in, x, w_out_a, w_out_b, w_o)


def kernel(x_prompt, x_sample, cache_kv_w128, cache_kv_w512, cache_kv_w2048, state_conv, norm_gain, w_in,
           q_norm_gain, k_norm_gain, rel_bias, conv_w, w_out_a, w_out_b, w_o):
    seq = x_prompt.shape[1]
    db = x_sample.shape[0]
    xp = x_prompt.reshape(seq, D_MODEL)
    xs = x_sample.reshape(db, D_MODEL)
    caches = (cache_kv_w128, cache_kv_w512, cache_kv_w2048)
    qk_gain_row = jnp.concatenate([jnp.tile(q_norm_gain, QKV_COLS // HEAD_DIM),
                                   jnp.tile(k_norm_gain, QKV_COLS // HEAD_DIM)]).reshape(1, 2 * QKV_COLS)
    woa, wob, wo = w_out_a.astype(BF16), w_out_b.astype(BF16), w_o.astype(BF16)
    bias_p = _prompt_bias(rel_bias)
    cbias, nbias = _sample_bias(rel_bias)

    def kv_out(tail, g, w):
        r0 = tail.shape[0] - w
        k = tail[r0:, g * D_ATTN:(g + 1) * D_ATTN]
        v = tail[r0:, QKV_COLS + g * D_ATTN:QKV_COLS + (g + 1) * D_ATTN]
        return jnp.stack([k, v], axis=1).reshape(w, 2, N_SLOTS, HEAD_DIM)

    xn = _pre_norm(xp, norm_gain)
    tail_rows = min(max(WINDOWS), seq)
    qkv, kv_tail = _qkv_proj(xn, w_in, qk_gain_row, BF16, tail_rows)
    yb_in, h_last = _branch_b(xn, w_in, conv_w)
    gates = _gates(xn, w_in)
    os_, lses = zip(*[_attn_prompt(qkv, bias_p, g) for g in range(N_GROUPS)])
    y_prompt = _out_proj(os_, lses, gates, yb_in, xp, woa, wob, wo).reshape(1, seq, D_MODEL)
    kv_p = [kv_out(kv_tail, g, min(WINDOWS[g], seq))[None] for g in range(N_GROUPS)]
    conv_p = h_last[SUBLANES - (CONV_W - 1):][None]

    xn_s = _pre_norm(xs, norm_gain)
    qkv_s, kv_tail_s = _qkv_proj(xn_s, w_in, qk_gain_row, F32, db)
    yb_in_s, h_s = _branch_b(xn_s, w_in, conv_w, state=state_conv)
    gates_s = _gates(xn_s, w_in)
    o_s = _attn_sample(qkv_s, caches, cbias, nbias)
    y_sample = _out_proj((o_s,), (), gates_s, yb_in_s, xs, woa, wob, wo).reshape(db, 1, D_MODEL)
    kv_s = [kv_out(kv_tail_s, g, db).reshape(db, 1, 2, N_SLOTS, HEAD_DIM) for g in range(N_GROUPS)]
    conv_s = jnp.stack([state_conv[:, CONV_W - 2], h_s], axis=1)

    return (y_prompt, y_sample, kv_p[0], kv_p[1], kv_p[2], conv_p, kv_s[0], kv_s[1], kv_s[2], conv_s)
```

```python
import functools
import math

import numpy as np
import jax
import jax.numpy as jnp
from jax import lax
from jax.experimental import pallas as pl
from jax.experimental.pallas import tpu as pltpu

D_MODEL = 2048
N_GROUPS = 3
DILATIONS = (1, 4, 16)
N_BACK = 128
WINDOWS = (128, 512, 2048)
N_SLOTS = 8
HEAD_DIM = 128
D_ATTN = N_SLOTS * HEAD_DIM
QKV_COLS = N_GROUPS * D_ATTN
D_CONV = D_MODEL // 2
CONV_W = 3
BLOCK = N_BACK
NUM_BUCKETS = 32
MAX_EXACT = NUM_BUCKETS // 2
MAX_DISTANCE = 2048
EPS = 1e-6
SCALE = HEAD_DIM ** -0.5

COL_K = QKV_COLS
COL_V = 2 * QKV_COLS
COL_GATE_A = 3 * QKV_COLS
COL_B = COL_GATE_A + D_ATTN
COL_C = COL_B + D_CONV
COL_XIN = COL_C + D_CONV
COL_GATE_B = COL_XIN + D_CONV
COL_MERGE_A = COL_GATE_B + D_CONV
COL_MERGE_B = COL_MERGE_A + D_MODEL

SUBLANES = 8
LANES = 128
VMEM_LIMIT_BYTES = 56 * 1024 * 1024

BF16 = jnp.bfloat16
F32 = jnp.float32
NEG_INF = float("-inf")


def _params(semantics):
    return pltpu.CompilerParams(dimension_semantics=semantics, vmem_limit_bytes=VMEM_LIMIT_BYTES)


def _cast_kernel(w_ref, o_ref):
    o_ref[...] = w_ref[...].astype(o_ref.dtype)


def _to_bf16(w):
    rows, cols = w.shape
    tn = 512
    return pl.pallas_call(
        _cast_kernel,
        grid=(cols // tn,),
        in_specs=[pl.BlockSpec((rows, tn), lambda n: (0, n))],
        out_specs=pl.BlockSpec((rows, tn), lambda n: (0, n)),
        out_shape=jax.ShapeDtypeStruct((rows, cols), BF16),
        compiler_params=_params(("arbitrary",)),
        name="to_bf16",
    )(w)


def _norm_kernel(x_ref, g_ref, o_ref):
    x = x_ref[...]
    r = lax.rsqrt(jnp.mean(x * x, axis=-1, keepdims=True) + EPS)
    o_ref[...] = ((x * r) * g_ref[...]).astype(o_ref.dtype)


def _pre_norm(x, gain):
    rows = x.shape[0]
    tm = min(rows, 512)
    return pl.pallas_call(
        _norm_kernel,
        grid=(rows // tm,),
        in_specs=[pl.BlockSpec((tm, D_MODEL), lambda m: (m, 0)),
                  pl.BlockSpec((1, D_MODEL), lambda m: (0, 0))],
        out_specs=pl.BlockSpec((tm, D_MODEL), lambda m: (m, 0)),
        out_shape=jax.ShapeDtypeStruct((rows, D_MODEL), BF16),
        compiler_params=_params(("arbitrary",)),
        name="pre_norm",
    )(x, gain.reshape(1, D_MODEL))


def _qkv_kernel(xn_ref, w_ref, gain_ref, o_ref, tail_ref, res_ref, *, dil, tm, tr, m_tail0):
    kind = pl.program_id(0)
    m = pl.program_id(1)
    p = jnp.dot(xn_ref[...], w_ref[...], preferred_element_type=F32)
    heads = [slice(h * HEAD_DIM, (h + 1) * HEAD_DIM) for h in range(N_SLOTS)]

    @pl.when(kind < 2)
    def _():
        for h, sl in enumerate(heads):
            ph = p[:, sl]
            r = lax.rsqrt(jnp.mean(ph * ph, axis=-1, keepdims=True) + EPS)
            res_ref[h] = (ph * r) * gain_ref[:, sl]

    @pl.when(kind == 2)
    def _():
        for h, sl in enumerate(heads):
            res_ref[h] = p[:, sl]

    sub = tm // dil
    for r in range(dil):
        for h, sl in enumerate(heads):
            rows = res_ref[h] if dil == 1 else res_ref[h, pl.ds(r, sub, stride=dil), :]
            o_ref[r, :, sl] = rows.astype(o_ref.dtype)

    @pl.when(jnp.logical_and(kind >= 1, m >= m_tail0))
    def _():
        for h in range(N_SLOTS):
            tail_ref[:, h, :] = res_ref[h, tm - tr:tm, :]


def _qkv_proj(xn, w_in, qk_gains, g, dil, out_dtype, tail_rows):
    rows = xn.shape[0]
    tm = min(rows, 1024)
    tn = D_ATTN
    tr = min(tail_rows, tm)
    m_tail0 = rows // tm - tail_rows // tr

    def tail_map(kind, m):
        return (jnp.where(kind >= 1, jnp.maximum(m - m_tail0, 0), 0), jnp.maximum(kind - 1, 0), 0, 0)

    return pl.pallas_call(
        functools.partial(_qkv_kernel, dil=dil, tm=tm, tr=tr, m_tail0=m_tail0),
        grid=(3, rows // tm),
        in_specs=[pl.BlockSpec((tm, D_MODEL), lambda kind, m: (m, 0)),
                  pl.BlockSpec((D_MODEL, tn), lambda kind, m: (0, kind * N_GROUPS + g)),
                  pl.BlockSpec((None, 1, tn), lambda kind, m: (jnp.minimum(kind, 1), 0, 0))],
        out_specs=[pl.BlockSpec((None, dil, tm // dil, tn), lambda kind, m: (kind, 0, m, 0)),
                   pl.BlockSpec((tr, None, N_SLOTS, HEAD_DIM), tail_map)],
        out_shape=[jax.ShapeDtypeStruct((3, dil, rows // dil, tn), out_dtype),
                   jax.ShapeDtypeStruct((tail_rows, 2, N_SLOTS, HEAD_DIM), F32)],
        scratch_shapes=[pltpu.VMEM((N_SLOTS, tm, HEAD_DIM), F32)],
        compiler_params=_params(("arbitrary", "arbitrary")),
        name=f"qkv_proj_g{g}",
    )(xn, w_in, qk_gains)


def _branch_b_kernel(*refs, tm, per_row_state):
    if per_row_state:
        xn_ref, wb_ref, wc_ref, wx_ref, wg_ref, cw_ref, hm2_ref, hm1_ref, y_ref, h_ref = refs
    else:
        xn_ref, wb_ref, wc_ref, wx_ref, wg_ref, cw_ref, y_ref, h_ref, hbuf_ref = refs

        @pl.when(pl.program_id(1) == 0)
        def _():
            hbuf_ref[0:SUBLANES, :] = jnp.zeros((SUBLANES, hbuf_ref.shape[1]), F32)

    xn = xn_ref[...]
    pb = jnp.dot(xn, wb_ref[...], preferred_element_type=F32)
    pc = jnp.dot(xn, wc_ref[...], preferred_element_type=F32)
    px = jnp.dot(xn, wx_ref[...], preferred_element_type=F32)
    pg = jnp.dot(xn, wg_ref[...], preferred_element_type=F32)
    h = pc * px
    if per_row_state:
        hm2 = hm2_ref[...]
        hm1 = hm1_ref[...]
        h_ref[...] = h
    else:
        hbuf_ref[SUBLANES:SUBLANES + tm, :] = h
        hm2 = hbuf_ref[pl.ds(SUBLANES - 2, tm), :]
        hm1 = hbuf_ref[pl.ds(SUBLANES - 1, tm), :]
        last = hbuf_ref[pl.ds(tm, SUBLANES), :]
        h_ref[...] = last
        hbuf_ref[0:SUBLANES, :] = last
    z = cw_ref[0:1, :] * hm2 + cw_ref[1:2, :] * hm1 + cw_ref[2:3, :] * h
    y_ref[...] = (pb * z * (pg * jax.nn.sigmoid(pg))).astype(y_ref.dtype)


def _branch_b(xn, w_in, conv_w, state=None):
    rows = xn.shape[0]
    tm = min(rows, 1024)
    tnb = 256
    per_row_state = state is not None

    def wspec(col0):
        return pl.BlockSpec((D_MODEL, tnb), lambda j, m: (0, col0 // tnb + j))

    in_specs = [pl.BlockSpec((tm, D_MODEL), lambda j, m: (m, 0)),
                wspec(COL_B), wspec(COL_C), wspec(COL_XIN), wspec(COL_GATE_B),
                pl.BlockSpec((CONV_W, tnb), lambda j, m: (0, j))]
    args = [xn, w_in, w_in, w_in, w_in, conv_w]
    scratch = []
    if per_row_state:
        st = state.reshape(rows, (CONV_W - 1) * D_CONV)
        in_specs += [pl.BlockSpec((tm, tnb), lambda j, m: (m, j)),
                     pl.BlockSpec((tm, tnb), lambda j, m: (m, D_CONV // tnb + j))]
        args += [st, st]
        h_spec = pl.BlockSpec((tm, tnb), lambda j, m: (m, j))
        h_shape = jax.ShapeDtypeStruct((rows, D_CONV), F32)
    else:
        scratch.append(pltpu.VMEM((tm + SUBLANES, tnb), F32))
        h_spec = pl.BlockSpec((SUBLANES, tnb), lambda j, m: (0, j))
        h_shape = jax.ShapeDtypeStruct((SUBLANES, D_CONV), F32)
    return pl.pallas_call(
        functools.partial(_branch_b_kernel, tm=tm, per_row_state=per_row_state),
        grid=(D_CONV // tnb, rows // tm),
        in_specs=in_specs,
        out_specs=[pl.BlockSpec((tm, tnb), lambda j, m: (m, j)), h_spec],
        out_shape=[jax.ShapeDtypeStruct((rows, D_CONV), BF16), h_shape],
        scratch_shapes=scratch,
        compiler_params=_params(("arbitrary", "arbitrary")),
        name="branch_b",
    )(*args)


def _gates_kernel(xn_ref, w_ref, o_ref, *, n_silu):
    n = pl.program_id(0)
    p = jnp.dot(xn_ref[...], w_ref[...], preferred_element_type=F32)
    s = jax.nn.sigmoid(p)

    @pl.when(n == n_silu)
    def _():
        o_ref[...] = p * s

    @pl.when(n != n_silu)
    def _():
        o_ref[...] = s


def _gates(xn, w_in):
    rows = xn.shape[0]
    tm = min(rows, 1024)
    tn = 1024
    n_tiles = (D_ATTN + 2 * D_MODEL) // tn
    n_silu = n_tiles - 1

    def wmap(n, m):
        return (0, jnp.where(n == n_silu, COL_GATE_A // tn, COL_MERGE_A // tn + n))

    return pl.pallas_call(
        functools.partial(_gates_kernel, n_silu=n_silu),
        grid=(n_tiles, rows // tm),
        in_specs=[pl.BlockSpec((tm, D_MODEL), lambda n, m: (m, 0)),
                  pl.BlockSpec((D_MODEL, tn), wmap)],
        out_specs=pl.BlockSpec((tm, tn), lambda n, m: (m, n)),
        out_shape=jax.ShapeDtypeStruct((rows, n_tiles * tn), F32),
        compiler_params=_params(("arbitrary", "arbitrary")),
        name="gates",
    )(xn, w_in)


def _t5_bucket(dist):
    d = np.maximum(dist, 1).astype(np.float32)
    large = MAX_EXACT + (np.log(d / np.float32(MAX_EXACT)) / np.float32(math.log(MAX_DISTANCE / MAX_EXACT))
                         * np.float32(NUM_BUCKETS - MAX_EXACT)).astype(np.int32)
    large = np.minimum(large, NUM_BUCKETS - 1)
    return np.where(dist < MAX_EXACT, dist, large)


def _bias_by_offset(rel_bias):
    steps = N_BACK - np.arange(N_BACK + 1)
    idx = np.stack([_t5_bucket(d * steps) for d in DILATIONS])
    g_idx = np.arange(N_GROUPS)[:, None]
    vals = jnp.transpose(rel_bias[idx, g_idx], (0, 2, 1)).astype(F32)
    return jnp.pad(vals, ((0, 0), (0, 0), (0, 2 * BLOCK - N_BACK - 1)), constant_values=NEG_INF)


def _attn_prompt_kernel(cvec_ref, q_ref, kp_ref, kc_ref, vp_ref, vc_ref, o_ref, lse_ref, bias_ref):
    first = jnp.logical_and(pl.program_id(0) == 0, pl.program_id(1) == 0)
    has_prev = pl.program_id(1) > 0

    @pl.when(first)
    def _():
        for h in range(N_SLOTS):
            row = jnp.broadcast_to(cvec_ref[h:h + 1, :], (BLOCK, 2 * BLOCK))
            bias_ref[h] = pltpu.roll(row, 0, 1, stride=1, stride_axis=0)

    nt = (((1,), (1,)), ((), ()))
    lane = lax.broadcasted_iota(jnp.int32, (BLOCK, LANES), 1)
    lse_all = jnp.zeros((BLOCK, LANES), F32)
    for h in range(N_SLOTS):
        sl = slice(h * HEAD_DIM, (h + 1) * HEAD_DIM)
        q = q_ref[:, sl]
        sc_p = lax.dot_general(q, kp_ref[:, sl], nt, preferred_element_type=F32) * SCALE + bias_ref[h, :, :BLOCK]
        sc_p = jnp.where(has_prev, sc_p, NEG_INF)
        sc_c = lax.dot_general(q, kc_ref[:, sl], nt, preferred_element_type=F32) * SCALE + bias_ref[h, :, BLOCK:]
        mx = jnp.maximum(jnp.max(sc_p, axis=-1, keepdims=True), jnp.max(sc_c, axis=-1, keepdims=True))
        p_p = jnp.exp(sc_p - mx)
        p_c = jnp.exp(sc_c - mx)
        l = jnp.sum(p_p, axis=-1, keepdims=True) + jnp.sum(p_c, axis=-1, keepdims=True)
        o = (jnp.dot(p_p.astype(BF16), vp_ref[:, sl], preferred_element_type=F32)
             + jnp.dot(p_c.astype(BF16), vc_ref[:, sl], preferred_element_type=F32))
        o_ref[:, sl] = o / l
        lse_all = jnp.where(lane == h, mx + jnp.log(l), lse_all)
    lse_ref[...] = lse_all


def _attn_prompt(qkv, cvec, g):
    _, dil, ls, _ = qkv.shape
    nb = ls // BLOCK

    def spec(kind, prev):
        if prev:
            return pl.BlockSpec((None, None, BLOCK, D_ATTN), lambda r, i: (kind, r, jnp.maximum(i - 1, 0), 0))
        return pl.BlockSpec((None, None, BLOCK, D_ATTN), lambda r, i: (kind, r, i, 0))

    return pl.pallas_call(
        _attn_prompt_kernel,
        grid=(dil, nb),
        in_specs=[pl.BlockSpec((None, N_SLOTS, 2 * BLOCK), lambda r, i: (g, 0, 0)),
                  spec(0, False), spec(1, True), spec(1, False), spec(2, True), spec(2, False)],
        out_specs=[pl.BlockSpec((None, BLOCK, D_ATTN), lambda r, i: (r, i, 0)),
                   pl.BlockSpec((None, BLOCK, LANES), lambda r, i: (r, i, 0))],
        out_shape=[jax.ShapeDtypeStruct((dil, ls, D_ATTN), F32),
                   jax.ShapeDtypeStruct((dil, ls, LANES), F32)],
        scratch_shapes=[pltpu.VMEM((N_SLOTS, BLOCK, 2 * BLOCK), F32)],
        compiler_params=_params(("arbitrary", "arbitrary")),
        name=f"attn_prompt_g{g}",
    )(cvec, qkv, qkv, qkv, qkv, qkv)


def _attn_sample_kernel(cvec_ref, q0_ref, q1_ref, q2_ref, c0_ref, c1_ref, c2_ref, o_ref, *, tb):
    step = pl.program_id(0)
    nt = (((1,), (1,)), ((), ()))
    head_of_lane = lax.broadcasted_iota(jnp.int32, (N_SLOTS, D_ATTN), 1) // HEAD_DIM
    head_row = lax.broadcasted_iota(jnp.int32, (N_SLOTS, D_ATTN), 0)
    diag = head_of_lane == head_row
    for t in range(tb):
        row = step * tb + t
        outs, lses = [], []
        for g, (qkv_ref, c_ref) in enumerate(((q0_ref, c0_ref), (q1_ref, c1_ref), (q2_ref, c2_ref))):
            q = qkv_ref[0, 0, pl.ds(row, 1), :]
            k_new = qkv_ref[1, 0, pl.ds(row, 1), :]
            v_new = qkv_ref[2, 0, pl.ds(row, 1), :]
            q_bd = jnp.where(diag, jnp.broadcast_to(q, (N_SLOTS, D_ATTN)), 0.0).astype(BF16)
            k_c = jnp.concatenate([c_ref[t, :, 0, h, :] for h in range(N_SLOTS)], axis=1).astype(BF16)
            v_c = jnp.concatenate([c_ref[t, :, 1, h, :] for h in range(N_SLOTS)], axis=1).astype(BF16)
            k_n = jnp.broadcast_to(k_new, (N_SLOTS, D_ATTN)).astype(BF16)
            s_c = lax.dot_general(q_bd, k_c, nt, preferred_element_type=F32)
            s_n = lax.dot_general(q_bd, k_n, nt, preferred_element_type=F32)[:, 0:1]
            sc_c = s_c * SCALE + cvec_ref[g, :, 0:N_BACK]
            sc_n = s_n * SCALE + cvec_ref[g, :, N_BACK:N_BACK + 1]
            mx = jnp.maximum(jnp.max(sc_c, axis=-1, keepdims=True), sc_n)
            p_c = jnp.exp(sc_c - mx)
            p_n = jnp.exp(sc_n - mx)
            l = jnp.sum(p_c, axis=-1, keepdims=True) + p_n
            v_n = jnp.broadcast_to(v_new, (N_SLOTS, D_ATTN)).astype(BF16).astype(F32)
            o = jnp.dot(p_c.astype(BF16), v_c, preferred_element_type=F32) + p_n.astype(BF16).astype(F32) * v_n
            outs.append(o / l)
            lses.append(mx + jnp.log(l))
        top = jnp.maximum(jnp.maximum(lses[0], lses[1]), lses[2])
        es = [jnp.exp(x - top) for x in lses]
        tot = es[0] + es[1] + es[2]
        comb = (es[0] / tot) * outs[0] + (es[1] / tot) * outs[1] + (es[2] / tot) * outs[2]
        o_ref[pl.ds(row, 1), :] = jnp.sum(jnp.where(diag, comb, 0.0), axis=0, keepdims=True)


def _attn_sample(qkvs, caches, cvec):
    db = qkvs[0].shape[2]
    tb = 4
    views, specs = [], []
    for g, c in enumerate(caches):
        dil = DILATIONS[g]
        lc = c.shape[1]
        views.append(c.reshape(db, lc // dil, dil, 2, N_SLOTS, HEAD_DIM))
        specs.append(pl.BlockSpec((tb, N_BACK, None, 2, N_SLOTS, HEAD_DIM), lambda s: (s, 0, 0, 0, 0, 0)))
    q_spec = pl.BlockSpec((3, 1, db, D_ATTN), lambda s: (0, 0, 0, 0))
    return pl.pallas_call(
        functools.partial(_attn_sample_kernel, tb=tb),
        grid=(db // tb,),
        in_specs=[pl.BlockSpec((N_GROUPS, N_SLOTS, 2 * BLOCK), lambda s: (0, 0, 0)), q_spec, q_spec, q_spec] + specs,
        out_specs=pl.BlockSpec((db, D_ATTN), lambda s: (0, 0)),
        out_shape=jax.ShapeDtypeStruct((db, D_ATTN), F32),
        compiler_params=_params(("arbitrary",)),
        name="attn_sample",
    )(cvec, *qkvs, *views)


def _out_kernel(*refs, tm, n_groups):
    o_refs = refs[:n_groups]
    lse_refs = refs[n_groups:2 * n_groups] if n_groups > 1 else ()
    rest = refs[len(o_refs) + len(lse_refs):]
    if n_groups > 1:
        sga_ref, sma_ref, smb_ref, yb_ref, x_ref, woa_ref, wob_ref, wo_ref, y_ref, a_ref, og_ref, lg_ref = rest
        for g in range(n_groups):
            dil = DILATIONS[g]
            sub = tm // dil
            for r in range(dil):
                rows = slice(None) if dil == 1 else pl.ds(r, sub, stride=dil)
                lg_ref[g, rows, :] = lse_refs[g][r]
                for h in range(N_SLOTS):
                    og_ref[g, h, rows, :] = o_refs[g][r, :, h * HEAD_DIM:(h + 1) * HEAD_DIM]
        lses = [lg_ref[g] for g in range(n_groups)]
        top = functools.reduce(jnp.maximum, lses)
        es = [jnp.exp(x - top) for x in lses]
        tot = functools.reduce(lambda a, b: a + b, es)
        ws = [e / tot for e in es]
        for h in range(N_SLOTS):
            sl = slice(h * HEAD_DIM, (h + 1) * HEAD_DIM)
            o = functools.reduce(lambda a, b: a + b,
                                 [ws[g][:, h:h + 1] * og_ref[g, h] for g in range(n_groups)])
            a_ref[:, sl] = (o * sga_ref[:, sl]).astype(BF16)
    else:
        sga_ref, sma_ref, smb_ref, yb_ref, x_ref, woa_ref, wob_ref, wo_ref, y_ref, a_ref = rest
        a_ref[...] = (o_refs[0][...] * sga_ref[...]).astype(BF16)
    ya = jnp.dot(a_ref[...], woa_ref[...], preferred_element_type=F32)
    yb = jnp.dot(yb_ref[...], wob_ref[...], preferred_element_type=F32)
    merged = sma_ref[...] * ya + smb_ref[...] * yb
    y_ref[...] = x_ref[...] + jnp.dot(merged.astype(BF16), wo_ref[...], preferred_element_type=F32)


def _out_proj(os_, lses, gates, yb_in, x, w_out_a, w_out_b, w_o):
    rows = x.shape[0]
    tm = min(rows, 256)
    n_groups = len(os_)

    def row_spec(width, col_block=0):
        return pl.BlockSpec((tm, width), lambda m: (m, col_block))

    def sub_spec(dil, width):
        return pl.BlockSpec((dil, tm // dil, width), lambda m: (0, m, 0))

    def const_spec(shape):
        return pl.BlockSpec(shape, lambda m: (0, 0), pipeline_mode=pl.Buffered(1))

    scratch = [pltpu.VMEM((tm, D_ATTN), BF16)]
    if n_groups > 1:
        in_specs = ([sub_spec(DILATIONS[g], D_ATTN) for g in range(n_groups)]
                    + [sub_spec(DILATIONS[g], LANES) for g in range(n_groups)])
        scratch += [pltpu.VMEM((n_groups, N_SLOTS, tm, HEAD_DIM), F32), pltpu.VMEM((n_groups, tm, LANES), F32)]
    else:
        in_specs = [row_spec(D_ATTN)]
    in_specs += [row_spec(D_ATTN, 2 * D_MODEL // D_ATTN),
                 row_spec(D_MODEL, 0), row_spec(D_MODEL, 1),
                 row_spec(D_CONV), row_spec(D_MODEL),
                 const_spec((D_ATTN, D_MODEL)), const_spec((D_CONV, D_MODEL)), const_spec((D_MODEL, D_MODEL))]
    return pl.pallas_call(
        functools.partial(_out_kernel, tm=tm, n_groups=n_groups),
        grid=(rows // tm,),
        in_specs=in_specs,
        out_specs=row_spec(D_MODEL),
        out_shape=jax.ShapeDtypeStruct((rows, D_MODEL), F32),
        scratch_shapes=scratch,
        compiler_params=_params(("arbitrary",)),
        name="out_proj",
    )(*os_, *lses, gates, gates, gates, yb_in, x, w_out_a, w_out_b, w_o)


def kernel(x_prompt, x_sample, cache_kv_w128, cache_kv_w512, cache_kv_w2048, state_conv, norm_gain, w_in,
           q_norm_gain, k_norm_gain, rel_bias, conv_w, w_out_a, w_out_b, w_o):
    seq = x_prompt.shape[1]
    db = x_sample.shape[0]
    xp = x_prompt.reshape(seq, D_MODEL)
    xs = x_sample.reshape(db, D_MODEL)
    caches = (cache_kv_w128, cache_kv_w512, cache_kv_w2048)
    qk_gains = jnp.stack([jnp.tile(q_norm_gain, N_SLOTS), jnp.tile(k_norm_gain, N_SLOTS)]).reshape(2, 1, D_ATTN)
    w_in, woa, wob, wo = _to_bf16(w_in), _to_bf16(w_out_a), _to_bf16(w_out_b), _to_bf16(w_o)
    cvec = _bias_by_offset(rel_bias)

    xn = _pre_norm(xp, norm_gain)
    qkvs, kv_p = zip(*[_qkv_proj(xn, w_in, qk_gains, g, DILATIONS[g], BF16, min(WINDOWS[g], seq))
                       for g in range(N_GROUPS)])
    yb_in, h_last = _branch_b(xn, w_in, conv_w)
    gates = _gates(xn, w_in)
    os_, lses = zip(*[_attn_prompt(qkvs[g], cvec, g) for g in range(N_GROUPS)])
    y_prompt = _out_proj(os_, lses, gates, yb_in, xp, woa, wob, wo).reshape(1, seq, D_MODEL)
    conv_p = h_last[SUBLANES - (CONV_W - 1):][None]

    xn_s = _pre_norm(xs, norm_gain)
    qkvs_s, kv_s = zip(*[_qkv_proj(xn_s, w_in, qk_gains, g, 1, F32, db) for g in range(N_GROUPS)])
    yb_in_s, h_s = _branch_b(xn_s, w_in, conv_w, state=state_conv)
    gates_s = _gates(xn_s, w_in)
    o_s = _attn_sample(qkvs_s, caches, cvec)
    y_sample = _out_proj((o_s,), (), gates_s, yb_in_s, xs, woa, wob, wo).reshape(db, 1, D_MODEL)
    conv_s = jnp.stack([state_conv[:, CONV_W - 2], h_s], axis=1)

    return (y_prompt, y_sample, kv_p[0][None], kv_p[1][None], kv_p[2][None], conv_p,
            kv_s[0][:, None], kv_s[1][:, None], kv_s[2][:, None], conv_s)
```

```python
import functools
import math

import numpy as np
import jax
import jax.numpy as jnp
from jax import lax
from jax.experimental import pallas as pl
from jax.experimental.pallas import tpu as pltpu

D_MODEL = 2048
N_GROUPS = 3
DILATIONS = (1, 4, 16)
N_BACK = 128
WINDOWS = (128, 512, 2048)
N_SLOTS = 8
HEAD_DIM = 128
D_ATTN = N_SLOTS * HEAD_DIM
QKV_COLS = N_GROUPS * D_ATTN
D_CONV = D_MODEL // 2
CONV_W = 3
BLOCK = N_BACK
NUM_BUCKETS = 32
MAX_EXACT = NUM_BUCKETS // 2
MAX_DISTANCE = 2048
EPS = 1e-6
SCALE = HEAD_DIM ** -0.5

COL_K = QKV_COLS
COL_V = 2 * QKV_COLS
COL_GATE_A = 3 * QKV_COLS
COL_B = COL_GATE_A + D_ATTN
COL_C = COL_B + D_CONV
COL_XIN = COL_C + D_CONV
COL_GATE_B = COL_XIN + D_CONV
COL_MERGE_A = COL_GATE_B + D_CONV
COL_MERGE_B = COL_MERGE_A + D_MODEL

SUBLANES = 8
LANES = 128
MXU_COLS = 256
VMEM_LIMIT_BYTES = 56 * 1024 * 1024

BF16 = jnp.bfloat16
F32 = jnp.float32
NEG_INF = float("-inf")


def _params(semantics):
    return pltpu.CompilerParams(dimension_semantics=semantics, vmem_limit_bytes=VMEM_LIMIT_BYTES)


def _cast_kernel(w_ref, o_ref):
    o_ref[...] = w_ref[...].astype(o_ref.dtype)


def _to_bf16(w):
    rows, cols = w.shape
    tn = 512
    return pl.pallas_call(
        _cast_kernel,
        grid=(cols // tn,),
        in_specs=[pl.BlockSpec((rows, tn), lambda n: (0, n))],
        out_specs=pl.BlockSpec((rows, tn), lambda n: (0, n)),
        out_shape=jax.ShapeDtypeStruct((rows, cols), BF16),
        compiler_params=_params(("arbitrary",)),
        name="to_bf16",
    )(w)


def _norm_kernel(x_ref, g_ref, o_ref):
    x = x_ref[...]
    r = lax.rsqrt(jnp.mean(x * x, axis=-1, keepdims=True) + EPS)
    o_ref[...] = ((x * r) * g_ref[...]).astype(o_ref.dtype)


def _pre_norm(x, gain):
    rows = x.shape[0]
    tm = min(rows, 512)
    return pl.pallas_call(
        _norm_kernel,
        grid=(rows // tm,),
        in_specs=[pl.BlockSpec((tm, D_MODEL), lambda m: (m, 0)),
                  pl.BlockSpec((1, D_MODEL), lambda m: (0, 0))],
        out_specs=pl.BlockSpec((tm, D_MODEL), lambda m: (m, 0)),
        out_shape=jax.ShapeDtypeStruct((rows, D_MODEL), BF16),
        compiler_params=_params(("arbitrary",)),
        name="pre_norm",
    )(x, gain.reshape(1, D_MODEL))


def _residue_stride(dil, sub):
    return sub + SUBLANES if dil > 4 else sub


def _qkv_kernel(xn_ref, w_ref, gain_ref, o_ref, tail_ref, res_ref, *, dil, tm, tr, m_tail0):
    kind = pl.program_id(0)
    m = pl.program_id(1)
    normed = kind < 2
    xn = xn_ref[...]
    sub = tm // dil
    pitch = _residue_stride(dil, sub)
    scatter = dil > 4
    heads_per_chunk = MXU_COLS // HEAD_DIM
    for j in range(D_ATTN // MXU_COLS):
        p = jnp.dot(xn, w_ref[:, j * MXU_COLS:(j + 1) * MXU_COLS], preferred_element_type=F32)
        for hh in range(heads_per_chunk):
            h = j * heads_per_chunk + hh
            sl = slice(h * HEAD_DIM, (h + 1) * HEAD_DIM)
            ph = p[:, hh * HEAD_DIM:(hh + 1) * HEAD_DIM]
            r = lax.rsqrt(jnp.mean(ph * ph, axis=-1, keepdims=True) + EPS)
            res = (ph * jnp.where(normed, r, 1.0)) * jnp.where(normed, gain_ref[:, sl], 1.0)
            if scatter:
                for i in range(sub):
                    res_ref[h, pl.ds(i, dil, stride=pitch), :] = res[i * dil:(i + 1) * dil]
            else:
                res_ref[h] = res
            for rr in range(dil):
                if dil == 1:
                    rows = res
                elif scatter:
                    rows = res_ref[h, rr * pitch:rr * pitch + sub, :]
                else:
                    rows = res_ref[h, pl.ds(rr, sub, stride=dil), :]
                o_ref[rr, :, sl] = rows.astype(o_ref.dtype)

    @pl.when(jnp.logical_and(kind >= 1, m >= m_tail0))
    def _():
        tail_flat = tail_ref.reshape(tr * N_SLOTS, HEAD_DIM)
        for h in range(N_SLOTS):
            if scatter:
                i0 = (tm - tr) // dil
                for rr in range(dil):
                    tail_flat[pl.ds(rr * N_SLOTS + h, sub - i0, stride=dil * N_SLOTS), :] = (
                        res_ref[h, rr * pitch + i0:rr * pitch + sub, :])
            else:
                tail_flat[pl.ds(h, tr, stride=N_SLOTS), :] = res_ref[h, tm - tr:tm, :]


def _qkv_proj(xn, w_in, qk_gains, g, dil, out_dtype, tail_rows):
    rows = xn.shape[0]
    tm = min(rows, 1024)
    tn = D_ATTN
    tr = min(tail_rows, tm)
    m_tail0 = rows // tm - tail_rows // tr

    def tail_map(kind, m):
        return (jnp.where(kind >= 1, jnp.maximum(m - m_tail0, 0), 0), jnp.maximum(kind - 1, 0), 0, 0)

    return pl.pallas_call(
        functools.partial(_qkv_kernel, dil=dil, tm=tm, tr=tr, m_tail0=m_tail0),
        grid=(3, rows // tm),
        in_specs=[pl.BlockSpec((tm, D_MODEL), lambda kind, m: (m, 0)),
                  pl.BlockSpec((D_MODEL, tn), lambda kind, m: (0, kind * N_GROUPS + g)),
                  pl.BlockSpec((None, 1, tn), lambda kind, m: (jnp.minimum(kind, 1), 0, 0))],
        out_specs=[pl.BlockSpec((None, dil, tm // dil, tn), lambda kind, m: (kind, 0, m, 0)),
                   pl.BlockSpec((tr, None, N_SLOTS, HEAD_DIM), tail_map)],
        out_shape=[jax.ShapeDtypeStruct((3, dil, rows // dil, tn), out_dtype),
                   jax.ShapeDtypeStruct((tail_rows, 2, N_SLOTS, HEAD_DIM), F32)],
        scratch_shapes=[pltpu.VMEM((N_SLOTS, dil * _residue_stride(dil, tm // dil), HEAD_DIM), F32)],
        compiler_params=_params(("arbitrary", "arbitrary")),
        name=f"qkv_proj_g{g}",
    )(xn, w_in, qk_gains)


def _branch_b_kernel(*refs, tm, per_row_state):
    if per_row_state:
        xn_ref, wb_ref, wc_ref, wx_ref, wg_ref, cw_ref, hm2_ref, hm1_ref, y_ref, h_ref = refs
    else:
        xn_ref, wb_ref, wc_ref, wx_ref, wg_ref, cw_ref, y_ref, h_ref, hbuf_ref = refs

        @pl.when(pl.program_id(1) == 0)
        def _():
            hbuf_ref[0:SUBLANES, :] = jnp.zeros((SUBLANES, hbuf_ref.shape[1]), F32)

    xn = xn_ref[...]
    pb = jnp.dot(xn, wb_ref[...], preferred_element_type=F32)
    pc = jnp.dot(xn, wc_ref[...], preferred_element_type=F32)
    px = jnp.dot(xn, wx_ref[...], preferred_element_type=F32)
    pg = jnp.dot(xn, wg_ref[...], preferred_element_type=F32)
    h = pc * px
    if per_row_state:
        hm2 = hm2_ref[...]
        hm1 = hm1_ref[...]
        h_ref[...] = h
    else:
        hbuf_ref[SUBLANES:SUBLANES + tm, :] = h
        hm2 = hbuf_ref[pl.ds(SUBLANES - 2, tm), :]
        hm1 = hbuf_ref[pl.ds(SUBLANES - 1, tm), :]
        last = hbuf_ref[pl.ds(tm, SUBLANES), :]
        h_ref[...] = last
        hbuf_ref[0:SUBLANES, :] = last
    z = cw_ref[0:1, :] * hm2 + cw_ref[1:2, :] * hm1 + cw_ref[2:3, :] * h
    y_ref[...] = (pb * z * (pg * jax.nn.sigmoid(pg))).astype(y_ref.dtype)


def _branch_b(xn, w_in, conv_w, state=None):
    rows = xn.shape[0]
    tm = min(rows, 1024)
    tnb = 256
    per_row_state = state is not None

    def wspec(col0):
        return pl.BlockSpec((D_MODEL, tnb), lambda j, m: (0, col0 // tnb + j))

    in_specs = [pl.BlockSpec((tm, D_MODEL), lambda j, m: (m, 0)),
                wspec(COL_B), wspec(COL_C), wspec(COL_XIN), wspec(COL_GATE_B),
                pl.BlockSpec((CONV_W, tnb), lambda j, m: (0, j))]
    args = [xn, w_in, w_in, w_in, w_in, conv_w]
    scratch = []
    if per_row_state:
        st = state.reshape(rows, (CONV_W - 1) * D_CONV)
        in_specs += [pl.BlockSpec((tm, tnb), lambda j, m: (m, j)),
                     pl.BlockSpec((tm, tnb), lambda j, m: (m, D_CONV // tnb + j))]
        args += [st, st]
        h_spec = pl.BlockSpec((tm, tnb), lambda j, m: (m, j))
        h_shape = jax.ShapeDtypeStruct((rows, D_CONV), F32)
    else:
        scratch.append(pltpu.VMEM((tm + SUBLANES, tnb), F32))
        h_spec = pl.BlockSpec((SUBLANES, tnb), lambda j, m: (0, j))
        h_shape = jax.ShapeDtypeStruct((SUBLANES, D_CONV), F32)
    return pl.pallas_call(
        functools.partial(_branch_b_kernel, tm=tm, per_row_state=per_row_state),
        grid=(D_CONV // tnb, rows // tm),
        in_specs=in_specs,
        out_specs=[pl.BlockSpec((tm, tnb), lambda j, m: (m, j)), h_spec],
        out_shape=[jax.ShapeDtypeStruct((rows, D_CONV), BF16), h_shape],
        scratch_shapes=scratch,
        compiler_params=_params(("arbitrary", "arbitrary")),
        name="branch_b",
    )(*args)


def _gates_kernel(xn_ref, w_ref, o_ref, *, n_silu):
    is_silu = pl.program_id(0) == n_silu
    xn = xn_ref[...]
    for j in range(w_ref.shape[1] // MXU_COLS):
        cs = slice(j * MXU_COLS, (j + 1) * MXU_COLS)
        p = jnp.dot(xn, w_ref[:, cs], preferred_element_type=F32)
        o_ref[:, cs] = jax.nn.sigmoid(p) * jnp.where(is_silu, p, 1.0)


def _gates(xn, w_in):
    rows = xn.shape[0]
    tm = min(rows, 1024)
    tn = 1024
    n_tiles = (D_ATTN + 2 * D_MODEL) // tn
    n_silu = n_tiles - 1

    def wmap(n, m):
        return (0, jnp.where(n == n_silu, COL_GATE_A // tn, COL_MERGE_A // tn + n))

    return pl.pallas_call(
        functools.partial(_gates_kernel, n_silu=n_silu),
        grid=(n_tiles, rows // tm),
        in_specs=[pl.BlockSpec((tm, D_MODEL), lambda n, m: (m, 0)),
                  pl.BlockSpec((D_MODEL, tn), wmap)],
        out_specs=pl.BlockSpec((tm, tn), lambda n, m: (m, n)),
        out_shape=jax.ShapeDtypeStruct((rows, n_tiles * tn), F32),
        compiler_params=_params(("arbitrary", "arbitrary")),
        name="gates",
    )(xn, w_in)


def _t5_bucket(dist):
    d = np.maximum(dist, 1).astype(np.float32)
    large = MAX_EXACT + (np.log(d / np.float32(MAX_EXACT)) / np.float32(math.log(MAX_DISTANCE / MAX_EXACT))
                         * np.float32(NUM_BUCKETS - MAX_EXACT)).astype(np.int32)
    large = np.minimum(large, NUM_BUCKETS - 1)
    return np.where(dist < MAX_EXACT, dist, large)


def _bias_by_offset(rel_bias):
    steps = N_BACK - np.arange(N_BACK + 1)
    idx = np.stack([_t5_bucket(d * steps) for d in DILATIONS])
    g_idx = np.arange(N_GROUPS)[:, None]
    vals = jnp.transpose(rel_bias[idx, g_idx], (0, 2, 1)).astype(F32)
    return jnp.pad(vals, ((0, 0), (0, 0), (0, 2 * BLOCK - N_BACK - 1)), constant_values=NEG_INF)


def _sample_bias(cvec):
    kv_head = np.arange(N_BACK * 2 * N_SLOTS) % (2 * N_SLOTS)
    own_k_row = kv_head[None, :] == np.arange(N_SLOTS)[:, None]
    per_row = jnp.repeat(cvec[:, :, :N_BACK], 2 * N_SLOTS, axis=2)
    return jnp.where(own_k_row[None], per_row, NEG_INF), cvec[:, :, N_BACK:N_BACK + 1]


def _attn_prompt_kernel(cvec_ref, q_ref, kp_ref, kc_ref, vp_ref, vc_ref, o_ref, lse_ref, bias_ref, s_ref, p_ref):
    first = jnp.logical_and(pl.program_id(0) == 0, pl.program_id(1) == 0)

    @pl.when(first)
    def _():
        for h in range(N_SLOTS):
            row = jnp.broadcast_to(cvec_ref[h:h + 1, :], (BLOCK, 2 * BLOCK))
            full = pltpu.roll(row, 0, 1, stride=1, stride_axis=0)
            bias_ref[1, h] = full
            bias_ref[0, h, :, :BLOCK] = jnp.full((BLOCK, BLOCK), NEG_INF, F32)
            bias_ref[0, h, :, BLOCK:] = full[:, BLOCK:]

    nt = (((1,), (1,)), ((), ()))
    heads = [slice(h * HEAD_DIM, (h + 1) * HEAD_DIM) for h in range(N_SLOTS)]
    for h, sl in enumerate(heads):
        q = q_ref[:, sl]
        s_ref[h, :, :BLOCK] = lax.dot_general(q, kp_ref[:, sl], nt, preferred_element_type=F32)
        s_ref[h, :, BLOCK:] = lax.dot_general(q, kc_ref[:, sl], nt, preferred_element_type=F32)
    sc = s_ref[...] * SCALE + bias_ref[jnp.minimum(pl.program_id(1), 1)]
    mx = jnp.max(jnp.maximum(sc[:, :, :BLOCK], sc[:, :, BLOCK:]), axis=-1, keepdims=True)
    p = jnp.exp(sc - mx)
    l = jnp.sum(p[:, :, :BLOCK] + p[:, :, BLOCK:], axis=-1, keepdims=True)
    p_ref[...] = p.astype(BF16)
    lse = mx + jnp.log(l)
    lane = lax.broadcasted_iota(jnp.int32, (BLOCK, LANES), 1)
    lse_all = jnp.zeros((BLOCK, LANES), F32)
    for h, sl in enumerate(heads):
        o = (jnp.dot(p_ref[h, :, :BLOCK], vp_ref[:, sl], preferred_element_type=F32)
             + jnp.dot(p_ref[h, :, BLOCK:], vc_ref[:, sl], preferred_element_type=F32))
        o_ref[:, sl] = o / l[h]
        lse_all = jnp.where(lane == h, lse[h], lse_all)
    lse_ref[...] = lse_all


def _attn_prompt(qkv, cvec, g):
    _, dil, ls, _ = qkv.shape
    nb = ls // BLOCK

    def spec(kind, prev):
        if prev:
            return pl.BlockSpec((None, None, BLOCK, D_ATTN), lambda r, i: (kind, r, jnp.maximum(i - 1, 0), 0))
        return pl.BlockSpec((None, None, BLOCK, D_ATTN), lambda r, i: (kind, r, i, 0))

    return pl.pallas_call(
        _attn_prompt_kernel,
        grid=(dil, nb),
        in_specs=[pl.BlockSpec((None, N_SLOTS, 2 * BLOCK), lambda r, i: (g, 0, 0)),
                  spec(0, False), spec(1, True), spec(1, False), spec(2, True), spec(2, False)],
        out_specs=[pl.BlockSpec((None, BLOCK, D_ATTN), lambda r, i: (r, i, 0)),
                   pl.BlockSpec((None, BLOCK, LANES), lambda r, i: (r, i, 0))],
        out_shape=[jax.ShapeDtypeStruct((dil, ls, D_ATTN), F32),
                   jax.ShapeDtypeStruct((dil, ls, LANES), F32)],
        scratch_shapes=[pltpu.VMEM((2, N_SLOTS, BLOCK, 2 * BLOCK), F32),
                        pltpu.VMEM((N_SLOTS, BLOCK, 2 * BLOCK), F32),
                        pltpu.VMEM((N_SLOTS, BLOCK, 2 * BLOCK), BF16)],
        compiler_params=_params(("arbitrary", "arbitrary")),
        name=f"attn_prompt_g{g}",
    )(cvec, qkv, qkv, qkv, qkv, qkv)


def _attn_sample_kernel(bias_ref, nbias_ref, q0_ref, q1_ref, q2_ref, kv0_ref, kv1_ref, kv2_ref,
                        c0_ref, c1_ref, c2_ref, o_ref, slab_ref, s_ref, p_ref, og_ref, *, tb):
    rows = pl.ds(pl.program_id(0) * tb, tb)
    nt = (((1,), (1,)), ((), ()))
    kv_rows = 2 * N_SLOTS
    slab_rows = N_BACK * kv_rows
    q_refs = (q0_ref, q1_ref, q2_ref)
    kv_refs = (kv0_ref, kv1_ref, kv2_ref)
    flat_refs = [c.reshape(tb * slab_rows, HEAD_DIM) for c in (c0_ref, c1_ref, c2_ref)]

    def rounded(x):
        return x.astype(BF16).astype(F32)

    qs = [q_ref[rows] for q_ref in q_refs]
    for g in range(N_GROUPS):
        for t in range(tb):
            slab_ref[g, t] = flat_refs[g][pl.ds(t * slab_rows, slab_rows), :].astype(BF16)
            s_ref[g, t] = lax.dot_general(qs[g][t].astype(BF16), slab_ref[g, t], nt, preferred_element_type=F32)
    p_news, ls, lses = [], [], []
    for g in range(N_GROUPS):
        s_n = jnp.sum(rounded(qs[g]) * rounded(kv_refs[g][rows, 0]), axis=-1, keepdims=True)
        sc_c = s_ref[g] * SCALE + bias_ref[g][None]
        sc_n = s_n * SCALE + nbias_ref[g][None]
        mx = jnp.maximum(jnp.max(sc_c, axis=-1, keepdims=True), sc_n)
        p_c = jnp.exp(sc_c - mx)
        p_n = jnp.exp(sc_n - mx)
        l = jnp.sum(p_c, axis=-1, keepdims=True) + p_n
        p_ref[g] = pltpu.roll(p_c, N_SLOTS, 2).astype(BF16)
        p_news.append(p_n)
        ls.append(l)
        lses.append(mx + jnp.log(l))
    for g in range(N_GROUPS):
        for t in range(tb):
            og_ref[g, t] = jnp.dot(p_ref[g, t], slab_ref[g, t], preferred_element_type=F32)
    outs = [(og_ref[g] + rounded(p_news[g]) * rounded(kv_refs[g][rows, 1])) / ls[g] for g in range(N_GROUPS)]
    top = jnp.maximum(jnp.maximum(lses[0], lses[1]), lses[2])
    es = [jnp.exp(x - top) for x in lses]
    tot = es[0] + es[1] + es[2]
    o_ref[rows] = (es[0] / tot) * outs[0] + (es[1] / tot) * outs[1] + (es[2] / tot) * outs[2]


def _attn_sample(qs, kvs, caches, bias, nbias):
    db = qs[0].shape[0]
    tb = 4
    kv_rows = 2 * N_SLOTS
    views, specs = [], []
    for g, c in enumerate(caches):
        dil = DILATIONS[g]
        lc = c.shape[1]
        views.append(c.reshape(db, lc // dil, dil * kv_rows, HEAD_DIM))
        specs.append(pl.BlockSpec((tb, N_BACK, kv_rows, HEAD_DIM), lambda s: (s, 0, 0, 0)))
    q_spec = pl.BlockSpec((db, N_SLOTS, HEAD_DIM), lambda s: (0, 0, 0))
    kv_spec = pl.BlockSpec((db, 2, N_SLOTS, HEAD_DIM), lambda s: (0, 0, 0, 0))
    return pl.pallas_call(
        functools.partial(_attn_sample_kernel, tb=tb),
        grid=(db // tb,),
        in_specs=[pl.BlockSpec((N_GROUPS, N_SLOTS, N_BACK * kv_rows), lambda s: (0, 0, 0)),
                  pl.BlockSpec((N_GROUPS, N_SLOTS, 1), lambda s: (0, 0, 0)),
                  q_spec, q_spec, q_spec, kv_spec, kv_spec, kv_spec] + specs,
        out_specs=pl.BlockSpec((db, N_SLOTS, HEAD_DIM), lambda s: (0, 0, 0)),
        out_shape=jax.ShapeDtypeStruct((db, N_SLOTS, HEAD_DIM), F32),
        scratch_shapes=[pltpu.VMEM((N_GROUPS, tb, N_BACK * kv_rows, HEAD_DIM), BF16),
                        pltpu.VMEM((N_GROUPS, tb, N_SLOTS, N_BACK * kv_rows), F32),
                        pltpu.VMEM((N_GROUPS, tb, N_SLOTS, N_BACK * kv_rows), BF16),
                        pltpu.VMEM((N_GROUPS, tb, N_SLOTS, HEAD_DIM), F32)],
        compiler_params=_params(("arbitrary",)),
        name="attn_sample",
    )(bias, nbias, *qs, *kvs, *views)


def _out_kernel(*refs, tm, n_groups):
    o_refs = refs[:n_groups]
    lse_refs = refs[n_groups:2 * n_groups] if n_groups > 1 else ()
    rest = refs[len(o_refs) + len(lse_refs):]
    if n_groups > 1:
        sga_ref, sma_ref, smb_ref, yb_ref, x_ref, woa_ref, wob_ref, wo_ref, y_ref, a_ref, og_ref, lg_ref = rest
        for g in range(n_groups):
            dil = DILATIONS[g]
            sub = tm // dil
            for r in range(dil):
                rows = slice(None) if dil == 1 else pl.ds(r, sub, stride=dil)
                lg_ref[g, rows, :] = lse_refs[g][r]
                for h in range(N_SLOTS):
                    og_ref[g, h, rows, :] = o_refs[g][r, :, h * HEAD_DIM:(h + 1) * HEAD_DIM]
        lses = [lg_ref[g] for g in range(n_groups)]
        top = functools.reduce(jnp.maximum, lses)
        es = [jnp.exp(x - top) for x in lses]
        tot = functools.reduce(lambda a, b: a + b, es)
        ws = [e / tot for e in es]
        for h in range(N_SLOTS):
            sl = slice(h * HEAD_DIM, (h + 1) * HEAD_DIM)
            o = functools.reduce(lambda a, b: a + b,
                                 [ws[g][:, h:h + 1] * og_ref[g, h] for g in range(n_groups)])
            a_ref[:, sl] = (o * sga_ref[:, sl]).astype(BF16)
    else:
        sga_ref, sma_ref, smb_ref, yb_ref, x_ref, woa_ref, wob_ref, wo_ref, y_ref, a_ref = rest
        a_ref[...] = (o_refs[0][...] * sga_ref[...]).astype(BF16)
    ya = jnp.dot(a_ref[...], woa_ref[...], preferred_element_type=F32)
    yb = jnp.dot(yb_ref[...], wob_ref[...], preferred_element_type=F32)
    merged = sma_ref[...] * ya + smb_ref[...] * yb
    y_ref[...] = x_ref[...] + jnp.dot(merged.astype(BF16), wo_ref[...], preferred_element_type=F32)


def _out_proj(os_, lses, gates, yb_in, x, w_out_a, w_out_b, w_o):
    rows = x.shape[0]
    tm = min(rows, 256)
    n_groups = len(os_)

    def row_spec(width, col_block=0):
        return pl.BlockSpec((tm, width), lambda m: (m, col_block))

    def sub_spec(dil, width):
        return pl.BlockSpec((dil, tm // dil, width), lambda m: (0, m, 0))

    def const_spec(shape):
        return pl.BlockSpec(shape, lambda m: (0, 0), pipeline_mode=pl.Buffered(1))

    scratch = [pltpu.VMEM((tm, D_ATTN), BF16)]
    if n_groups > 1:
        in_specs = ([sub_spec(DILATIONS[g], D_ATTN) for g in range(n_groups)]
                    + [sub_spec(DILATIONS[g], LANES) for g in range(n_groups)])
        scratch += [pltpu.VMEM((n_groups, N_SLOTS, tm, HEAD_DIM), F32), pltpu.VMEM((n_groups, tm, LANES), F32)]
    else:
        in_specs = [row_spec(D_ATTN)]
    in_specs += [row_spec(D_ATTN, 2 * D_MODEL // D_ATTN),
                 row_spec(D_MODEL, 0), row_spec(D_MODEL, 1),
                 row_spec(D_CONV), row_spec(D_MODEL),
                 const_spec((D_ATTN, D_MODEL)), const_spec((D_CONV, D_MODEL)), const_spec((D_MODEL, D_MODEL))]
    return pl.pallas_call(
        functools.partial(_out_kernel, tm=tm, n_groups=n_groups),
        grid=(rows // tm,),
        in_specs=in_specs,
        out_specs=row_spec(D_MODEL),
        out_shape=jax.ShapeDtypeStruct((rows, D_MODEL), F32),
        scratch_shapes=scratch,
        compiler_params=_params(("arbitrary",)),
        name="out_proj",
    )(*os_, *lses, gates, gates, gates, yb_in, x, w_out_a, w_out_b, w_o)


def kernel(x_prompt, x_sample, cache_kv_w128, cache_kv_w512, cache_kv_w2048, state_conv, norm_gain, w_in,
           q_norm_gain, k_norm_gain, rel_bias, conv_w, w_out_a, w_out_b, w_o):
    seq = x_prompt.shape[1]
    db = x_sample.shape[0]
    xp = x_prompt.reshape(seq, D_MODEL)
    xs = x_sample.reshape(db, D_MODEL)
    caches = (cache_kv_w128, cache_kv_w512, cache_kv_w2048)
    qk_gains = jnp.stack([jnp.tile(q_norm_gain, N_SLOTS), jnp.tile(k_norm_gain, N_SLOTS)]).reshape(2, 1, D_ATTN)
    w_in, woa, wob, wo = _to_bf16(w_in), _to_bf16(w_out_a), _to_bf16(w_out_b), _to_bf16(w_o)
    cvec = _bias_by_offset(rel_bias)

    xn = _pre_norm(xp, norm_gain)
    qkvs, kv_p = zip(*[_qkv_proj(xn, w_in, qk_gains, g, DILATIONS[g], BF16, min(WINDOWS[g], seq))
                       for g in range(N_GROUPS)])
    yb_in, h_last = _branch_b(xn, w_in, conv_w)
    gates = _gates(xn, w_in)
    os_, lses = zip(*[_attn_prompt(qkvs[g], cvec, g) for g in range(N_GROUPS)])
    y_prompt = _out_proj(os_, lses, gates, yb_in, xp, woa, wob, wo).reshape(1, seq, D_MODEL)
    conv_p = h_last[SUBLANES - (CONV_W - 1):][None]

    xn_s = _pre_norm(xs, norm_gain)
    qkvs_s, kv_s = zip(*[_qkv_proj(xn_s, w_in, qk_gains, g, 1, F32, db) for g in range(N_GROUPS)])
    yb_in_s, h_s = _branch_b(xn_s, w_in, conv_w, state=state_conv)
    gates_s = _gates(xn_s, w_in)
    qs_s = [qkv[0, 0].reshape(db, N_SLOTS, HEAD_DIM) for qkv in qkvs_s]
    o_s = _attn_sample(qs_s, kv_s, caches, *_sample_bias(cvec)).reshape(db, D_ATTN)
    y_sample = _out_proj((o_s,), (), gates_s, yb_in_s, xs, woa, wob, wo).reshape(db, 1, D_MODEL)
    conv_s = jnp.stack([state_conv[:, CONV_W - 2], h_s], axis=1)

    return (y_prompt, y_sample, kv_p[0][None], kv_p[1][None], kv_p[2][None], conv_p,
            kv_s[0][:, None], kv_s[1][:, None], kv_s[2][:, None], conv_s)
```

```python
import functools
import math
from typing import Callable, NamedTuple

import numpy as np
import jax
import jax.numpy as jnp
from jax import lax
from jax.experimental import pallas as pl
from jax.experimental.pallas import tpu as pltpu

D_MODEL = 2048
N_GROUPS = 3
DILATIONS = (1, 4, 16)
N_BACK = 128
WINDOWS = (128, 512, 2048)
N_SLOTS = 8
HEAD_DIM = 128
D_ATTN = N_SLOTS * HEAD_DIM
QKV_COLS = N_GROUPS * D_ATTN
D_CONV = D_MODEL // 2
CONV_W = 3
BLOCK = N_BACK
NUM_BUCKETS = 32
MAX_EXACT = NUM_BUCKETS // 2
MAX_DISTANCE = 2048
EPS = 1e-6
SCALE = HEAD_DIM ** -0.5

COL_K = QKV_COLS
COL_V = 2 * QKV_COLS
COL_GATE_A = 3 * QKV_COLS
COL_B = COL_GATE_A + D_ATTN
COL_C = COL_B + D_CONV
COL_XIN = COL_C + D_CONV
COL_GATE_B = COL_XIN + D_CONV
COL_MERGE_A = COL_GATE_B + D_CONV
COL_MERGE_B = COL_MERGE_A + D_MODEL

SUBLANES = 8
LANES = 128
MXU_COLS = 256
VMEM_LIMIT_BYTES = 56 * 1024 * 1024

BF16 = jnp.bfloat16
F32 = jnp.float32
NEG_INF = float("-inf")


def _params(semantics):
    return pltpu.CompilerParams(dimension_semantics=semantics, vmem_limit_bytes=VMEM_LIMIT_BYTES)


def _cast_kernel(w_ref, o_ref):
    o_ref[...] = w_ref[...].astype(o_ref.dtype)


def _to_bf16(w):
    rows, cols = w.shape
    tn = 512
    return pl.pallas_call(
        _cast_kernel,
        grid=(cols // tn,),
        in_specs=[pl.BlockSpec((rows, tn), lambda n: (0, n))],
        out_specs=pl.BlockSpec((rows, tn), lambda n: (0, n)),
        out_shape=jax.ShapeDtypeStruct((rows, cols), BF16),
        compiler_params=_params(("arbitrary",)),
        name="to_bf16",
    )(w)


def _norm_kernel(x_ref, g_ref, o_ref):
    x = x_ref[...]
    r = lax.rsqrt(jnp.mean(x * x, axis=-1, keepdims=True) + EPS)
    o_ref[...] = ((x * r) * g_ref[...]).astype(o_ref.dtype)


def _pre_norm(x, gain):
    rows = x.shape[0]
    tm = min(rows, 512)
    return pl.pallas_call(
        _norm_kernel,
        grid=(rows // tm,),
        in_specs=[pl.BlockSpec((tm, D_MODEL), lambda m: (m, 0)),
                  pl.BlockSpec((1, D_MODEL), lambda m: (0, 0))],
        out_specs=pl.BlockSpec((tm, D_MODEL), lambda m: (m, 0)),
        out_shape=jax.ShapeDtypeStruct((rows, D_MODEL), BF16),
        compiler_params=_params(("arbitrary",)),
        name="pre_norm",
    )(x, gain.reshape(1, D_MODEL))


def _residue_stride(dil, sub):
    return sub + SUBLANES if dil > 4 else sub


def _qkv_kernel(xn_ref, w_ref, gain_ref, o_ref, tail_ref, res_ref, *, dil, tm, tr, m_tail0):
    kind = pl.program_id(0)
    m = pl.program_id(1)
    normed = kind < 2
    xn = xn_ref[...]
    sub = tm // dil
    pitch = _residue_stride(dil, sub)
    scatter = dil > 4
    heads_per_chunk = MXU_COLS // HEAD_DIM
    for j in range(D_ATTN // MXU_COLS):
        p = jnp.dot(xn, w_ref[:, j * MXU_COLS:(j + 1) * MXU_COLS], preferred_element_type=F32)
        for hh in range(heads_per_chunk):
            h = j * heads_per_chunk + hh
            sl = slice(h * HEAD_DIM, (h + 1) * HEAD_DIM)
            ph = p[:, hh * HEAD_DIM:(hh + 1) * HEAD_DIM]
            r = lax.rsqrt(jnp.mean(ph * ph, axis=-1, keepdims=True) + EPS)
            res = (ph * jnp.where(normed, r, 1.0)) * jnp.where(normed, gain_ref[:, sl], 1.0)
            if scatter:
                for i in range(sub):
                    res_ref[h, pl.ds(i, dil, stride=pitch), :] = res[i * dil:(i + 1) * dil]
            else:
                res_ref[h] = res
            for rr in range(dil):
                if dil == 1:
                    rows = res
                elif scatter:
                    rows = res_ref[h, rr * pitch:rr * pitch + sub, :]
                else:
                    rows = res_ref[h, pl.ds(rr, sub, stride=dil), :]
                o_ref[rr, :, sl] = rows.astype(o_ref.dtype)

    @pl.when(jnp.logical_and(kind >= 1, m >= m_tail0))
    def _():
        for h in range(N_SLOTS):
            if scatter:
                i0 = (tm - tr) // dil
                for rr in range(dil):
                    tail_ref[pl.ds(rr, sub - i0, stride=dil), h, :] = res_ref[h, rr * pitch + i0:rr * pitch + sub, :]
            else:
                tail_ref[:, h, :] = res_ref[h, tm - tr:tm, :]


def _qkv_proj(xn, w_in, qk_gains, g, dil, out_dtype, tail_rows):
    rows = xn.shape[0]
    tm = min(rows, 1024)
    tn = D_ATTN
    tr = min(tail_rows, tm)
    m_tail0 = rows // tm - tail_rows // tr

    def tail_map(kind, m):
        return (jnp.where(kind >= 1, jnp.maximum(m - m_tail0, 0), 0), jnp.maximum(kind - 1, 0), 0, 0)

    return pl.pallas_call(
        functools.partial(_qkv_kernel, dil=dil, tm=tm, tr=tr, m_tail0=m_tail0),
        grid=(3, rows // tm),
        in_specs=[pl.BlockSpec((tm, D_MODEL), lambda kind, m: (m, 0)),
                  pl.BlockSpec((D_MODEL, tn), lambda kind, m: (0, kind * N_GROUPS + g)),
                  pl.BlockSpec((None, 1, tn), lambda kind, m: (jnp.minimum(kind, 1), 0, 0))],
        out_specs=[pl.BlockSpec((None, dil, tm // dil, tn), lambda kind, m: (kind, 0, m, 0)),
                   pl.BlockSpec((tr, None, N_SLOTS, HEAD_DIM), tail_map)],
        out_shape=[jax.ShapeDtypeStruct((3, dil, rows // dil, tn), out_dtype),
                   jax.ShapeDtypeStruct((tail_rows, 2, N_SLOTS, HEAD_DIM), F32)],
        scratch_shapes=[pltpu.VMEM((N_SLOTS, dil * _residue_stride(dil, tm // dil), HEAD_DIM), F32)],
        compiler_params=_params(("arbitrary", "arbitrary")),
        name=f"qkv_proj_g{g}",
    )(xn, w_in, qk_gains)


def _branch_b_kernel(*refs, tm, per_row_state, side):
    if per_row_state:
        xn_ref, wb_ref, wc_ref, wx_ref, wg_ref, cw_ref, hm2_ref, hm1_ref, y_ref, h_ref = refs
    else:
        (xn_ref, wb_ref, wc_ref, wx_ref, wg_ref, cw_ref), refs = refs[:6], refs[6:]
        if side is not None:
            side_in, (y_ref, h_ref), side_out, (hbuf_ref,), side_scratch = (
                refs[:6], refs[6:8], refs[8:10], refs[10:11], refs[11:])
        else:
            y_ref, h_ref, hbuf_ref = refs

        @pl.when(pl.program_id(1) == 0)
        def _():
            hbuf_ref[0:SUBLANES, :] = jnp.zeros((SUBLANES, hbuf_ref.shape[1]), F32)

        if side is not None:
            _attn_side(side, pl.program_id(0) * pl.num_programs(1) + pl.program_id(1),
                       *side_in, *side_out, *side_scratch)

    xn = xn_ref[...]
    pb = jnp.dot(xn, wb_ref[...], preferred_element_type=F32)
    pc = jnp.dot(xn, wc_ref[...], preferred_element_type=F32)
    px = jnp.dot(xn, wx_ref[...], preferred_element_type=F32)
    pg = jnp.dot(xn, wg_ref[...], preferred_element_type=F32)
    h = pc * px
    if per_row_state:
        hm2 = hm2_ref[...]
        hm1 = hm1_ref[...]
        h_ref[...] = h
    else:
        hbuf_ref[SUBLANES:SUBLANES + tm, :] = h
        hm2 = hbuf_ref[pl.ds(SUBLANES - 2, tm), :]
        hm1 = hbuf_ref[pl.ds(SUBLANES - 1, tm), :]
        last = hbuf_ref[pl.ds(tm, SUBLANES), :]
        h_ref[...] = last
        hbuf_ref[0:SUBLANES, :] = last
    z = cw_ref[0:1, :] * hm2 + cw_ref[1:2, :] * hm1 + cw_ref[2:3, :] * h
    y_ref[...] = (pb * z * (pg * jax.nn.sigmoid(pg))).astype(y_ref.dtype)


def _branch_b(xn, w_in, conv_w, state=None, attn=None):
    rows = xn.shape[0]
    tm = min(rows, 1024)
    tnb = 256
    per_row_state = state is not None
    grid = (D_CONV // tnb, rows // tm)
    side = None if attn is None else _attn_side_plan(*attn, n_steps=grid[0] * grid[1],
                                                     step_of=lambda j, m: j * grid[1] + m)

    def wspec(col0):
        return pl.BlockSpec((D_MODEL, tnb), lambda j, m: (0, col0 // tnb + j))

    in_specs = [pl.BlockSpec((tm, D_MODEL), lambda j, m: (m, 0)),
                wspec(COL_B), wspec(COL_C), wspec(COL_XIN), wspec(COL_GATE_B),
                pl.BlockSpec((CONV_W, tnb), lambda j, m: (0, j))]
    args = [xn, w_in, w_in, w_in, w_in, conv_w]
    scratch = []
    if per_row_state:
        st = state.reshape(rows, (CONV_W - 1) * D_CONV)
        in_specs += [pl.BlockSpec((tm, tnb), lambda j, m: (m, j)),
                     pl.BlockSpec((tm, tnb), lambda j, m: (m, D_CONV // tnb + j))]
        args += [st, st]
        h_spec = pl.BlockSpec((tm, tnb), lambda j, m: (m, j))
        h_shape = jax.ShapeDtypeStruct((rows, D_CONV), F32)
    else:
        scratch.append(pltpu.VMEM((tm + SUBLANES, tnb), F32))
        h_spec = pl.BlockSpec((SUBLANES, tnb), lambda j, m: (0, j))
        h_shape = jax.ShapeDtypeStruct((SUBLANES, D_CONV), F32)
    out_specs = [pl.BlockSpec((tm, tnb), lambda j, m: (m, j)), h_spec]
    out_shape = [jax.ShapeDtypeStruct((rows, D_CONV), BF16), h_shape]
    if side is not None:
        in_specs += side.in_specs
        args += side.args
        out_specs += side.out_specs
        out_shape += side.out_shape
        scratch += side.scratch
    outs = pl.pallas_call(
        functools.partial(_branch_b_kernel, tm=tm, per_row_state=per_row_state,
                          side=None if side is None else side.static),
        grid=grid,
        in_specs=in_specs,
        out_specs=out_specs,
        out_shape=out_shape,
        scratch_shapes=scratch,
        compiler_params=_params(("arbitrary", "arbitrary")),
        name="branch_b",
    )(*args)
    return outs if side is None else (outs[0], outs[1], side.finish(outs[2], outs[3]))


GATE_TILE = 1024
N_GATE_TILES = (2 * D_MODEL + D_ATTN) // GATE_TILE
SILU_TILE = N_GATE_TILES - 1


def _gates_kernel(xn_ref, w_ref, *refs, tile0, side):
    if side is not None:
        side_in, o_ref, side_out, side_scratch = refs[:6], refs[6], refs[7:9], refs[9:]
        _attn_side(side, pl.program_id(0) * pl.num_programs(1) + pl.program_id(1),
                   *side_in, *side_out, *side_scratch)
    else:
        o_ref, = refs
    is_silu = tile0 + pl.program_id(0) == SILU_TILE
    xn = xn_ref[...]
    for j in range(w_ref.shape[1] // MXU_COLS):
        cs = slice(j * MXU_COLS, (j + 1) * MXU_COLS)
        p = jnp.dot(xn, w_ref[:, cs], preferred_element_type=F32)
        o_ref[:, cs] = jax.nn.sigmoid(p) * jnp.where(is_silu, p, 1.0)


def _gates(xn, w_in, tile0, n_tiles, attn=None):
    rows = xn.shape[0]
    tm = min(rows, 1024)
    tn = GATE_TILE
    grid = (n_tiles, rows // tm)
    side = None if attn is None else _attn_side_plan(*attn, n_steps=grid[0] * grid[1],
                                                     step_of=lambda n, m: n * grid[1] + m)

    def wmap(n, m):
        return (0, jnp.where(tile0 + n == SILU_TILE, COL_GATE_A // tn, COL_MERGE_A // tn + tile0 + n))

    in_specs = [pl.BlockSpec((tm, D_MODEL), lambda n, m: (m, 0)), pl.BlockSpec((D_MODEL, tn), wmap)]
    args = [xn, w_in]
    out_specs = [pl.BlockSpec((tm, tn), lambda n, m: (m, n))]
    out_shape = [jax.ShapeDtypeStruct((rows, n_tiles * tn), F32)]
    scratch = []
    if side is not None:
        in_specs += side.in_specs
        args += side.args
        out_specs += side.out_specs
        out_shape += side.out_shape
        scratch += side.scratch
    outs = pl.pallas_call(
        functools.partial(_gates_kernel, tile0=tile0, side=None if side is None else side.static),
        grid=grid,
        in_specs=in_specs,
        out_specs=out_specs,
        out_shape=out_shape,
        scratch_shapes=scratch,
        compiler_params=_params(("arbitrary", "arbitrary")),
        name="gates",
    )(*args)
    return outs[0] if side is None else (outs[0], side.finish(outs[1], outs[2]))


def _t5_bucket(dist):
    d = np.maximum(dist, 1).astype(np.float32)
    large = MAX_EXACT + (np.log(d / np.float32(MAX_EXACT)) / np.float32(math.log(MAX_DISTANCE / MAX_EXACT))
                         * np.float32(NUM_BUCKETS - MAX_EXACT)).astype(np.int32)
    large = np.minimum(large, NUM_BUCKETS - 1)
    return np.where(dist < MAX_EXACT, dist, large)


def _bias_by_offset(rel_bias):
    steps = N_BACK - np.arange(N_BACK + 1)
    idx = np.stack([_t5_bucket(d * steps) for d in DILATIONS])
    g_idx = np.arange(N_GROUPS)[:, None]
    vals = jnp.transpose(rel_bias[idx, g_idx], (0, 2, 1)).astype(F32)
    return jnp.pad(vals, ((0, 0), (0, 0), (0, 2 * BLOCK - N_BACK - 1)), constant_values=NEG_INF)


def _sample_bias(cvec):
    kv_head = np.arange(N_BACK * 2 * N_SLOTS) % (2 * N_SLOTS)
    own_k_row = kv_head[None, :] == np.arange(N_SLOTS)[:, None]
    per_row = jnp.repeat(cvec[:, :, :N_BACK], 2 * N_SLOTS, axis=2)
    return jnp.where(own_k_row[None], per_row, NEG_INF), cvec[:, :, N_BACK:N_BACK + 1]


class _SideStatic(NamedTuple):
    per_step: int
    nb: int
    steps: int


class _SidePlan(NamedTuple):
    static: _SideStatic
    in_specs: list
    args: list
    out_specs: list
    out_shape: list
    scratch: list
    finish: Callable


def _attn_side(side, step, cvec_ref, q_ref, kp_ref, kc_ref, vp_ref, vc_ref, o_ref, lse_ref, bias_ref, s_ref, p_ref):
    @pl.when(step == 0)
    def _():
        for h in range(N_SLOTS):
            row = jnp.broadcast_to(cvec_ref[h:h + 1, :], (BLOCK, 2 * BLOCK))
            full = pltpu.roll(row, 0, 1, stride=1, stride_axis=0)
            bias_ref[1, h] = full
            bias_ref[0, h, :, :BLOCK] = jnp.full((BLOCK, BLOCK), NEG_INF, F32)
            bias_ref[0, h, :, BLOCK:] = full[:, BLOCK:]

    nt = (((1,), (1,)), ((), ()))
    heads = [slice(h * HEAD_DIM, (h + 1) * HEAD_DIM) for h in range(N_SLOTS)]
    unit0 = jnp.minimum(step, side.steps - 1) * side.per_step
    lane = lax.broadcasted_iota(jnp.int32, (BLOCK, LANES), 1)
    for u in range(side.per_step):
        rows = slice(u * BLOCK, (u + 1) * BLOCK)
        prev = slice((u - 1) * BLOCK, u * BLOCK)
        for h, sl in enumerate(heads):
            q = q_ref[rows, sl]
            k_prev = kp_ref[:, sl] if u == 0 else kc_ref[prev, sl]
            s_ref[u, h, :, :BLOCK] = lax.dot_general(q, k_prev, nt, preferred_element_type=F32)
            s_ref[u, h, :, BLOCK:] = lax.dot_general(q, kc_ref[rows, sl], nt, preferred_element_type=F32)
    ls, lses = [], []
    for u in range(side.per_step):
        has_prev = jnp.minimum((unit0 + u) % side.nb, 1)
        sc = s_ref[u] * SCALE + bias_ref[has_prev]
        mx = jnp.max(jnp.maximum(sc[:, :, :BLOCK], sc[:, :, BLOCK:]), axis=-1, keepdims=True)
        p = jnp.exp(sc - mx)
        l = jnp.sum(p[:, :, :BLOCK] + p[:, :, BLOCK:], axis=-1, keepdims=True)
        p_ref[u] = p.astype(BF16)
        ls.append(l)
        lses.append(mx + jnp.log(l))
    for u in range(side.per_step):
        rows = slice(u * BLOCK, (u + 1) * BLOCK)
        prev = slice((u - 1) * BLOCK, u * BLOCK)
        lse_all = jnp.zeros((BLOCK, LANES), F32)
        for h, sl in enumerate(heads):
            v_prev = vp_ref[:, sl] if u == 0 else vc_ref[prev, sl]
            o = (jnp.dot(p_ref[u, h, :, :BLOCK], v_prev, preferred_element_type=F32)
                 + jnp.dot(p_ref[u, h, :, BLOCK:], vc_ref[rows, sl], preferred_element_type=F32))
            o_ref[rows, sl] = o / ls[u][h]
            lse_all = jnp.where(lane == h, lses[u][h], lse_all)
        lse_ref[rows, :] = lse_all


def _attn_side_plan(qkv, cvec, g, n_steps, step_of):
    _, dil, ls, _ = qkv.shape
    nb = ls // BLOCK
    units = dil * nb
    per_step = min(d for d in range(1, units + 1) if units % d == 0 and d * n_steps >= units)
    static = _SideStatic(per_step=per_step, nb=nb, steps=units // per_step)
    flat = qkv.reshape(3, dil * ls, D_ATTN)
    rows = per_step * BLOCK

    def cur(kind):
        return pl.BlockSpec((None, rows, D_ATTN), lambda *idx: (kind, jnp.minimum(step_of(*idx), static.steps - 1), 0))

    def prev(kind):
        def index(*idx):
            first_unit = jnp.minimum(step_of(*idx), static.steps - 1) * per_step
            return (kind, jnp.maximum(first_unit - 1, 0), 0)
        return pl.BlockSpec((None, BLOCK, D_ATTN), index)

    def out_spec(width):
        return pl.BlockSpec((rows, width), lambda *idx: (jnp.minimum(step_of(*idx), static.steps - 1), 0))

    return _SidePlan(
        static=static,
        in_specs=[pl.BlockSpec((None, N_SLOTS, 2 * BLOCK), lambda *idx: (g, 0, 0)),
                  cur(0), prev(1), cur(1), prev(2), cur(2)],
        args=[cvec, flat, flat, flat, flat, flat],
        out_specs=[out_spec(D_ATTN), out_spec(LANES)],
        out_shape=[jax.ShapeDtypeStruct((dil * ls, D_ATTN), F32), jax.ShapeDtypeStruct((dil * ls, LANES), F32)],
        scratch=[pltpu.VMEM((2, N_SLOTS, BLOCK, 2 * BLOCK), F32),
                 pltpu.VMEM((per_step, N_SLOTS, BLOCK, 2 * BLOCK), F32),
                 pltpu.VMEM((per_step, N_SLOTS, BLOCK, 2 * BLOCK), BF16)],
        finish=lambda o, lse: (o.reshape(dil, ls, D_ATTN), lse.reshape(dil, ls, LANES)),
    )


def _attn_sample_kernel(bias_ref, nbias_ref, q0_ref, q1_ref, q2_ref, kv0_ref, kv1_ref, kv2_ref,
                        c0_ref, c1_ref, c2_ref, o_ref, slab_ref, s_ref, p_ref, og_ref, *, tb):
    rows = pl.ds(pl.program_id(0) * tb, tb)
    nt = (((1,), (1,)), ((), ()))
    kv_rows = 2 * N_SLOTS
    slab_rows = N_BACK * kv_rows
    q_refs = (q0_ref, q1_ref, q2_ref)
    kv_refs = (kv0_ref, kv1_ref, kv2_ref)
    c_refs = (c0_ref, c1_ref, c2_ref)

    def rounded(x):
        return x.astype(BF16).astype(F32)

    qs = [q_ref[rows] for q_ref in q_refs]
    for g in range(N_GROUPS):
        for t in range(tb):
            slab_ref[g, t] = c_refs[g][t].reshape(slab_rows, HEAD_DIM).astype(BF16)
            s_ref[g, t] = lax.dot_general(qs[g][t].astype(BF16), slab_ref[g, t], nt, preferred_element_type=F32)
    p_news, ls, lses = [], [], []
    for g in range(N_GROUPS):
        s_n = jnp.sum(rounded(qs[g]) * rounded(kv_refs[g][rows, 0]), axis=-1, keepdims=True)
        sc_c = s_ref[g] * SCALE + bias_ref[g][None]
        sc_n = s_n * SCALE + nbias_ref[g][None]
        mx = jnp.maximum(jnp.max(sc_c, axis=-1, keepdims=True), sc_n)
        p_c = jnp.exp(sc_c - mx)
        p_n = jnp.exp(sc_n - mx)
        l = jnp.sum(p_c, axis=-1, keepdims=True) + p_n
        p_ref[g] = pltpu.roll(p_c, N_SLOTS, 2).astype(BF16)
        p_news.append(p_n)
        ls.append(l)
        lses.append(mx + jnp.log(l))
    for g in range(N_GROUPS):
        for t in range(tb):
            og_ref[g, t] = jnp.dot(p_ref[g, t], slab_ref[g, t], preferred_element_type=F32)
    outs = [(og_ref[g] + rounded(p_news[g]) * rounded(kv_refs[g][rows, 1])) / ls[g] for g in range(N_GROUPS)]
    top = jnp.maximum(jnp.maximum(lses[0], lses[1]), lses[2])
    es = [jnp.exp(x - top) for x in lses]
    tot = es[0] + es[1] + es[2]
    o_ref[rows] = (es[0] / tot) * outs[0] + (es[1] / tot) * outs[1] + (es[2] / tot) * outs[2]


def _attn_sample(qs, kvs, caches, bias, nbias):
    db = qs[0].shape[0]
    tb = 4
    kv_rows = 2 * N_SLOTS
    views, specs = [], []
    for g, c in enumerate(caches):
        dil = DILATIONS[g]
        lc = c.shape[1]
        views.append(c.reshape(db, lc // dil, dil * kv_rows, HEAD_DIM))
        specs.append(pl.BlockSpec((tb, N_BACK, kv_rows, HEAD_DIM), lambda s: (s, 0, 0, 0)))
    q_spec = pl.BlockSpec((db, N_SLOTS, HEAD_DIM), lambda s: (0, 0, 0))
    kv_spec = pl.BlockSpec((db, 2, N_SLOTS, HEAD_DIM), lambda s: (0, 0, 0, 0))
    return pl.pallas_call(
        functools.partial(_attn_sample_kernel, tb=tb),
        grid=(db // tb,),
        in_specs=[pl.BlockSpec((N_GROUPS, N_SLOTS, N_BACK * kv_rows), lambda s: (0, 0, 0)),
                  pl.BlockSpec((N_GROUPS, N_SLOTS, 1), lambda s: (0, 0, 0)),
                  q_spec, q_spec, q_spec, kv_spec, kv_spec, kv_spec] + specs,
        out_specs=pl.BlockSpec((db, N_SLOTS, HEAD_DIM), lambda s: (0, 0, 0)),
        out_shape=jax.ShapeDtypeStruct((db, N_SLOTS, HEAD_DIM), F32),
        scratch_shapes=[pltpu.VMEM((N_GROUPS, tb, N_BACK * kv_rows, HEAD_DIM), BF16),
                        pltpu.VMEM((N_GROUPS, tb, N_SLOTS, N_BACK * kv_rows), F32),
                        pltpu.VMEM((N_GROUPS, tb, N_SLOTS, N_BACK * kv_rows), BF16),
                        pltpu.VMEM((N_GROUPS, tb, N_SLOTS, HEAD_DIM), F32)],
        compiler_params=_params(("arbitrary",)),
        name="attn_sample",
    )(bias, nbias, *qs, *kvs, *views)


def _out_kernel(*refs, tm, n_groups):
    o_refs = refs[:n_groups]
    lse_refs = refs[n_groups:2 * n_groups] if n_groups > 1 else ()
    rest = refs[len(o_refs) + len(lse_refs):]
    if n_groups > 1:
        sga_ref, sma_ref, smb_ref, yb_ref, x_ref, woa_ref, wob_ref, wo_ref, y_ref, a_ref, og_ref, lg_ref = rest
        for g in range(n_groups):
            dil = DILATIONS[g]
            sub = tm // dil
            for r in range(dil):
                rows = slice(None) if dil == 1 else pl.ds(r, sub, stride=dil)
                lg_ref[g, rows, :] = lse_refs[g][r]
                for h in range(N_SLOTS):
                    og_ref[g, h, rows, :] = o_refs[g][r, :, h * HEAD_DIM:(h + 1) * HEAD_DIM]
        lses = [lg_ref[g] for g in range(n_groups)]
        top = functools.reduce(jnp.maximum, lses)
        es = [jnp.exp(x - top) for x in lses]
        tot = functools.reduce(lambda a, b: a + b, es)
        ws = [e / tot for e in es]
        for h in range(N_SLOTS):
            sl = slice(h * HEAD_DIM, (h + 1) * HEAD_DIM)
            o = functools.reduce(lambda a, b: a + b,
                                 [ws[g][:, h:h + 1] * og_ref[g, h] for g in range(n_groups)])
            a_ref[:, sl] = (o * sga_ref[:, sl]).astype(BF16)
    else:
        sga_ref, sma_ref, smb_ref, yb_ref, x_ref, woa_ref, wob_ref, wo_ref, y_ref, a_ref = rest
        a_ref[...] = (o_refs[0][...] * sga_ref[...]).astype(BF16)
    ya = jnp.dot(a_ref[...], woa_ref[...], preferred_element_type=F32)
    yb = jnp.dot(yb_ref[...], wob_ref[...], preferred_element_type=F32)
    merged = sma_ref[...] * ya + smb_ref[...] * yb
    y_ref[...] = x_ref[...] + jnp.dot(merged.astype(BF16), wo_ref[...], preferred_element_type=F32)


def _out_proj(os_, lses, gates, yb_in, x, w_out_a, w_out_b, w_o):
    rows = x.shape[0]

    def gate_operand(tile, width):
        arr, tile0 = next((a, t0) for a, t0 in gates if t0 <= tile < t0 + a.shape[1] // GATE_TILE)
        assert (tile - tile0) * GATE_TILE % width == 0
        return arr, pl.BlockSpec((tm, width), lambda m: (m, (tile - tile0) * GATE_TILE // width))

    tm = min(rows, 256)
    n_groups = len(os_)

    def row_spec(width, col_block=0):
        return pl.BlockSpec((tm, width), lambda m: (m, col_block))

    def sub_spec(dil, width):
        return pl.BlockSpec((dil, tm // dil, width), lambda m: (0, m, 0))

    def const_spec(shape):
        return pl.BlockSpec(shape, lambda m: (0, 0), pipeline_mode=pl.Buffered(1))

    scratch = [pltpu.VMEM((tm, D_ATTN), BF16)]
    if n_groups > 1:
        in_specs = ([sub_spec(DILATIONS[g], D_ATTN) for g in range(n_groups)]
                    + [sub_spec(DILATIONS[g], LANES) for g in range(n_groups)])
        scratch += [pltpu.VMEM((n_groups, N_SLOTS, tm, HEAD_DIM), F32), pltpu.VMEM((n_groups, tm, LANES), F32)]
    else:
        in_specs = [row_spec(D_ATTN)]
    sga, sga_spec = gate_operand(SILU_TILE, D_ATTN)
    sma, sma_spec = gate_operand(0, D_MODEL)
    smb, smb_spec = gate_operand(D_MODEL // GATE_TILE, D_MODEL)
    in_specs += [sga_spec, sma_spec, smb_spec, row_spec(D_CONV), row_spec(D_MODEL),
                 const_spec((D_ATTN, D_MODEL)), const_spec((D_CONV, D_MODEL)), const_spec((D_MODEL, D_MODEL))]
    return pl.pallas_call(
        functools.partial(_out_kernel, tm=tm, n_groups=n_groups),
        grid=(rows // tm,),
        in_specs=in_specs,
        out_specs=row_spec(D_MODEL),
        out_shape=jax.ShapeDtypeStruct((rows, D_MODEL), F32),
        scratch_shapes=scratch,
        compiler_params=_params(("arbitrary",)),
        name="out_proj",
    )(*os_, *lses, sga, sma, smb, yb_in, x, w_out_a, w_out_b, w_o)


def kernel(x_prompt, x_sample, cache_kv_w128, cache_kv_w512, cache_kv_w2048, state_conv, norm_gain, w_in,
           q_norm_gain, k_norm_gain, rel_bias, conv_w, w_out_a, w_out_b, w_o):
    seq = x_prompt.shape[1]
    db = x_sample.shape[0]
    xp = x_prompt.reshape(seq, D_MODEL)
    xs = x_sample.reshape(db, D_MODEL)
    caches = (cache_kv_w128, cache_kv_w512, cache_kv_w2048)
    qk_gains = jnp.stack([jnp.tile(q_norm_gain, N_SLOTS), jnp.tile(k_norm_gain, N_SLOTS)]).reshape(2, 1, D_ATTN)
    w_in, woa, wob, wo = _to_bf16(w_in), _to_bf16(w_out_a), _to_bf16(w_out_b), _to_bf16(w_o)
    cvec = _bias_by_offset(rel_bias)

    xn = _pre_norm(xp, norm_gain)
    qkvs, kv_p = zip(*[_qkv_proj(xn, w_in, qk_gains, g, DILATIONS[g], BF16, min(WINDOWS[g], seq))
                       for g in range(N_GROUPS)])
    gates_a, attn0 = _gates(xn, w_in, 0, 2, attn=(qkvs[0], cvec, 0))
    gates_b, attn1 = _gates(xn, w_in, 2, 2, attn=(qkvs[1], cvec, 1))
    gates_c = _gates(xn, w_in, SILU_TILE, 1)
    yb_in, h_last, attn2 = _branch_b(xn, w_in, conv_w, attn=(qkvs[2], cvec, 2))
    gates = [(gates_a, 0), (gates_b, 2), (gates_c, SILU_TILE)]
    os_, lses = zip(attn0, attn1, attn2)
    y_prompt = _out_proj(os_, lses, gates, yb_in, xp, woa, wob, wo).reshape(1, seq, D_MODEL)
    conv_p = h_last[SUBLANES - (CONV_W - 1):][None]

    xn_s = _pre_norm(xs, norm_gain)
    qkvs_s, kv_s = zip(*[_qkv_proj(xn_s, w_in, qk_gains, g, 1, F32, db) for g in range(N_GROUPS)])
    yb_in_s, h_s = _branch_b(xn_s, w_in, conv_w, state=state_conv)
    gates_s = [(_gates(xn_s, w_in, 0, N_GATE_TILES), 0)]
    qs_s = [qkv[0, 0].reshape(db, N_SLOTS, HEAD_DIM) for qkv in qkvs_s]
    o_s = _attn_sample(qs_s, kv_s, caches, *_sample_bias(cvec)).reshape(db, D_ATTN)
    y_sample = _out_proj((o_s,), (), gates_s, yb_in_s, xs, woa, wob, wo).reshape(db, 1, D_MODEL)
    conv_s = jnp.stack([state_conv[:, CONV_W - 2], h_s], axis=1)

    return (y_prompt, y_sample, kv_p[0][None], kv_p[1][None], kv_p[2][None], conv_p,
            kv_s[0][:, None], kv_s[1][:, None], kv_s[2][:, None], conv_s)
```

```python
import functools
import math
from typing import Callable, NamedTuple

import numpy as np
import jax
import jax.numpy as jnp
from jax import lax
from jax.experimental import pallas as pl
from jax.experimental.pallas import tpu as pltpu

D_MODEL = 2048
N_GROUPS = 3
DILATIONS = (1, 4, 16)
N_BACK = 128
WINDOWS = (128, 512, 2048)
N_SLOTS = 8
HEAD_DIM = 128
D_ATTN = N_SLOTS * HEAD_DIM
QKV_COLS = N_GROUPS * D_ATTN
D_CONV = D_MODEL // 2
CONV_W = 3
BLOCK = N_BACK
NUM_BUCKETS = 32
MAX_EXACT = NUM_BUCKETS // 2
MAX_DISTANCE = 2048
EPS = 1e-6
SCALE = HEAD_DIM ** -0.5

COL_K = QKV_COLS
COL_V = 2 * QKV_COLS
COL_GATE_A = 3 * QKV_COLS
COL_B = COL_GATE_A + D_ATTN
COL_C = COL_B + D_CONV
COL_XIN = COL_C + D_CONV
COL_GATE_B = COL_XIN + D_CONV
COL_MERGE_A = COL_GATE_B + D_CONV
COL_MERGE_B = COL_MERGE_A + D_MODEL

SUBLANES = 8
LANES = 128
MXU_COLS = 256
VMEM_LIMIT_BYTES = 56 * 1024 * 1024

BF16 = jnp.bfloat16
F32 = jnp.float32
NEG_INF = float("-inf")


def _params(semantics):
    return pltpu.CompilerParams(dimension_semantics=semantics, vmem_limit_bytes=VMEM_LIMIT_BYTES)


def _cast_kernel(w_ref, o_ref):
    o_ref[...] = w_ref[...].astype(o_ref.dtype)


def _to_bf16(w):
    rows, cols = w.shape
    tn = 512
    return pl.pallas_call(
        _cast_kernel,
        grid=(cols // tn,),
        in_specs=[pl.BlockSpec((rows, tn), lambda n: (0, n))],
        out_specs=pl.BlockSpec((rows, tn), lambda n: (0, n)),
        out_shape=jax.ShapeDtypeStruct((rows, cols), BF16),
        compiler_params=_params(("arbitrary",)),
        name="to_bf16",
    )(w)


def _norm_kernel(x_ref, g_ref, o_ref):
    x = x_ref[...]
    r = lax.rsqrt(jnp.mean(x * x, axis=-1, keepdims=True) + EPS)
    o_ref[...] = ((x * r) * g_ref[...]).astype(o_ref.dtype)


def _pre_norm(x, gain):
    rows = x.shape[0]
    tm = min(rows, 512)
    return pl.pallas_call(
        _norm_kernel,
        grid=(rows // tm,),
        in_specs=[pl.BlockSpec((tm, D_MODEL), lambda m: (m, 0)),
                  pl.BlockSpec((1, D_MODEL), lambda m: (0, 0))],
        out_specs=pl.BlockSpec((tm, D_MODEL), lambda m: (m, 0)),
        out_shape=jax.ShapeDtypeStruct((rows, D_MODEL), BF16),
        compiler_params=_params(("arbitrary",)),
        name="pre_norm",
    )(x, gain.reshape(1, D_MODEL))


def _residue_stride(dil, sub):
    return sub + SUBLANES if dil > 4 else sub


def _qkv_kernel(xn_ref, w_ref, gain_ref, *refs, dil, tm, tr, m_tail0, n_casts, w_is_f32):
    cast_in, (o_ref, tail_ref), cast_out, (res_ref,), scratch = (
        refs[:n_casts], refs[n_casts:n_casts + 2], refs[n_casts + 2:2 * n_casts + 2],
        refs[2 * n_casts + 2:2 * n_casts + 3], refs[2 * n_casts + 3:])
    kind = pl.program_id(0)
    m = pl.program_id(1)
    if w_is_f32:
        w_f32_ref, (w_ref,) = w_ref, scratch

        @pl.when(m == 0)
        def _():
            w_ref[...] = w_f32_ref[...].astype(BF16)

    normed = kind < 2
    xn = xn_ref[...]
    sub = tm // dil
    pitch = _residue_stride(dil, sub)
    scatter = dil > 4
    heads_per_chunk = MXU_COLS // HEAD_DIM
    n_chunks = D_ATTN // MXU_COLS
    for j in range(n_chunks):
        p = jnp.dot(xn, w_ref[:, j * MXU_COLS:(j + 1) * MXU_COLS], preferred_element_type=F32)
        for src_ref, dst_ref in zip(cast_in, cast_out):
            part = src_ref.shape[0] // n_chunks
            dst_ref[j * part:(j + 1) * part, :] = src_ref[j * part:(j + 1) * part, :].astype(BF16)
        for hh in range(heads_per_chunk):
            h = j * heads_per_chunk + hh
            sl = slice(h * HEAD_DIM, (h + 1) * HEAD_DIM)
            ph = p[:, hh * HEAD_DIM:(hh + 1) * HEAD_DIM]
            r = lax.rsqrt(jnp.mean(ph * ph, axis=-1, keepdims=True) + EPS)
            res = (ph * jnp.where(normed, r, 1.0)) * jnp.where(normed, gain_ref[:, sl], 1.0)
            if scatter:
                for i in range(sub):
                    res_ref[h, pl.ds(i, dil, stride=pitch), :] = res[i * dil:(i + 1) * dil]
            else:
                res_ref[h] = res
            for rr in range(dil):
                if dil == 1:
                    rows = res
                elif scatter:
                    rows = res_ref[h, rr * pitch:rr * pitch + sub, :]
                else:
                    rows = res_ref[h, pl.ds(rr, sub, stride=dil), :]
                o_ref[rr, :, sl] = rows.astype(o_ref.dtype)

    @pl.when(jnp.logical_and(kind >= 1, m >= m_tail0))
    def _():
        for h in range(N_SLOTS):
            if scatter:
                i0 = (tm - tr) // dil
                for rr in range(dil):
                    tail_ref[pl.ds(rr, sub - i0, stride=dil), h, :] = res_ref[h, rr * pitch + i0:rr * pitch + sub, :]
            else:
                tail_ref[:, h, :] = res_ref[h, tm - tr:tm, :]


def _qkv_proj(xn, w_in, qk_gains, g, dil, out_dtype, tail_rows, tm=1024, casts=()):
    rows = xn.shape[0]
    tm = min(rows, tm)
    tn = D_ATTN
    tr = min(tail_rows, tm)
    m_tail0 = rows // tm - tail_rows // tr
    grid = (3, rows // tm)
    n_steps = grid[0] * grid[1]
    w_is_f32 = w_in.dtype == F32

    def tail_map(kind, m):
        return (jnp.where(kind >= 1, jnp.maximum(m - m_tail0, 0), 0), jnp.maximum(kind - 1, 0), 0, 0)

    def cast_spec(w):
        n_blocks = max(d for d in range(1, n_steps + 1) if w.shape[1] % (d * LANES) == 0)
        return pl.BlockSpec((w.shape[0], w.shape[1] // n_blocks),
                            lambda kind, m: (0, jnp.minimum(kind * grid[1] + m, n_blocks - 1)))

    cast_specs = [cast_spec(w) for w in casts]
    scratch = [pltpu.VMEM((N_SLOTS, dil * _residue_stride(dil, tm // dil), HEAD_DIM), F32)]
    if w_is_f32:
        scratch.append(pltpu.VMEM((D_MODEL, tn), BF16))
    return pl.pallas_call(
        functools.partial(_qkv_kernel, dil=dil, tm=tm, tr=tr, m_tail0=m_tail0, n_casts=len(casts), w_is_f32=w_is_f32),
        grid=grid,
        in_specs=[pl.BlockSpec((tm, D_MODEL), lambda kind, m: (m, 0)),
                  pl.BlockSpec((D_MODEL, tn), lambda kind, m: (0, kind * N_GROUPS + g)),
                  pl.BlockSpec((None, 1, tn), lambda kind, m: (jnp.minimum(kind, 1), 0, 0))] + cast_specs,
        out_specs=[pl.BlockSpec((None, dil, tm // dil, tn), lambda kind, m: (kind, 0, m, 0)),
                   pl.BlockSpec((tr, None, N_SLOTS, HEAD_DIM), tail_map)] + cast_specs,
        out_shape=[jax.ShapeDtypeStruct((3, dil, rows // dil, tn), out_dtype),
                   jax.ShapeDtypeStruct((tail_rows, 2, N_SLOTS, HEAD_DIM), F32)]
                  + [jax.ShapeDtypeStruct(w.shape, BF16) for w in casts],
        scratch_shapes=scratch,
        compiler_params=_params(("arbitrary", "arbitrary")),
        name=f"qkv_proj_g{g}",
    )(xn, w_in, qk_gains, *casts)


def _branch_b_kernel(*refs, tm, per_row_state, side):
    if per_row_state:
        xn_ref, wb_ref, wc_ref, wx_ref, wg_ref, cw_ref, hm2_ref, hm1_ref, y_ref, h_ref = refs
    else:
        (xn_ref, wb_ref, wc_ref, wx_ref, wg_ref, cw_ref), refs = refs[:6], refs[6:]
        if side is not None:
            side_in, (y_ref, h_ref), side_out, (hbuf_ref,), side_scratch = (
                refs[:6], refs[6:8], refs[8:10], refs[10:11], refs[11:])
        else:
            y_ref, h_ref, hbuf_ref = refs

        @pl.when(pl.program_id(1) == 0)
        def _():
            hbuf_ref[0:SUBLANES, :] = jnp.zeros((SUBLANES, hbuf_ref.shape[1]), F32)

        if side is not None:
            _attn_side(side, pl.program_id(0) * pl.num_programs(1) + pl.program_id(1),
                       *side_in, *side_out, *side_scratch)

    xn = xn_ref[...]
    pb = jnp.dot(xn, wb_ref[...], preferred_element_type=F32)
    pc = jnp.dot(xn, wc_ref[...], preferred_element_type=F32)
    px = jnp.dot(xn, wx_ref[...], preferred_element_type=F32)
    pg = jnp.dot(xn, wg_ref[...], preferred_element_type=F32)
    h = pc * px
    if per_row_state:
        hm2 = hm2_ref[...]
        hm1 = hm1_ref[...]
        h_ref[...] = h
    else:
        hbuf_ref[SUBLANES:SUBLANES + tm, :] = h
        hm2 = hbuf_ref[pl.ds(SUBLANES - 2, tm), :]
        hm1 = hbuf_ref[pl.ds(SUBLANES - 1, tm), :]
        last = hbuf_ref[pl.ds(tm, SUBLANES), :]
        h_ref[...] = last
        hbuf_ref[0:SUBLANES, :] = last
    z = cw_ref[0:1, :] * hm2 + cw_ref[1:2, :] * hm1 + cw_ref[2:3, :] * h
    y_ref[...] = (pb * z * (pg * jax.nn.sigmoid(pg))).astype(y_ref.dtype)


def _branch_b(xn, w_in, conv_w, state=None, attn=None):
    rows = xn.shape[0]
    tm = min(rows, 1024)
    tnb = 256
    per_row_state = state is not None
    grid = (D_CONV // tnb, rows // tm)
    side = None if attn is None else _attn_side_plan(*attn, n_steps=grid[0] * grid[1],
                                                     step_of=lambda j, m: j * grid[1] + m)

    def wspec(col0):
        return pl.BlockSpec((D_MODEL, tnb), lambda j, m: (0, col0 // tnb + j))

    in_specs = [pl.BlockSpec((tm, D_MODEL), lambda j, m: (m, 0)),
                wspec(COL_B), wspec(COL_C), wspec(COL_XIN), wspec(COL_GATE_B),
                pl.BlockSpec((CONV_W, tnb), lambda j, m: (0, j))]
    args = [xn, w_in, w_in, w_in, w_in, conv_w]
    scratch = []
    if per_row_state:
        st = state.reshape(rows, (CONV_W - 1) * D_CONV)
        in_specs += [pl.BlockSpec((tm, tnb), lambda j, m: (m, j)),
                     pl.BlockSpec((tm, tnb), lambda j, m: (m, D_CONV // tnb + j))]
        args += [st, st]
        h_spec = pl.BlockSpec((tm, tnb), lambda j, m: (m, j))
        h_shape = jax.ShapeDtypeStruct((rows, D_CONV), F32)
    else:
        scratch.append(pltpu.VMEM((tm + SUBLANES, tnb), F32))
        h_spec = pl.BlockSpec((SUBLANES, tnb), lambda j, m: (0, j))
        h_shape = jax.ShapeDtypeStruct((SUBLANES, D_CONV), F32)
    out_specs = [pl.BlockSpec((tm, tnb), lambda j, m: (m, j)), h_spec]
    out_shape = [jax.ShapeDtypeStruct((rows, D_CONV), BF16), h_shape]
    if side is not None:
        in_specs += side.in_specs
        args += side.args
        out_specs += side.out_specs
        out_shape += side.out_shape
        scratch += side.scratch
    outs = pl.pallas_call(
        functools.partial(_branch_b_kernel, tm=tm, per_row_state=per_row_state,
                          side=None if side is None else side.static),
        grid=grid,
        in_specs=in_specs,
        out_specs=out_specs,
        out_shape=out_shape,
        scratch_shapes=scratch,
        compiler_params=_params(("arbitrary", "arbitrary")),
        name="branch_b",
    )(*args)
    return outs if side is None else (outs[0], outs[1], side.finish(outs[2], outs[3]))


GATE_TILE = 1024
N_GATE_TILES = (2 * D_MODEL + D_ATTN) // GATE_TILE
SILU_TILE = N_GATE_TILES - 1


def _gates_kernel(xn_ref, w_ref, *refs, tile0, side):
    if side is not None:
        side_in, o_ref, side_out, side_scratch = refs[:6], refs[6], refs[7:9], refs[9:]
        _attn_side(side, pl.program_id(0) * pl.num_programs(1) + pl.program_id(1),
                   *side_in, *side_out, *side_scratch)
    else:
        o_ref, = refs
    is_silu = tile0 + pl.program_id(0) == SILU_TILE
    xn = xn_ref[...]
    for j in range(w_ref.shape[1] // MXU_COLS):
        cs = slice(j * MXU_COLS, (j + 1) * MXU_COLS)
        p = jnp.dot(xn, w_ref[:, cs], preferred_element_type=F32)
        o_ref[:, cs] = jax.nn.sigmoid(p) * jnp.where(is_silu, p, 1.0)


def _gates(xn, w_in, tile0, n_tiles, attn=None):
    rows = xn.shape[0]
    tm = min(rows, 1024)
    tn = GATE_TILE
    grid = (n_tiles, rows // tm)
    side = None if attn is None else _attn_side_plan(*attn, n_steps=grid[0] * grid[1],
                                                     step_of=lambda n, m: n * grid[1] + m)

    def wmap(n, m):
        return (0, jnp.where(tile0 + n == SILU_TILE, COL_GATE_A // tn, COL_MERGE_A // tn + tile0 + n))

    in_specs = [pl.BlockSpec((tm, D_MODEL), lambda n, m: (m, 0)), pl.BlockSpec((D_MODEL, tn), wmap)]
    args = [xn, w_in]
    out_specs = [pl.BlockSpec((tm, tn), lambda n, m: (m, n))]
    out_shape = [jax.ShapeDtypeStruct((rows, n_tiles * tn), F32)]
    scratch = []
    if side is not None:
        in_specs += side.in_specs
        args += side.args
        out_specs += side.out_specs
        out_shape += side.out_shape
        scratch += side.scratch
    outs = pl.pallas_call(
        functools.partial(_gates_kernel, tile0=tile0, side=None if side is None else side.static),
        grid=grid,
        in_specs=in_specs,
        out_specs=out_specs,
        out_shape=out_shape,
        scratch_shapes=scratch,
        compiler_params=_params(("arbitrary", "arbitrary")),
        name="gates",
    )(*args)
    return outs[0] if side is None else (outs[0], side.finish(outs[1], outs[2]))


def _t5_bucket(dist):
    d = np.maximum(dist, 1).astype(np.float32)
    large = MAX_EXACT + (np.log(d / np.float32(MAX_EXACT)) / np.float32(math.log(MAX_DISTANCE / MAX_EXACT))
                         * np.float32(NUM_BUCKETS - MAX_EXACT)).astype(np.int32)
    large = np.minimum(large, NUM_BUCKETS - 1)
    return np.where(dist < MAX_EXACT, dist, large)


def _bias_by_offset(rel_bias):
    steps = N_BACK - np.arange(N_BACK + 1)
    idx = np.stack([_t5_bucket(d * steps) for d in DILATIONS])
    g_idx = np.arange(N_GROUPS)[:, None]
    vals = jnp.transpose(rel_bias[idx, g_idx], (0, 2, 1)).astype(F32)
    return jnp.pad(vals, ((0, 0), (0, 0), (0, 2 * BLOCK - N_BACK - 1)), constant_values=NEG_INF)


def _sample_bias(cvec):
    kv_head = np.arange(N_BACK * 2 * N_SLOTS) % (2 * N_SLOTS)
    own_k_row = kv_head[None, :] == np.arange(N_SLOTS)[:, None]
    per_row = jnp.repeat(cvec[:, :, :N_BACK], 2 * N_SLOTS, axis=2)
    return jnp.where(own_k_row[None], per_row, NEG_INF), cvec[:, :, N_BACK:N_BACK + 1]


class _SideStatic(NamedTuple):
    per_step: int
    nb: int
    steps: int


class _SidePlan(NamedTuple):
    static: _SideStatic
    in_specs: list
    args: list
    out_specs: list
    out_shape: list
    scratch: list
    finish: Callable


def _attn_side(side, step, cvec_ref, q_ref, kp_ref, kc_ref, vp_ref, vc_ref, o_ref, lse_ref, bias_ref, s_ref, p_ref):
    @pl.when(step == 0)
    def _():
        for h in range(N_SLOTS):
            row = jnp.broadcast_to(cvec_ref[h:h + 1, :], (BLOCK, 2 * BLOCK))
            full = pltpu.roll(row, 0, 1, stride=1, stride_axis=0)
            bias_ref[1, h] = full
            bias_ref[0, h, :, :BLOCK] = jnp.full((BLOCK, BLOCK), NEG_INF, F32)
            bias_ref[0, h, :, BLOCK:] = full[:, BLOCK:]

    nt = (((1,), (1,)), ((), ()))
    heads = [slice(h * HEAD_DIM, (h + 1) * HEAD_DIM) for h in range(N_SLOTS)]
    unit0 = jnp.minimum(step, side.steps - 1) * side.per_step
    lane = lax.broadcasted_iota(jnp.int32, (BLOCK, LANES), 1)
    for u in range(side.per_step):
        rows = slice(u * BLOCK, (u + 1) * BLOCK)
        prev = slice((u - 1) * BLOCK, u * BLOCK)
        for h, sl in enumerate(heads):
            q = q_ref[rows, sl]
            k_prev = kp_ref[:, sl] if u == 0 else kc_ref[prev, sl]
            s_ref[u, h, :, :BLOCK] = lax.dot_general(q, k_prev, nt, preferred_element_type=F32)
            s_ref[u, h, :, BLOCK:] = lax.dot_general(q, kc_ref[rows, sl], nt, preferred_element_type=F32)
    ls, lses = [], []
    for u in range(side.per_step):
        has_prev = jnp.minimum((unit0 + u) % side.nb, 1)
        sc = s_ref[u] * SCALE + bias_ref[has_prev]
        mx = jnp.max(jnp.maximum(sc[:, :, :BLOCK], sc[:, :, BLOCK:]), axis=-1, keepdims=True)
        p = jnp.exp(sc - mx)
        l = jnp.sum(p[:, :, :BLOCK] + p[:, :, BLOCK:], axis=-1, keepdims=True)
        p_ref[u] = p.astype(BF16)
        ls.append(l)
        lses.append(mx + jnp.log(l))
    for u in range(side.per_step):
        rows = slice(u * BLOCK, (u + 1) * BLOCK)
        prev = slice((u - 1) * BLOCK, u * BLOCK)
        lse_all = jnp.zeros((BLOCK, LANES), F32)
        for h, sl in enumerate(heads):
            v_prev = vp_ref[:, sl] if u == 0 else vc_ref[prev, sl]
            o = (jnp.dot(p_ref[u, h, :, :BLOCK], v_prev, preferred_element_type=F32)
                 + jnp.dot(p_ref[u, h, :, BLOCK:], vc_ref[rows, sl], preferred_element_type=F32))
            o_ref[rows, sl] = o / ls[u][h]
            lse_all = jnp.where(lane == h, lses[u][h], lse_all)
        lse_ref[rows, :] = lse_all


def _attn_side_plan(qkv, cvec, g, n_steps, step_of):
    _, dil, ls, _ = qkv.shape
    nb = ls // BLOCK
    units = dil * nb
    per_step = min(d for d in range(1, units + 1) if units % d == 0 and d * n_steps >= units)
    static = _SideStatic(per_step=per_step, nb=nb, steps=units // per_step)
    flat = qkv.reshape(3, dil * ls, D_ATTN)
    rows = per_step * BLOCK

    def cur(kind):
        return pl.BlockSpec((None, rows, D_ATTN), lambda *idx: (kind, jnp.minimum(step_of(*idx), static.steps - 1), 0))

    def prev(kind):
        def index(*idx):
            first_unit = jnp.minimum(step_of(*idx), static.steps - 1) * per_step
            return (kind, jnp.maximum(first_unit - 1, 0), 0)
        return pl.BlockSpec((None, BLOCK, D_ATTN), index)

    def out_spec(width):
        return pl.BlockSpec((rows, width), lambda *idx: (jnp.minimum(step_of(*idx), static.steps - 1), 0))

    return _SidePlan(
        static=static,
        in_specs=[pl.BlockSpec((None, N_SLOTS, 2 * BLOCK), lambda *idx: (g, 0, 0)),
                  cur(0), prev(1), cur(1), prev(2), cur(2)],
        args=[cvec, flat, flat, flat, flat, flat],
        out_specs=[out_spec(D_ATTN), out_spec(LANES)],
        out_shape=[jax.ShapeDtypeStruct((dil * ls, D_ATTN), F32), jax.ShapeDtypeStruct((dil * ls, LANES), F32)],
        scratch=[pltpu.VMEM((2, N_SLOTS, BLOCK, 2 * BLOCK), F32),
                 pltpu.VMEM((per_step, N_SLOTS, BLOCK, 2 * BLOCK), F32),
                 pltpu.VMEM((per_step, N_SLOTS, BLOCK, 2 * BLOCK), BF16)],
        finish=lambda o, lse: (o.reshape(dil, ls, D_ATTN), lse.reshape(dil, ls, LANES)),
    )


def _attn_sample_kernel(bias_ref, nbias_ref, q0_ref, q1_ref, q2_ref, kv0_ref, kv1_ref, kv2_ref,
                        c0_ref, c1_ref, c2_ref, o_ref, slab_ref, s_ref, p_ref, og_ref, *, tb):
    rows = pl.ds(pl.program_id(0) * tb, tb)
    nt = (((1,), (1,)), ((), ()))
    kv_rows = 2 * N_SLOTS
    slab_rows = N_BACK * kv_rows
    q_refs = (q0_ref, q1_ref, q2_ref)
    kv_refs = (kv0_ref, kv1_ref, kv2_ref)
    c_refs = (c0_ref, c1_ref, c2_ref)

    def rounded(x):
        return x.astype(BF16).astype(F32)

    qs = [q_ref[rows] for q_ref in q_refs]
    for g in range(N_GROUPS):
        for t in range(tb):
            slab_ref[g, t] = c_refs[g][t].reshape(slab_rows, HEAD_DIM).astype(BF16)
            s_ref[g, t] = lax.dot_general(qs[g][t].astype(BF16), slab_ref[g, t], nt, preferred_element_type=F32)
    p_news, ls, lses = [], [], []
    for g in range(N_GROUPS):
        s_n = jnp.sum(rounded(qs[g]) * rounded(kv_refs[g][rows, 0]), axis=-1, keepdims=True)
        sc_c = s_ref[g] * SCALE + bias_ref[g][None]
        sc_n = s_n * SCALE + nbias_ref[g][None]
        mx = jnp.maximum(jnp.max(sc_c, axis=-1, keepdims=True), sc_n)
        p_c = jnp.exp(sc_c - mx)
        p_n = jnp.exp(sc_n - mx)
        l = jnp.sum(p_c, axis=-1, keepdims=True) + p_n
        p_ref[g] = pltpu.roll(p_c, N_SLOTS, 2).astype(BF16)
        p_news.append(p_n)
        ls.append(l)
        lses.append(mx + jnp.log(l))
    for g in range(N_GROUPS):
        for t in range(tb):
            og_ref[g, t] = jnp.dot(p_ref[g, t], slab_ref[g, t], preferred_element_type=F32)
    outs = [(og_ref[g] + rounded(p_news[g]) * rounded(kv_refs[g][rows, 1])) / ls[g] for g in range(N_GROUPS)]
    top = jnp.maximum(jnp.maximum(lses[0], lses[1]), lses[2])
    es = [jnp.exp(x - top) for x in lses]
    tot = es[0] + es[1] + es[2]
    o_ref[rows] = (es[0] / tot) * outs[0] + (es[1] / tot) * outs[1] + (es[2] / tot) * outs[2]


def _attn_sample(qs, kvs, caches, bias, nbias):
    db = qs[0].shape[0]
    tb = 4
    kv_rows = 2 * N_SLOTS
    views, specs = [], []
    for g, c in enumerate(caches):
        dil = DILATIONS[g]
        lc = c.shape[1]
        views.append(c.reshape(db, lc // dil, dil * kv_rows, HEAD_DIM))
        specs.append(pl.BlockSpec((tb, N_BACK, kv_rows, HEAD_DIM), lambda s: (s, 0, 0, 0)))
    q_spec = pl.BlockSpec((db, N_SLOTS, HEAD_DIM), lambda s: (0, 0, 0))
    kv_spec = pl.BlockSpec((db, 2, N_SLOTS, HEAD_DIM), lambda s: (0, 0, 0, 0))
    return pl.pallas_call(
        functools.partial(_attn_sample_kernel, tb=tb),
        grid=(db // tb,),
        in_specs=[pl.BlockSpec((N_GROUPS, N_SLOTS, N_BACK * kv_rows), lambda s: (0, 0, 0)),
                  pl.BlockSpec((N_GROUPS, N_SLOTS, 1), lambda s: (0, 0, 0)),
                  q_spec, q_spec, q_spec, kv_spec, kv_spec, kv_spec] + specs,
        out_specs=pl.BlockSpec((db, N_SLOTS, HEAD_DIM), lambda s: (0, 0, 0)),
        out_shape=jax.ShapeDtypeStruct((db, N_SLOTS, HEAD_DIM), F32),
        scratch_shapes=[pltpu.VMEM((N_GROUPS, tb, N_BACK * kv_rows, HEAD_DIM), BF16),
                        pltpu.VMEM((N_GROUPS, tb, N_SLOTS, N_BACK * kv_rows), F32),
                        pltpu.VMEM((N_GROUPS, tb, N_SLOTS, N_BACK * kv_rows), BF16),
                        pltpu.VMEM((N_GROUPS, tb, N_SLOTS, HEAD_DIM), F32)],
        compiler_params=_params(("arbitrary",)),
        name="attn_sample",
    )(bias, nbias, *qs, *kvs, *views)


def _out_kernel(*refs, tm, n_groups):
    o_refs = refs[:n_groups]
    lse_refs = refs[n_groups:2 * n_groups] if n_groups > 1 else ()
    rest = refs[len(o_refs) + len(lse_refs):]
    if n_groups > 1:
        sga_ref, sma_ref, smb_ref, yb_ref, x_ref, woa_ref, wob_ref, wo_ref, y_ref, a_ref, og_ref, lg_ref = rest
        for g in range(n_groups):
            dil = DILATIONS[g]
            sub = tm // dil
            for r in range(dil):
                rows = slice(None) if dil == 1 else pl.ds(r, sub, stride=dil)
                lg_ref[g, rows, :] = lse_refs[g][r]
                for h in range(N_SLOTS):
                    og_ref[g, h, rows, :] = o_refs[g][r, :, h * HEAD_DIM:(h + 1) * HEAD_DIM]
        lses = [lg_ref[g] for g in range(n_groups)]
        top = functools.reduce(jnp.maximum, lses)
        es = [jnp.exp(x - top) for x in lses]
        tot = functools.reduce(lambda a, b: a + b, es)
        ws = [e / tot for e in es]
        for h in range(N_SLOTS):
            sl = slice(h * HEAD_DIM, (h + 1) * HEAD_DIM)
            o = functools.reduce(lambda a, b: a + b,
                                 [ws[g][:, h:h + 1] * og_ref[g, h] for g in range(n_groups)])
            a_ref[:, sl] = (o * sga_ref[:, sl]).astype(BF16)
    else:
        sga_ref, sma_ref, smb_ref, yb_ref, x_ref, woa_ref, wob_ref, wo_ref, y_ref, a_ref = rest
        a_ref[...] = (o_refs[0][...] * sga_ref[...]).astype(BF16)
    ya = jnp.dot(a_ref[...], woa_ref[...], preferred_element_type=F32)
    yb = jnp.dot(yb_ref[...], wob_ref[...], preferred_element_type=F32)
    merged = sma_ref[...] * ya + smb_ref[...] * yb
    y_ref[...] = x_ref[...] + jnp.dot(merged.astype(BF16), wo_ref[...], preferred_element_type=F32)


def _out_proj(os_, lses, gates, yb_in, x, w_out_a, w_out_b, w_o):
    rows = x.shape[0]

    def gate_operand(tile, width):
        arr, tile0 = next((a, t0) for a, t0 in gates if t0 <= tile < t0 + a.shape[1] // GATE_TILE)
        assert (tile - tile0) * GATE_TILE % width == 0
        return arr, pl.BlockSpec((tm, width), lambda m: (m, (tile - tile0) * GATE_TILE // width))

    tm = min(rows, 256)
    n_groups = len(os_)

    def row_spec(width, col_block=0):
        return pl.BlockSpec((tm, width), lambda m: (m, col_block))

    def sub_spec(dil, width):
        return pl.BlockSpec((dil, tm // dil, width), lambda m: (0, m, 0))

    def const_spec(shape):
        return pl.BlockSpec(shape, lambda m: (0, 0), pipeline_mode=pl.Buffered(1))

    scratch = [pltpu.VMEM((tm, D_ATTN), BF16)]
    if n_groups > 1:
        in_specs = ([sub_spec(DILATIONS[g], D_ATTN) for g in range(n_groups)]
                    + [sub_spec(DILATIONS[g], LANES) for g in range(n_groups)])
        scratch += [pltpu.VMEM((n_groups, N_SLOTS, tm, HEAD_DIM), F32), pltpu.VMEM((n_groups, tm, LANES), F32)]
    else:
        in_specs = [row_spec(D_ATTN)]
    sga, sga_spec = gate_operand(SILU_TILE, D_ATTN)
    sma, sma_spec = gate_operand(0, D_MODEL)
    smb, smb_spec = gate_operand(D_MODEL // GATE_TILE, D_MODEL)
    in_specs += [sga_spec, sma_spec, smb_spec, row_spec(D_CONV), row_spec(D_MODEL),
                 const_spec((D_ATTN, D_MODEL)), const_spec((D_CONV, D_MODEL)), const_spec((D_MODEL, D_MODEL))]
    return pl.pallas_call(
        functools.partial(_out_kernel, tm=tm, n_groups=n_groups),
        grid=(rows // tm,),
        in_specs=in_specs,
        out_specs=row_spec(D_MODEL),
        out_shape=jax.ShapeDtypeStruct((rows, D_MODEL), F32),
        scratch_shapes=scratch,
        compiler_params=_params(("arbitrary",)),
        name="out_proj",
    )(*os_, *lses, sga, sma, smb, yb_in, x, w_out_a, w_out_b, w_o)


def kernel(x_prompt, x_sample, cache_kv_w128, cache_kv_w512, cache_kv_w2048, state_conv, norm_gain, w_in,
           q_norm_gain, k_norm_gain, rel_bias, conv_w, w_out_a, w_out_b, w_o):
    seq = x_prompt.shape[1]
    db = x_sample.shape[0]
    xp = x_prompt.reshape(seq, D_MODEL)
    xs = x_sample.reshape(db, D_MODEL)
    caches = (cache_kv_w128, cache_kv_w512, cache_kv_w2048)
    qk_gains = jnp.stack([jnp.tile(q_norm_gain, N_SLOTS), jnp.tile(k_norm_gain, N_SLOTS)]).reshape(2, 1, D_ATTN)
    cvec = _bias_by_offset(rel_bias)

    xn = _pre_norm(xp, norm_gain)
    qkv0, kv_p0, w_in = _qkv_proj(xn, w_in, qk_gains, 0, DILATIONS[0], BF16, min(WINDOWS[0], seq), tm=512,
                                  casts=(w_in,))
    qkv1, kv_p1, woa, wob, wo = _qkv_proj(xn, w_in, qk_gains, 1, DILATIONS[1], BF16, min(WINDOWS[1], seq),
                                          casts=(w_out_a, w_out_b, w_o))
    qkv2, kv_p2 = _qkv_proj(xn, w_in, qk_gains, 2, DILATIONS[2], BF16, min(WINDOWS[2], seq))
    qkvs, kv_p = (qkv0, qkv1, qkv2), (kv_p0, kv_p1, kv_p2)
    gates_a, attn0 = _gates(xn, w_in, 0, 2, attn=(qkvs[0], cvec, 0))
    gates_b, attn1 = _gates(xn, w_in, 2, 2, attn=(qkvs[1], cvec, 1))
    gates_c = _gates(xn, w_in, SILU_TILE, 1)
    yb_in, h_last, attn2 = _branch_b(xn, w_in, conv_w, attn=(qkvs[2], cvec, 2))
    gates = [(gates_a, 0), (gates_b, 2), (gates_c, SILU_TILE)]
    os_, lses = zip(attn0, attn1, attn2)
    y_prompt = _out_proj(os_, lses, gates, yb_in, xp, woa, wob, wo).reshape(1, seq, D_MODEL)
    conv_p = h_last[SUBLANES - (CONV_W - 1):][None]

    xn_s = _pre_norm(xs, norm_gain)
    qkvs_s, kv_s = zip(*[_qkv_proj(xn_s, w_in, qk_gains, g, 1, F32, db) for g in range(N_GROUPS)])
    yb_in_s, h_s = _branch_b(xn_s, w_in, conv_w, state=state_conv)
    gates_s = [(_gates(xn_s, w_in, 0, N_GATE_TILES), 0)]
    qs_s = [qkv[0, 0].reshape(db, N_SLOTS, HEAD_DIM) for qkv in qkvs_s]
    o_s = _attn_sample(qs_s, kv_s, caches, *_sample_bias(cvec)).reshape(db, D_ATTN)
    y_sample = _out_proj((o_s,), (), gates_s, yb_in_s, xs, woa, wob, wo).reshape(db, 1, D_MODEL)
    conv_s = jnp.stack([state_conv[:, CONV_W - 2], h_s], axis=1)

    return (y_prompt, y_sample, kv_p[0][None], kv_p[1][None], kv_p[2][None], conv_p,
            kv_s[0][:, None], kv_s[1][:, None], kv_s[2][:, None], conv_s)
```

```python
import functools
import math
from typing import Callable, NamedTuple

import numpy as np
import jax
import jax.numpy as jnp
from jax import lax
from jax.experimental import pallas as pl
from jax.experimental.pallas import tpu as pltpu

D_MODEL = 2048
N_GROUPS = 3
DILATIONS = (1, 4, 16)
N_BACK = 128
WINDOWS = (128, 512, 2048)
N_SLOTS = 8
HEAD_DIM = 128
D_ATTN = N_SLOTS * HEAD_DIM
QKV_COLS = N_GROUPS * D_ATTN
D_CONV = D_MODEL // 2
CONV_W = 3
BLOCK = N_BACK
NUM_BUCKETS = 32
MAX_EXACT = NUM_BUCKETS // 2
MAX_DISTANCE = 2048
EPS = 1e-6
SCALE = HEAD_DIM ** -0.5

COL_K = QKV_COLS
COL_V = 2 * QKV_COLS
COL_GATE_A = 3 * QKV_COLS
COL_B = COL_GATE_A + D_ATTN
COL_C = COL_B + D_CONV
COL_XIN = COL_C + D_CONV
COL_GATE_B = COL_XIN + D_CONV
COL_MERGE_A = COL_GATE_B + D_CONV
COL_MERGE_B = COL_MERGE_A + D_MODEL

SUBLANES = 8
LANES = 128
MXU_COLS = 256
VMEM_LIMIT_BYTES = 56 * 1024 * 1024

BF16 = jnp.bfloat16
F32 = jnp.float32
NEG_INF = float("-inf")


def _params(semantics):
    return pltpu.CompilerParams(dimension_semantics=semantics, vmem_limit_bytes=VMEM_LIMIT_BYTES)


def _norm_kernel(x_ref, g_ref, o_ref):
    x = x_ref[...]
    r = lax.rsqrt(jnp.mean(x * x, axis=-1, keepdims=True) + EPS)
    o_ref[...] = ((x * r) * g_ref[...]).astype(o_ref.dtype)


def _pre_norm(x, gain):
    rows = x.shape[0]
    tm = min(rows, 512)
    return pl.pallas_call(
        _norm_kernel,
        grid=(rows // tm,),
        in_specs=[pl.BlockSpec((tm, D_MODEL), lambda m: (m, 0)),
                  pl.BlockSpec((1, D_MODEL), lambda m: (0, 0))],
        out_specs=pl.BlockSpec((tm, D_MODEL), lambda m: (m, 0)),
        out_shape=jax.ShapeDtypeStruct((rows, D_MODEL), BF16),
        compiler_params=_params(("arbitrary",)),
        name="pre_norm",
    )(x, gain.reshape(1, D_MODEL))


def _residue_stride(dil, sub):
    return sub + SUBLANES if dil > 4 else sub


def _qkv_kernel(xn_ref, w_ref, gain_ref, *refs, dil, tm, tr, m_tail0, n_casts, w_is_f32):
    cast_in, (o_ref, tail_ref), cast_out, (res_ref,), scratch = (
        refs[:n_casts], refs[n_casts:n_casts + 2], refs[n_casts + 2:2 * n_casts + 2],
        refs[2 * n_casts + 2:2 * n_casts + 3], refs[2 * n_casts + 3:])
    kind = pl.program_id(0)
    m = pl.program_id(1)
    if w_is_f32:
        w_f32_ref, (w_ref,) = w_ref, scratch

        @pl.when(m == 0)
        def _():
            w_ref[...] = w_f32_ref[...].astype(BF16)

    normed = kind < 2
    xn = xn_ref[...]
    sub = tm // dil
    pitch = _residue_stride(dil, sub)
    scatter = dil > 4
    heads_per_chunk = MXU_COLS // HEAD_DIM
    n_chunks = D_ATTN // MXU_COLS
    for j in range(n_chunks):
        p = jnp.dot(xn, w_ref[:, j * MXU_COLS:(j + 1) * MXU_COLS], preferred_element_type=F32)
        for src_ref, dst_ref in zip(cast_in, cast_out):
            part = src_ref.shape[0] // n_chunks
            dst_ref[j * part:(j + 1) * part, :] = src_ref[j * part:(j + 1) * part, :].astype(BF16)
        for hh in range(heads_per_chunk):
            h = j * heads_per_chunk + hh
            sl = slice(h * HEAD_DIM, (h + 1) * HEAD_DIM)
            ph = p[:, hh * HEAD_DIM:(hh + 1) * HEAD_DIM]
            r = lax.rsqrt(jnp.mean(ph * ph, axis=-1, keepdims=True) + EPS)
            res = (ph * jnp.where(normed, r, 1.0)) * jnp.where(normed, gain_ref[:, sl], 1.0)
            if scatter:
                for i in range(sub):
                    res_ref[h, pl.ds(i, dil, stride=pitch), :] = res[i * dil:(i + 1) * dil]
            else:
                res_ref[h] = res
            for rr in range(dil):
                if dil == 1:
                    rows = res
                elif scatter:
                    rows = res_ref[h, rr * pitch:rr * pitch + sub, :]
                else:
                    rows = res_ref[h, pl.ds(rr, sub, stride=dil), :]
                o_ref[rr, :, sl] = rows.astype(o_ref.dtype)

    @pl.when(jnp.logical_and(kind >= 1, m >= m_tail0))
    def _():
        for h in range(N_SLOTS):
            if scatter:
                i0 = (tm - tr) // dil
                for rr in range(dil):
                    tail_ref[pl.ds(rr, sub - i0, stride=dil), h, :] = res_ref[h, rr * pitch + i0:rr * pitch + sub, :]
            else:
                tail_ref[:, h, :] = res_ref[h, tm - tr:tm, :]


def _qkv_proj(xn, w_in, qk_gains, g, dil, out_dtype, tail_rows, tm=1024, casts=()):
    rows = xn.shape[0]
    tm = min(rows, tm)
    tn = D_ATTN
    tr = min(tail_rows, tm)
    m_tail0 = rows // tm - tail_rows // tr
    grid = (3, rows // tm)
    n_steps = grid[0] * grid[1]
    w_is_f32 = w_in.dtype == F32

    def tail_map(kind, m):
        return (jnp.where(kind >= 1, jnp.maximum(m - m_tail0, 0), 0), jnp.maximum(kind - 1, 0), 0, 0)

    def cast_spec(w):
        n_blocks = max(d for d in range(1, n_steps + 1) if w.shape[1] % (d * LANES) == 0)
        return pl.BlockSpec((w.shape[0], w.shape[1] // n_blocks),
                            lambda kind, m: (0, jnp.minimum(kind * grid[1] + m, n_blocks - 1)))

    cast_specs = [cast_spec(w) for w in casts]
    scratch = [pltpu.VMEM((N_SLOTS, dil * _residue_stride(dil, tm // dil), HEAD_DIM), F32)]
    if w_is_f32:
        scratch.append(pltpu.VMEM((D_MODEL, tn), BF16))
    return pl.pallas_call(
        functools.partial(_qkv_kernel, dil=dil, tm=tm, tr=tr, m_tail0=m_tail0, n_casts=len(casts), w_is_f32=w_is_f32),
        grid=grid,
        in_specs=[pl.BlockSpec((tm, D_MODEL), lambda kind, m: (m, 0)),
                  pl.BlockSpec((D_MODEL, tn), lambda kind, m: (0, kind * N_GROUPS + g)),
                  pl.BlockSpec((None, 1, tn), lambda kind, m: (jnp.minimum(kind, 1), 0, 0))] + cast_specs,
        out_specs=[pl.BlockSpec((None, dil, tm // dil, tn), lambda kind, m: (kind, 0, m, 0)),
                   pl.BlockSpec((tr, None, N_SLOTS, HEAD_DIM), tail_map)] + cast_specs,
        out_shape=[jax.ShapeDtypeStruct((3, dil, rows // dil, tn), out_dtype),
                   jax.ShapeDtypeStruct((tail_rows, 2, N_SLOTS, HEAD_DIM), F32)]
                  + [jax.ShapeDtypeStruct(w.shape, BF16) for w in casts],
        scratch_shapes=scratch,
        compiler_params=_params(("arbitrary", "arbitrary")),
        name=f"qkv_proj_g{g}",
    )(xn, w_in, qk_gains, *casts)


def _branch_b_kernel(*refs, tm, per_row_state, side):
    if per_row_state:
        xn_ref, wb_ref, wc_ref, wx_ref, wg_ref, cw_ref, hm2_ref, hm1_ref, y_ref, h_ref = refs
    else:
        (xn_ref, wb_ref, wc_ref, wx_ref, wg_ref, cw_ref), refs = refs[:6], refs[6:]
        if side is not None:
            side_in, (y_ref, h_ref), side_out, (hbuf_ref,), side_scratch = (
                refs[:6], refs[6:8], refs[8:10], refs[10:11], refs[11:])
        else:
            y_ref, h_ref, hbuf_ref = refs

        @pl.when(pl.program_id(1) == 0)
        def _():
            hbuf_ref[0:SUBLANES, :] = jnp.zeros((SUBLANES, hbuf_ref.shape[1]), F32)

        if side is not None:
            _attn_side(side, pl.program_id(0) * pl.num_programs(1) + pl.program_id(1),
                       *side_in, *side_out, *side_scratch)

    xn = xn_ref[...]
    pb = jnp.dot(xn, wb_ref[...], preferred_element_type=F32)
    pc = jnp.dot(xn, wc_ref[...], preferred_element_type=F32)
    px = jnp.dot(xn, wx_ref[...], preferred_element_type=F32)
    pg = jnp.dot(xn, wg_ref[...], preferred_element_type=F32)
    h = pc * px
    if per_row_state:
        hm2 = hm2_ref[...]
        hm1 = hm1_ref[...]
        h_ref[...] = h
    else:
        hbuf_ref[SUBLANES:SUBLANES + tm, :] = h
        hm2 = hbuf_ref[pl.ds(SUBLANES - 2, tm), :]
        hm1 = hbuf_ref[pl.ds(SUBLANES - 1, tm), :]
        last = hbuf_ref[pl.ds(tm, SUBLANES), :]
        h_ref[...] = last
        hbuf_ref[0:SUBLANES, :] = last
    z = cw_ref[0:1, :] * hm2 + cw_ref[1:2, :] * hm1 + cw_ref[2:3, :] * h
    y_ref[...] = (pb * z * (pg * jax.nn.sigmoid(pg))).astype(y_ref.dtype)


def _branch_b(xn, w_in, conv_w, state=None, attn=None):
    rows = xn.shape[0]
    tm = min(rows, 1024)
    tnb = 256
    per_row_state = state is not None
    grid = (D_CONV // tnb, rows // tm)
    side = None if attn is None else _attn_side_plan(*attn, n_steps=grid[0] * grid[1],
                                                     step_of=lambda j, m: j * grid[1] + m)

    def wspec(col0):
        return pl.BlockSpec((D_MODEL, tnb), lambda j, m: (0, col0 // tnb + j))

    in_specs = [pl.BlockSpec((tm, D_MODEL), lambda j, m: (m, 0)),
                wspec(COL_B), wspec(COL_C), wspec(COL_XIN), wspec(COL_GATE_B),
                pl.BlockSpec((CONV_W, tnb), lambda j, m: (0, j))]
    args = [xn, w_in, w_in, w_in, w_in, conv_w]
    scratch = []
    if per_row_state:
        st = state.reshape(rows, (CONV_W - 1) * D_CONV)
        in_specs += [pl.BlockSpec((tm, tnb), lambda j, m: (m, j)),
                     pl.BlockSpec((tm, tnb), lambda j, m: (m, D_CONV // tnb + j))]
        args += [st, st]
        h_spec = pl.BlockSpec((tm, tnb), lambda j, m: (m, j))
        h_shape = jax.ShapeDtypeStruct((rows, D_CONV), F32)
    else:
        scratch.append(pltpu.VMEM((tm + SUBLANES, tnb), F32))
        h_spec = pl.BlockSpec((SUBLANES, tnb), lambda j, m: (0, j))
        h_shape = jax.ShapeDtypeStruct((SUBLANES, D_CONV), F32)
    out_specs = [pl.BlockSpec((tm, tnb), lambda j, m: (m, j)), h_spec]
    out_shape = [jax.ShapeDtypeStruct((rows, D_CONV), BF16), h_shape]
    if side is not None:
        in_specs += side.in_specs
        args += side.args
        out_specs += side.out_specs
        out_shape += side.out_shape
        scratch += side.scratch
    outs = pl.pallas_call(
        functools.partial(_branch_b_kernel, tm=tm, per_row_state=per_row_state,
                          side=None if side is None else side.static),
        grid=grid,
        in_specs=in_specs,
        out_specs=out_specs,
        out_shape=out_shape,
        scratch_shapes=scratch,
        compiler_params=_params(("arbitrary", "arbitrary")),
        name="branch_b",
    )(*args)
    return outs if side is None else (outs[0], outs[1], side.finish(outs[2], outs[3]))


GATE_TILE = 1024
N_GATE_TILES = (2 * D_MODEL + D_ATTN) // GATE_TILE
SILU_TILE = N_GATE_TILES - 1


def _gates_kernel(xn_ref, w_ref, *refs, tile0, side):
    if side is not None:
        side_in, o_ref, side_out, side_scratch = refs[:6], refs[6], refs[7:9], refs[9:]
        _attn_side(side, pl.program_id(0) * pl.num_programs(1) + pl.program_id(1),
                   *side_in, *side_out, *side_scratch)
    else:
        o_ref, = refs
    is_silu = tile0 + pl.program_id(0) == SILU_TILE
    xn = xn_ref[...]
    for j in range(w_ref.shape[1] // MXU_COLS):
        cs = slice(j * MXU_COLS, (j + 1) * MXU_COLS)
        p = jnp.dot(xn, w_ref[:, cs], preferred_element_type=F32)
        o_ref[:, cs] = jax.nn.sigmoid(p) * jnp.where(is_silu, p, 1.0)


def _gates(xn, w_in, tile0, n_tiles, attn=None):
    rows = xn.shape[0]
    tm = min(rows, 1024)
    tn = GATE_TILE
    grid = (n_tiles, rows // tm)
    side = None if attn is None else _attn_side_plan(*attn, n_steps=grid[0] * grid[1],
                                                     step_of=lambda n, m: n * grid[1] + m)

    def wmap(n, m):
        return (0, jnp.where(tile0 + n == SILU_TILE, COL_GATE_A // tn, COL_MERGE_A // tn + tile0 + n))

    in_specs = [pl.BlockSpec((tm, D_MODEL), lambda n, m: (m, 0)), pl.BlockSpec((D_MODEL, tn), wmap)]
    args = [xn, w_in]
    out_specs = [pl.BlockSpec((tm, tn), lambda n, m: (m, n))]
    out_shape = [jax.ShapeDtypeStruct((rows, n_tiles * tn), F32)]
    scratch = []
    if side is not None:
        in_specs += side.in_specs
        args += side.args
        out_specs += side.out_specs
        out_shape += side.out_shape
        scratch += side.scratch
    outs = pl.pallas_call(
        functools.partial(_gates_kernel, tile0=tile0, side=None if side is None else side.static),
        grid=grid,
        in_specs=in_specs,
        out_specs=out_specs,
        out_shape=out_shape,
        scratch_shapes=scratch,
        compiler_params=_params(("arbitrary", "arbitrary")),
        name="gates",
    )(*args)
    return outs[0] if side is None else (outs[0], side.finish(outs[1], outs[2]))


def _t5_bucket(dist):
    d = np.maximum(dist, 1).astype(np.float32)
    large = MAX_EXACT + (np.log(d / np.float32(MAX_EXACT)) / np.float32(math.log(MAX_DISTANCE / MAX_EXACT))
                         * np.float32(NUM_BUCKETS - MAX_EXACT)).astype(np.int32)
    large = np.minimum(large, NUM_BUCKETS - 1)
    return np.where(dist < MAX_EXACT, dist, large)


def _bias_by_offset(rel_bias):
    steps = N_BACK - np.arange(N_BACK + 1)
    idx = np.stack([_t5_bucket(d * steps) for d in DILATIONS])
    g_idx = np.arange(N_GROUPS)[:, None]
    vals = jnp.transpose(rel_bias[idx, g_idx], (0, 2, 1)).astype(F32)
    return jnp.pad(vals, ((0, 0), (0, 0), (0, 2 * BLOCK - N_BACK - 1)), constant_values=NEG_INF)


def _sample_bias(cvec):
    kv_head = np.arange(N_BACK * 2 * N_SLOTS) % (2 * N_SLOTS)
    own_k_row = kv_head[None, :] == np.arange(N_SLOTS)[:, None]
    per_row = jnp.repeat(cvec[:, :, :N_BACK], 2 * N_SLOTS, axis=2)
    return jnp.where(own_k_row[None], per_row, NEG_INF), cvec[:, :, N_BACK:N_BACK + 1]


class _SideStatic(NamedTuple):
    per_step: int
    nb: int
    steps: int


class _SidePlan(NamedTuple):
    static: _SideStatic
    in_specs: list
    args: list
    out_specs: list
    out_shape: list
    scratch: list
    finish: Callable


def _attn_side(side, step, cvec_ref, q_ref, kp_ref, kc_ref, vp_ref, vc_ref, o_ref, lse_ref, bias_ref, s_ref, p_ref):
    @pl.when(step == 0)
    def _():
        for h in range(N_SLOTS):
            row = jnp.broadcast_to(cvec_ref[h:h + 1, :], (BLOCK, 2 * BLOCK))
            full = pltpu.roll(row, 0, 1, stride=1, stride_axis=0)
            bias_ref[1, h] = full
            bias_ref[0, h, :, :BLOCK] = jnp.full((BLOCK, BLOCK), NEG_INF, F32)
            bias_ref[0, h, :, BLOCK:] = full[:, BLOCK:]

    nt = (((1,), (1,)), ((), ()))
    heads = [slice(h * HEAD_DIM, (h + 1) * HEAD_DIM) for h in range(N_SLOTS)]
    unit0 = jnp.minimum(step, side.steps - 1) * side.per_step
    lane = lax.broadcasted_iota(jnp.int32, (BLOCK, LANES), 1)
    for u in range(side.per_step):
        rows = slice(u * BLOCK, (u + 1) * BLOCK)
        prev = slice((u - 1) * BLOCK, u * BLOCK)
        for h, sl in enumerate(heads):
            q = q_ref[rows, sl]
            k_prev = kp_ref[:, sl] if u == 0 else kc_ref[prev, sl]
            s_ref[u, h, :, :BLOCK] = lax.dot_general(q, k_prev, nt, preferred_element_type=F32)
            s_ref[u, h, :, BLOCK:] = lax.dot_general(q, kc_ref[rows, sl], nt, preferred_element_type=F32)
    ls, lses = [], []
    for u in range(side.per_step):
        has_prev = jnp.minimum((unit0 + u) % side.nb, 1)
        sc = s_ref[u] * SCALE + bias_ref[has_prev]
        mx = jnp.max(jnp.maximum(sc[:, :, :BLOCK], sc[:, :, BLOCK:]), axis=-1, keepdims=True)
        p = jnp.exp(sc - mx)
        l = jnp.sum(p[:, :, :BLOCK] + p[:, :, BLOCK:], axis=-1, keepdims=True)
        p_ref[u] = p.astype(BF16)
        ls.append(l)
        lses.append(mx + jnp.log(l))
    for u in range(side.per_step):
        rows = slice(u * BLOCK, (u + 1) * BLOCK)
        prev = slice((u - 1) * BLOCK, u * BLOCK)
        lse_all = jnp.zeros((BLOCK, LANES), F32)
        for h, sl in enumerate(heads):
            v_prev = vp_ref[:, sl] if u == 0 else vc_ref[prev, sl]
            o = (jnp.dot(p_ref[u, h, :, :BLOCK], v_prev, preferred_element_type=F32)
                 + jnp.dot(p_ref[u, h, :, BLOCK:], vc_ref[rows, sl], preferred_element_type=F32))
            o_ref[rows, sl] = o / ls[u][h]
            lse_all = jnp.where(lane == h, lses[u][h], lse_all)
        lse_ref[rows, :] = lse_all


def _attn_side_plan(qkv, cvec, g, n_steps, step_of):
    _, dil, ls, _ = qkv.shape
    nb = ls // BLOCK
    units = dil * nb
    per_step = min(d for d in range(1, units + 1) if units % d == 0 and d * n_steps >= units)
    static = _SideStatic(per_step=per_step, nb=nb, steps=units // per_step)
    flat = qkv.reshape(3, dil * ls, D_ATTN)
    rows = per_step * BLOCK

    def cur(kind):
        return pl.BlockSpec((None, rows, D_ATTN), lambda *idx: (kind, jnp.minimum(step_of(*idx), static.steps - 1), 0))

    def prev(kind):
        def index(*idx):
            first_unit = jnp.minimum(step_of(*idx), static.steps - 1) * per_step
            return (kind, jnp.maximum(first_unit - 1, 0), 0)
        return pl.BlockSpec((None, BLOCK, D_ATTN), index)

    def out_spec(width):
        return pl.BlockSpec((rows, width), lambda *idx: (jnp.minimum(step_of(*idx), static.steps - 1), 0))

    return _SidePlan(
        static=static,
        in_specs=[pl.BlockSpec((None, N_SLOTS, 2 * BLOCK), lambda *idx: (g, 0, 0)),
                  cur(0), prev(1), cur(1), prev(2), cur(2)],
        args=[cvec, flat, flat, flat, flat, flat],
        out_specs=[out_spec(D_ATTN), out_spec(LANES)],
        out_shape=[jax.ShapeDtypeStruct((dil * ls, D_ATTN), F32), jax.ShapeDtypeStruct((dil * ls, LANES), F32)],
        scratch=[pltpu.VMEM((2, N_SLOTS, BLOCK, 2 * BLOCK), F32),
                 pltpu.VMEM((per_step, N_SLOTS, BLOCK, 2 * BLOCK), F32),
                 pltpu.VMEM((per_step, N_SLOTS, BLOCK, 2 * BLOCK), BF16)],
        finish=lambda o, lse: (o.reshape(dil, ls, D_ATTN), lse.reshape(dil, ls, LANES)),
    )


def _attn_sample_kernel(bias_ref, nbias_ref, q0_ref, q1_ref, q2_ref, kv0_ref, kv1_ref, kv2_ref,
                        c0_ref, c1_ref, c2_ref, o_ref, slab_ref, s_ref, p_ref, og_ref, *, tb):
    rows = pl.ds(pl.program_id(0) * tb, tb)
    nt = (((1,), (1,)), ((), ()))
    kv_rows = 2 * N_SLOTS
    slab_rows = N_BACK * kv_rows
    q_refs = (q0_ref, q1_ref, q2_ref)
    kv_refs = (kv0_ref, kv1_ref, kv2_ref)
    c_refs = (c0_ref, c1_ref, c2_ref)

    def rounded(x):
        return x.astype(BF16).astype(F32)

    qs = [q_ref[rows] for q_ref in q_refs]
    for g in range(N_GROUPS):
        for t in range(tb):
            slab_ref[g, t] = c_refs[g][t].reshape(slab_rows, HEAD_DIM).astype(BF16)
            s_ref[g, t] = lax.dot_general(qs[g][t].astype(BF16), slab_ref[g, t], nt, preferred_element_type=F32)
    p_news, ls, lses = [], [], []
    for g in range(N_GROUPS):
        s_n = jnp.sum(rounded(qs[g]) * rounded(kv_refs[g][rows, 0]), axis=-1, keepdims=True)
        sc_c = s_ref[g] * SCALE + bias_ref[g][None]
        sc_n = s_n * SCALE + nbias_ref[g][None]
        mx = jnp.maximum(jnp.max(sc_c, axis=-1, keepdims=True), sc_n)
        p_c = jnp.exp(sc_c - mx)
        p_n = jnp.exp(sc_n - mx)
        l = jnp.sum(p_c, axis=-1, keepdims=True) + p_n
        p_ref[g] = pltpu.roll(p_c, N_SLOTS, 2).astype(BF16)
        p_news.append(p_n)
        ls.append(l)
        lses.append(mx + jnp.log(l))
    for g in range(N_GROUPS):
        for t in range(tb):
            og_ref[g, t] = jnp.dot(p_ref[g, t], slab_ref[g, t], preferred_element_type=F32)
    outs = [(og_ref[g] + rounded(p_news[g]) * rounded(kv_refs[g][rows, 1])) / ls[g] for g in range(N_GROUPS)]
    top = jnp.maximum(jnp.maximum(lses[0], lses[1]), lses[2])
    es = [jnp.exp(x - top) for x in lses]
    tot = es[0] + es[1] + es[2]
    o_ref[rows] = (es[0] / tot) * outs[0] + (es[1] / tot) * outs[1] + (es[2] / tot) * outs[2]


def _attn_sample(qs, kvs, caches, bias, nbias):
    db = qs[0].shape[0]
    tb = 4
    kv_rows = 2 * N_SLOTS
    views, specs = [], []
    for g, c in enumerate(caches):
        dil = DILATIONS[g]
        lc = c.shape[1]
        views.append(c.reshape(db, lc // dil, dil * kv_rows, HEAD_DIM))
        specs.append(pl.BlockSpec((tb, N_BACK, kv_rows, HEAD_DIM), lambda s: (s, 0, 0, 0)))
    q_spec = pl.BlockSpec((db, N_SLOTS, HEAD_DIM), lambda s: (0, 0, 0))
    kv_spec = pl.BlockSpec((db, 2, N_SLOTS, HEAD_DIM), lambda s: (0, 0, 0, 0))
    return pl.pallas_call(
        functools.partial(_attn_sample_kernel, tb=tb),
        grid=(db // tb,),
        in_specs=[pl.BlockSpec((N_GROUPS, N_SLOTS, N_BACK * kv_rows), lambda s: (0, 0, 0)),
                  pl.BlockSpec((N_GROUPS, N_SLOTS, 1), lambda s: (0, 0, 0)),
                  q_spec, q_spec, q_spec, kv_spec, kv_spec, kv_spec] + specs,
        out_specs=pl.BlockSpec((db, N_SLOTS, HEAD_DIM), lambda s: (0, 0, 0)),
        out_shape=jax.ShapeDtypeStruct((db, N_SLOTS, HEAD_DIM), F32),
        scratch_shapes=[pltpu.VMEM((N_GROUPS, tb, N_BACK * kv_rows, HEAD_DIM), BF16),
                        pltpu.VMEM((N_GROUPS, tb, N_SLOTS, N_BACK * kv_rows), F32),
                        pltpu.VMEM((N_GROUPS, tb, N_SLOTS, N_BACK * kv_rows), BF16),
                        pltpu.VMEM((N_GROUPS, tb, N_SLOTS, HEAD_DIM), F32)],
        compiler_params=_params(("arbitrary",)),
        name="attn_sample",
    )(bias, nbias, *qs, *kvs, *views)


def _out_kernel(*refs, tm, n_groups):
    o_refs = refs[:n_groups]
    lse_refs = refs[n_groups:2 * n_groups] if n_groups > 1 else ()
    rest = refs[len(o_refs) + len(lse_refs):]
    def project(a):
        ya = jnp.dot(a, woa_ref[...], preferred_element_type=F32)
        yb = jnp.dot(yb_ref[...], wob_ref[...], preferred_element_type=F32)
        merged = sma_ref[...] * ya + smb_ref[...] * yb
        y_ref[...] = x_ref[...] + jnp.dot(merged.astype(BF16), wo_ref[...], preferred_element_type=F32)

    if n_groups > 1:
        sga_ref, sma_ref, smb_ref, yb_ref, x_ref, woa_ref, wob_ref, wo_ref, y_ref, a2_ref, og_ref, lg_ref = rest
        step = pl.program_id(0)

        @pl.when(step == 0)
        def _():
            a2_ref[1] = jnp.zeros(a2_ref.shape[1:], BF16)

        project(a2_ref[(step + 1) % 2])
        a_ref = a2_ref.at[step % 2]
        for g in range(n_groups):
            dil = DILATIONS[g]
            sub = tm // dil
            for r in range(dil):
                rows = slice(None) if dil == 1 else pl.ds(r, sub, stride=dil)
                lg_ref[g, rows, :] = lse_refs[g][r]
                for h in range(N_SLOTS):
                    og_ref[g, h, rows, :] = o_refs[g][r, :, h * HEAD_DIM:(h + 1) * HEAD_DIM]
        lses = [lg_ref[g] for g in range(n_groups)]
        top = functools.reduce(jnp.maximum, lses)
        es = [jnp.exp(x - top) for x in lses]
        tot = functools.reduce(lambda a, b: a + b, es)
        ws = [e / tot for e in es]
        for h in range(N_SLOTS):
            sl = slice(h * HEAD_DIM, (h + 1) * HEAD_DIM)
            o = functools.reduce(lambda a, b: a + b,
                                 [ws[g][:, h:h + 1] * og_ref[g, h] for g in range(n_groups)])
            a_ref[:, sl] = (o * sga_ref[:, sl]).astype(BF16)
    else:
        sga_ref, sma_ref, smb_ref, yb_ref, x_ref, woa_ref, wob_ref, wo_ref, y_ref = rest
        project((o_refs[0][...] * sga_ref[...]).astype(BF16))


def _out_proj(os_, lses, gates, yb_in, x, w_out_a, w_out_b, w_o):
    rows = x.shape[0]

    tm = min(rows, 256)
    n_groups = len(os_)
    n_blocks = rows // tm
    pipelined = n_groups > 1
    n_steps = n_blocks + 1 if pipelined else n_blocks

    def comb_block(s):
        return jnp.minimum(s, n_blocks - 1)

    def proj_block(s):
        return jnp.maximum(s - 1, 0) if pipelined else s

    def gate_operand(tile, width, block_of):
        arr, tile0 = next((a, t0) for a, t0 in gates if t0 <= tile < t0 + a.shape[1] // GATE_TILE)
        assert (tile - tile0) * GATE_TILE % width == 0
        return arr, pl.BlockSpec((tm, width), lambda s: (block_of(s), (tile - tile0) * GATE_TILE // width))

    def row_spec(width, block_of):
        return pl.BlockSpec((tm, width), lambda s: (block_of(s), 0))

    def sub_spec(dil, width):
        return pl.BlockSpec((dil, tm // dil, width), lambda s: (0, comb_block(s), 0))

    def const_spec(shape):
        return pl.BlockSpec(shape, lambda s: (0, 0), pipeline_mode=pl.Buffered(1))

    if pipelined:
        in_specs = ([sub_spec(DILATIONS[g], D_ATTN) for g in range(n_groups)]
                    + [sub_spec(DILATIONS[g], LANES) for g in range(n_groups)])
        scratch = [pltpu.VMEM((2, tm, D_ATTN), BF16),
                   pltpu.VMEM((n_groups, N_SLOTS, tm, HEAD_DIM), F32), pltpu.VMEM((n_groups, tm, LANES), F32)]
    else:
        in_specs = [row_spec(D_ATTN, comb_block)]
        scratch = []
    sga, sga_spec = gate_operand(SILU_TILE, D_ATTN, comb_block)
    sma, sma_spec = gate_operand(0, D_MODEL, proj_block)
    smb, smb_spec = gate_operand(D_MODEL // GATE_TILE, D_MODEL, proj_block)
    in_specs += [sga_spec, sma_spec, smb_spec, row_spec(D_CONV, proj_block), row_spec(D_MODEL, proj_block),
                 const_spec((D_ATTN, D_MODEL)), const_spec((D_CONV, D_MODEL)), const_spec((D_MODEL, D_MODEL))]
    return pl.pallas_call(
        functools.partial(_out_kernel, tm=tm, n_groups=n_groups),
        grid=(n_steps,),
        in_specs=in_specs,
        out_specs=row_spec(D_MODEL, proj_block),
        out_shape=jax.ShapeDtypeStruct((rows, D_MODEL), F32),
        scratch_shapes=scratch,
        compiler_params=_params(("arbitrary",)),
        name="out_proj",
    )(*os_, *lses, sga, sma, smb, yb_in, x, w_out_a, w_out_b, w_o)


def kernel(x_prompt, x_sample, cache_kv_w128, cache_kv_w512, cache_kv_w2048, state_conv, norm_gain, w_in,
           q_norm_gain, k_norm_gain, rel_bias, conv_w, w_out_a, w_out_b, w_o):
    seq = x_prompt.shape[1]
    db = x_sample.shape[0]
    xp = x_prompt.reshape(seq, D_MODEL)
    xs = x_sample.reshape(db, D_MODEL)
    caches = (cache_kv_w128, cache_kv_w512, cache_kv_w2048)
    qk_gains = jnp.stack([jnp.tile(q_norm_gain, N_SLOTS), jnp.tile(k_norm_gain, N_SLOTS)]).reshape(2, 1, D_ATTN)
    cvec = _bias_by_offset(rel_bias)

    xn = _pre_norm(xp, norm_gain)
    qkv0, kv_p0, w_in = _qkv_proj(xn, w_in, qk_gains, 0, DILATIONS[0], BF16, min(WINDOWS[0], seq), tm=512,
                                  casts=(w_in,))
    qkv1, kv_p1, woa, wob, wo = _qkv_proj(xn, w_in, qk_gains, 1, DILATIONS[1], BF16, min(WINDOWS[1], seq),
                                          casts=(w_out_a, w_out_b, w_o))
    qkv2, kv_p2 = _qkv_proj(xn, w_in, qk_gains, 2, DILATIONS[2], BF16, min(WINDOWS[2], seq))
    qkvs, kv_p = (qkv0, qkv1, qkv2), (kv_p0, kv_p1, kv_p2)
    gates_a, attn0 = _gates(xn, w_in, 0, 2, attn=(qkvs[0], cvec, 0))
    gates_b, attn1 = _gates(xn, w_in, 2, 2, attn=(qkvs[1], cvec, 1))
    gates_c = _gates(xn, w_in, SILU_TILE, 1)
    yb_in, h_last, attn2 = _branch_b(xn, w_in, conv_w, attn=(qkvs[2], cvec, 2))
    gates = [(gates_a, 0), (gates_b, 2), (gates_c, SILU_TILE)]
    os_, lses = zip(attn0, attn1, attn2)
    y_prompt = _out_proj(os_, lses, gates, yb_in, xp, woa, wob, wo).reshape(1, seq, D_MODEL)
    conv_p = h_last[SUBLANES - (CONV_W - 1):][None]

    xn_s = _pre_norm(xs, norm_gain)
    qkvs_s, kv_s = zip(*[_qkv_proj(xn_s, w_in, qk_gains, g, 1, F32, db) for g in range(N_GROUPS)])
    yb_in_s, h_s = _branch_b(xn_s, w_in, conv_w, state=state_conv)
    gates_s = [(_gates(xn_s, w_in, 0, N_GATE_TILES), 0)]
    qs_s = [qkv[0, 0].reshape(db, N_SLOTS, HEAD_DIM) for qkv in qkvs_s]
    o_s = _attn_sample(qs_s, kv_s, caches, *_sample_bias(cvec)).reshape(db, D_ATTN)
    y_sample = _out_proj((o_s,), (), gates_s, yb_in_s, xs, woa, wob, wo).reshape(db, 1, D_MODEL)
    conv_s = jnp.stack([state_conv[:, CONV_W - 2], h_s], axis=1)

    return (y_prompt, y_sample, kv_p[0][None], kv_p[1][None], kv_p[2][None], conv_p,
            kv_s[0][:, None], kv_s[1][:, None], kv_s[2][:, None], conv_s)
```

```python
import functools
import math
from typing import Callable, NamedTuple

import numpy as np
import jax
import jax.numpy as jnp
from jax import lax
from jax.experimental import pallas as pl
from jax.experimental.pallas import tpu as pltpu

D_MODEL = 2048
N_GROUPS = 3
DILATIONS = (1, 4, 16)
N_BACK = 128
WINDOWS = (128, 512, 2048)
N_SLOTS = 8
HEAD_DIM = 128
D_ATTN = N_SLOTS * HEAD_DIM
QKV_COLS = N_GROUPS * D_ATTN
D_CONV = D_MODEL // 2
CONV_W = 3
BLOCK = N_BACK
NUM_BUCKETS = 32
MAX_EXACT = NUM_BUCKETS // 2
MAX_DISTANCE = 2048
EPS = 1e-6
SCALE = HEAD_DIM ** -0.5

COL_K = QKV_COLS
COL_V = 2 * QKV_COLS
COL_GATE_A = 3 * QKV_COLS
COL_B = COL_GATE_A + D_ATTN
COL_C = COL_B + D_CONV
COL_XIN = COL_C + D_CONV
COL_GATE_B = COL_XIN + D_CONV
COL_MERGE_A = COL_GATE_B + D_CONV
COL_MERGE_B = COL_MERGE_A + D_MODEL

SUBLANES = 8
LANES = 128
MXU_COLS = 256
VMEM_LIMIT_BYTES = 56 * 1024 * 1024

BF16 = jnp.bfloat16
F32 = jnp.float32
NEG_INF = float("-inf")


def _params(semantics):
    return pltpu.CompilerParams(dimension_semantics=semantics, vmem_limit_bytes=VMEM_LIMIT_BYTES)


def _norm_kernel(x_ref, g_ref, o_ref):
    x = x_ref[...]
    r = lax.rsqrt(jnp.mean(x * x, axis=-1, keepdims=True) + EPS)
    o_ref[...] = ((x * r) * g_ref[...]).astype(o_ref.dtype)


def _pre_norm(x, gain):
    rows = x.shape[0]
    tm = min(rows, 512)
    return pl.pallas_call(
        _norm_kernel,
        grid=(rows // tm,),
        in_specs=[pl.BlockSpec((tm, D_MODEL), lambda m: (m, 0)),
                  pl.BlockSpec((1, D_MODEL), lambda m: (0, 0))],
        out_specs=pl.BlockSpec((tm, D_MODEL), lambda m: (m, 0)),
        out_shape=jax.ShapeDtypeStruct((rows, D_MODEL), BF16),
        compiler_params=_params(("arbitrary",)),
        name="pre_norm",
    )(x, gain.reshape(1, D_MODEL))


def _residue_stride(dil, sub):
    return sub + SUBLANES if dil > 4 else sub


def _qkv_kernel(xn_ref, w_ref, gain_ref, *refs, dil, tm, tr, m_tail0, n_casts, w_is_f32):
    cast_in, (o_ref, tail_ref), cast_out, (res_ref,), scratch = (
        refs[:n_casts], refs[n_casts:n_casts + 2], refs[n_casts + 2:2 * n_casts + 2],
        refs[2 * n_casts + 2:2 * n_casts + 3], refs[2 * n_casts + 3:])
    kind = pl.program_id(0)
    m = pl.program_id(1)
    if w_is_f32:
        w_f32_ref, (w_ref,) = w_ref, scratch

        @pl.when(m == 0)
        def _():
            w_ref[...] = w_f32_ref[...].astype(BF16)

    normed = kind < 2
    xn = xn_ref[...]
    sub = tm // dil
    pitch = _residue_stride(dil, sub)
    scatter = dil > 4
    heads_per_chunk = MXU_COLS // HEAD_DIM
    n_chunks = D_ATTN // MXU_COLS
    for j in range(n_chunks):
        p = jnp.dot(xn, w_ref[:, j * MXU_COLS:(j + 1) * MXU_COLS], preferred_element_type=F32)
        for src_ref, dst_ref in zip(cast_in, cast_out):
            part = src_ref.shape[0] // n_chunks
            dst_ref[j * part:(j + 1) * part, :] = src_ref[j * part:(j + 1) * part, :].astype(BF16)
        for hh in range(heads_per_chunk):
            h = j * heads_per_chunk + hh
            sl = slice(h * HEAD_DIM, (h + 1) * HEAD_DIM)
            ph = p[:, hh * HEAD_DIM:(hh + 1) * HEAD_DIM]
            r = lax.rsqrt(jnp.mean(ph * ph, axis=-1, keepdims=True) + EPS)
            res = (ph * jnp.where(normed, r, 1.0)) * jnp.where(normed, gain_ref[:, sl], 1.0)
            if scatter:
                for i in range(sub):
                    res_ref[h, pl.ds(i, dil, stride=pitch), :] = res[i * dil:(i + 1) * dil]
            else:
                res_ref[h] = res
            for rr in range(dil):
                if dil == 1:
                    rows = res
                elif scatter:
                    rows = res_ref[h, rr * pitch:rr * pitch + sub, :]
                else:
                    rows = res_ref[h, pl.ds(rr, sub, stride=dil), :]
                o_ref[rr, :, sl] = rows.astype(o_ref.dtype)

    @pl.when(jnp.logical_and(kind >= 1, m >= m_tail0))
    def _():
        for h in range(N_SLOTS):
            if scatter:
                i0 = (tm - tr) // dil
                for rr in range(dil):
                    tail_ref[pl.ds(rr, sub - i0, stride=dil), h, :] = res_ref[h, rr * pitch + i0:rr * pitch + sub, :]
            else:
                tail_ref[:, h, :] = res_ref[h, tm - tr:tm, :]


def _qkv_proj(xn, w_in, qk_gains, g, dil, out_dtype, tail_rows, tm=1024, casts=()):
    rows = xn.shape[0]
    tm = min(rows, tm)
    tn = D_ATTN
    tr = min(tail_rows, tm)
    m_tail0 = rows // tm - tail_rows // tr
    grid = (3, rows // tm)
    n_steps = grid[0] * grid[1]
    w_is_f32 = w_in.dtype == F32

    def tail_map(kind, m):
        return (jnp.where(kind >= 1, jnp.maximum(m - m_tail0, 0), 0), jnp.maximum(kind - 1, 0), 0, 0)

    def cast_spec(w):
        n_blocks = max(d for d in range(1, n_steps + 1) if w.shape[1] % (d * LANES) == 0)
        return pl.BlockSpec((w.shape[0], w.shape[1] // n_blocks),
                            lambda kind, m: (0, jnp.minimum(kind * grid[1] + m, n_blocks - 1)))

    cast_specs = [cast_spec(w) for w in casts]
    scratch = [pltpu.VMEM((N_SLOTS, dil * _residue_stride(dil, tm // dil), HEAD_DIM), F32)]
    if w_is_f32:
        scratch.append(pltpu.VMEM((D_MODEL, tn), BF16))
    return pl.pallas_call(
        functools.partial(_qkv_kernel, dil=dil, tm=tm, tr=tr, m_tail0=m_tail0, n_casts=len(casts), w_is_f32=w_is_f32),
        grid=grid,
        in_specs=[pl.BlockSpec((tm, D_MODEL), lambda kind, m: (m, 0)),
                  pl.BlockSpec((D_MODEL, tn), lambda kind, m: (0, kind * N_GROUPS + g)),
                  pl.BlockSpec((None, 1, tn), lambda kind, m: (jnp.minimum(kind, 1), 0, 0))] + cast_specs,
        out_specs=[pl.BlockSpec((None, dil, tm // dil, tn), lambda kind, m: (kind, 0, m, 0)),
                   pl.BlockSpec((tr, None, N_SLOTS, HEAD_DIM), tail_map)] + cast_specs,
        out_shape=[jax.ShapeDtypeStruct((3, dil, rows // dil, tn), out_dtype),
                   jax.ShapeDtypeStruct((tail_rows, 2, N_SLOTS, HEAD_DIM), F32)]
                  + [jax.ShapeDtypeStruct(w.shape, BF16) for w in casts],
        scratch_shapes=scratch,
        compiler_params=_params(("arbitrary", "arbitrary")),
        name=f"qkv_proj_g{g}",
    )(xn, w_in, qk_gains, *casts)


def _branch_b_kernel(*refs, tm, per_row_state, side):
    if per_row_state:
        xn_ref, wb_ref, wc_ref, wx_ref, wg_ref, cw_ref, hm2_ref, hm1_ref, y_ref, h_ref = refs
    else:
        (xn_ref, wb_ref, wc_ref, wx_ref, wg_ref, cw_ref), refs = refs[:6], refs[6:]
        if side is not None:
            side_in, (y_ref, h_ref), side_out, (hbuf_ref,), side_scratch = (
                refs[:6], refs[6:8], refs[8:10], refs[10:11], refs[11:])
        else:
            y_ref, h_ref, hbuf_ref = refs

        @pl.when(pl.program_id(1) == 0)
        def _():
            hbuf_ref[0:SUBLANES, :] = jnp.zeros((SUBLANES, hbuf_ref.shape[1]), F32)

        if side is not None:
            _attn_side(side, pl.program_id(0) * pl.num_programs(1) + pl.program_id(1),
                       *side_in, *side_out, *side_scratch)

    xn = xn_ref[...]
    pb = jnp.dot(xn, wb_ref[...], preferred_element_type=F32)
    pc = jnp.dot(xn, wc_ref[...], preferred_element_type=F32)
    px = jnp.dot(xn, wx_ref[...], preferred_element_type=F32)
    pg = jnp.dot(xn, wg_ref[...], preferred_element_type=F32)
    h = pc * px
    if per_row_state:
        hm2 = hm2_ref[...]
        hm1 = hm1_ref[...]
        h_ref[...] = h
    else:
        hbuf_ref[SUBLANES:SUBLANES + tm, :] = h
        hm2 = hbuf_ref[pl.ds(SUBLANES - 2, tm), :]
        hm1 = hbuf_ref[pl.ds(SUBLANES - 1, tm), :]
        last = hbuf_ref[pl.ds(tm, SUBLANES), :]
        h_ref[...] = last
        hbuf_ref[0:SUBLANES, :] = last
    z = cw_ref[0:1, :] * hm2 + cw_ref[1:2, :] * hm1 + cw_ref[2:3, :] * h
    y_ref[...] = (pb * z * (pg * jax.nn.sigmoid(pg))).astype(y_ref.dtype)


def _branch_b(xn, w_in, conv_w, state=None, attn=None):
    rows = xn.shape[0]
    tm = min(rows, 1024)
    tnb = 256
    per_row_state = state is not None
    grid = (D_CONV // tnb, rows // tm)
    side = None if attn is None else _attn_side_plan(*attn, n_steps=grid[0] * grid[1],
                                                     step_of=lambda j, m: j * grid[1] + m)

    def wspec(col0):
        return pl.BlockSpec((D_MODEL, tnb), lambda j, m: (0, col0 // tnb + j))

    in_specs = [pl.BlockSpec((tm, D_MODEL), lambda j, m: (m, 0)),
                wspec(COL_B), wspec(COL_C), wspec(COL_XIN), wspec(COL_GATE_B),
                pl.BlockSpec((CONV_W, tnb), lambda j, m: (0, j))]
    args = [xn, w_in, w_in, w_in, w_in, conv_w]
    scratch = []
    if per_row_state:
        st = state.reshape(rows, (CONV_W - 1) * D_CONV)
        in_specs += [pl.BlockSpec((tm, tnb), lambda j, m: (m, j)),
                     pl.BlockSpec((tm, tnb), lambda j, m: (m, D_CONV // tnb + j))]
        args += [st, st]
        h_spec = pl.BlockSpec((tm, tnb), lambda j, m: (m, j))
        h_shape = jax.ShapeDtypeStruct((rows, D_CONV), F32)
    else:
        scratch.append(pltpu.VMEM((tm + SUBLANES, tnb), F32))
        h_spec = pl.BlockSpec((SUBLANES, tnb), lambda j, m: (0, j))
        h_shape = jax.ShapeDtypeStruct((SUBLANES, D_CONV), F32)
    out_specs = [pl.BlockSpec((tm, tnb), lambda j, m: (m, j)), h_spec]
    out_shape = [jax.ShapeDtypeStruct((rows, D_CONV), BF16), h_shape]
    if side is not None:
        in_specs += side.in_specs
        args += side.args
        out_specs += side.out_specs
        out_shape += side.out_shape
        scratch += side.scratch
    outs = pl.pallas_call(
        functools.partial(_branch_b_kernel, tm=tm, per_row_state=per_row_state,
                          side=None if side is None else side.static),
        grid=grid,
        in_specs=in_specs,
        out_specs=out_specs,
        out_shape=out_shape,
        scratch_shapes=scratch,
        compiler_params=_params(("arbitrary", "arbitrary")),
        name="branch_b",
    )(*args)
    return outs if side is None else (outs[0], outs[1], side.finish(outs[2], outs[3]))


GATE_TILE = 1024
N_GATE_TILES = (2 * D_MODEL + D_ATTN) // GATE_TILE
SILU_TILE = N_GATE_TILES - 1


def _gate_columns(xn, w_ref, o_ref, is_silu):
    for j in range(w_ref.shape[1] // MXU_COLS):
        cs = slice(j * MXU_COLS, (j + 1) * MXU_COLS)
        p = jnp.dot(xn, w_ref[:, cs], preferred_element_type=F32)
        o_ref[:, cs] = jax.nn.sigmoid(p) * jnp.where(is_silu, p, 1.0)


def _gate_weight_block(tile):
    return jnp.where(tile == SILU_TILE, COL_GATE_A // GATE_TILE, COL_MERGE_A // GATE_TILE + tile)


def _gates_kernel(xn_ref, w_ref, *refs, tile0, side):
    if side is not None:
        side_in, o_ref, side_out, side_scratch = refs[:6], refs[6], refs[7:9], refs[9:]
        _attn_side(side, pl.program_id(0) * pl.num_programs(1) + pl.program_id(1),
                   *side_in, *side_out, *side_scratch)
    else:
        o_ref, = refs
    _gate_columns(xn_ref[...], w_ref, o_ref, tile0 + pl.program_id(0) == SILU_TILE)


def _gates(xn, w_in, tile0, n_tiles, attn=None):
    rows = xn.shape[0]
    tm = min(rows, 1024)
    tn = GATE_TILE
    grid = (n_tiles, rows // tm)
    side = None if attn is None else _attn_side_plan(*attn, n_steps=grid[0] * grid[1],
                                                     step_of=lambda n, m: n * grid[1] + m)

    def wmap(n, m):
        return (0, _gate_weight_block(tile0 + n))

    in_specs = [pl.BlockSpec((tm, D_MODEL), lambda n, m: (m, 0)), pl.BlockSpec((D_MODEL, tn), wmap)]
    args = [xn, w_in]
    out_specs = [pl.BlockSpec((tm, tn), lambda n, m: (m, n))]
    out_shape = [jax.ShapeDtypeStruct((rows, n_tiles * tn), F32)]
    scratch = []
    if side is not None:
        in_specs += side.in_specs
        args += side.args
        out_specs += side.out_specs
        out_shape += side.out_shape
        scratch += side.scratch
    outs = pl.pallas_call(
        functools.partial(_gates_kernel, tile0=tile0, side=None if side is None else side.static),
        grid=grid,
        in_specs=in_specs,
        out_specs=out_specs,
        out_shape=out_shape,
        scratch_shapes=scratch,
        compiler_params=_params(("arbitrary", "arbitrary")),
        name="gates",
    )(*args)
    return outs[0] if side is None else (outs[0], side.finish(outs[1], outs[2]))


def _t5_bucket(dist):
    d = np.maximum(dist, 1).astype(np.float32)
    large = MAX_EXACT + (np.log(d / np.float32(MAX_EXACT)) / np.float32(math.log(MAX_DISTANCE / MAX_EXACT))
                         * np.float32(NUM_BUCKETS - MAX_EXACT)).astype(np.int32)
    large = np.minimum(large, NUM_BUCKETS - 1)
    return np.where(dist < MAX_EXACT, dist, large)


def _bias_by_offset(rel_bias):
    steps = N_BACK - np.arange(N_BACK + 1)
    idx = np.stack([_t5_bucket(d * steps) for d in DILATIONS])
    g_idx = np.arange(N_GROUPS)[:, None]
    vals = jnp.transpose(rel_bias[idx, g_idx], (0, 2, 1)).astype(F32)
    return jnp.pad(vals, ((0, 0), (0, 0), (0, 2 * BLOCK - N_BACK - 1)), constant_values=NEG_INF)


def _sample_bias(cvec):
    kv_head = np.arange(N_BACK * 2 * N_SLOTS) % (2 * N_SLOTS)
    own_k_row = kv_head[None, :] == np.arange(N_SLOTS)[:, None]
    per_row = jnp.repeat(cvec[:, :, :N_BACK], 2 * N_SLOTS, axis=2)
    return jnp.where(own_k_row[None], per_row, NEG_INF), cvec[:, :, N_BACK:N_BACK + 1]


class _SideStatic(NamedTuple):
    per_step: int
    nb: int
    steps: int


class _SidePlan(NamedTuple):
    static: _SideStatic
    in_specs: list
    args: list
    out_specs: list
    out_shape: list
    scratch: list
    finish: Callable


def _attn_side(side, step, cvec_ref, q_ref, kp_ref, kc_ref, vp_ref, vc_ref, o_ref, lse_ref, bias_ref, s_ref, p_ref):
    @pl.when(step == 0)
    def _():
        for h in range(N_SLOTS):
            row = jnp.broadcast_to(cvec_ref[h:h + 1, :], (BLOCK, 2 * BLOCK))
            full = pltpu.roll(row, 0, 1, stride=1, stride_axis=0)
            bias_ref[1, h] = full
            bias_ref[0, h, :, :BLOCK] = jnp.full((BLOCK, BLOCK), NEG_INF, F32)
            bias_ref[0, h, :, BLOCK:] = full[:, BLOCK:]

    nt = (((1,), (1,)), ((), ()))
    heads = [slice(h * HEAD_DIM, (h + 1) * HEAD_DIM) for h in range(N_SLOTS)]
    unit0 = jnp.minimum(step, side.steps - 1) * side.per_step
    lane = lax.broadcasted_iota(jnp.int32, (BLOCK, LANES), 1)
    for u in range(side.per_step):
        rows = slice(u * BLOCK, (u + 1) * BLOCK)
        prev = slice((u - 1) * BLOCK, u * BLOCK)
        for h, sl in enumerate(heads):
            q = q_ref[rows, sl]
            k_prev = kp_ref[:, sl] if u == 0 else kc_ref[prev, sl]
            s_ref[u, h, :, :BLOCK] = lax.dot_general(q, k_prev, nt, preferred_element_type=F32)
            s_ref[u, h, :, BLOCK:] = lax.dot_general(q, kc_ref[rows, sl], nt, preferred_element_type=F32)
    ls, lses = [], []
    for u in range(side.per_step):
        has_prev = jnp.minimum((unit0 + u) % side.nb, 1)
        sc = s_ref[u] * SCALE + bias_ref[has_prev]
        mx = jnp.max(jnp.maximum(sc[:, :, :BLOCK], sc[:, :, BLOCK:]), axis=-1, keepdims=True)
        p = jnp.exp(sc - mx)
        l = jnp.sum(p[:, :, :BLOCK] + p[:, :, BLOCK:], axis=-1, keepdims=True)
        p_ref[u] = p.astype(BF16)
        ls.append(l)
        lses.append(mx + jnp.log(l))
    for u in range(side.per_step):
        rows = slice(u * BLOCK, (u + 1) * BLOCK)
        prev = slice((u - 1) * BLOCK, u * BLOCK)
        lse_all = jnp.zeros((BLOCK, LANES), F32)
        for h, sl in enumerate(heads):
            v_prev = vp_ref[:, sl] if u == 0 else vc_ref[prev, sl]
            o = (jnp.dot(p_ref[u, h, :, :BLOCK], v_prev, preferred_element_type=F32)
                 + jnp.dot(p_ref[u, h, :, BLOCK:], vc_ref[rows, sl], preferred_element_type=F32))
            o_ref[rows, sl] = o / ls[u][h]
            lse_all = jnp.where(lane == h, lses[u][h], lse_all)
        lse_ref[rows, :] = lse_all


def _attn_side_plan(qkv, cvec, g, n_steps, step_of):
    _, dil, ls, _ = qkv.shape
    nb = ls // BLOCK
    units = dil * nb
    per_step = min(d for d in range(1, units + 1) if units % d == 0 and d * n_steps >= units)
    static = _SideStatic(per_step=per_step, nb=nb, steps=units // per_step)
    flat = qkv.reshape(3, dil * ls, D_ATTN)
    rows = per_step * BLOCK

    def cur(kind):
        return pl.BlockSpec((None, rows, D_ATTN), lambda *idx: (kind, jnp.minimum(step_of(*idx), static.steps - 1), 0))

    def prev(kind):
        def index(*idx):
            first_unit = jnp.minimum(step_of(*idx), static.steps - 1) * per_step
            return (kind, jnp.maximum(first_unit - 1, 0), 0)
        return pl.BlockSpec((None, BLOCK, D_ATTN), index)

    def out_spec(width):
        return pl.BlockSpec((rows, width), lambda *idx: (jnp.minimum(step_of(*idx), static.steps - 1), 0))

    return _SidePlan(
        static=static,
        in_specs=[pl.BlockSpec((None, N_SLOTS, 2 * BLOCK), lambda *idx: (g, 0, 0)),
                  cur(0), prev(1), cur(1), prev(2), cur(2)],
        args=[cvec, flat, flat, flat, flat, flat],
        out_specs=[out_spec(D_ATTN), out_spec(LANES)],
        out_shape=[jax.ShapeDtypeStruct((dil * ls, D_ATTN), F32), jax.ShapeDtypeStruct((dil * ls, LANES), F32)],
        scratch=[pltpu.VMEM((2, N_SLOTS, BLOCK, 2 * BLOCK), F32),
                 pltpu.VMEM((per_step, N_SLOTS, BLOCK, 2 * BLOCK), F32),
                 pltpu.VMEM((per_step, N_SLOTS, BLOCK, 2 * BLOCK), BF16)],
        finish=lambda o, lse: (o.reshape(dil, ls, D_ATTN), lse.reshape(dil, ls, LANES)),
    )


def _attn_sample_kernel(bias_ref, nbias_ref, q0_ref, q1_ref, q2_ref, kv0_ref, kv1_ref, kv2_ref,
                        c0_ref, c1_ref, c2_ref, *refs, tb, gate_tile):
    if gate_tile is None:
        o_ref, slab_ref, s_ref, p_ref, og_ref = refs
    else:
        xn_ref, w_ref, o_ref, gate_ref, slab_ref, s_ref, p_ref, og_ref = refs
        _gate_columns(xn_ref[...], w_ref, gate_ref, gate_tile == SILU_TILE)
    rows = pl.ds(pl.program_id(0) * tb, tb)
    nt = (((1,), (1,)), ((), ()))
    kv_rows = 2 * N_SLOTS
    slab_rows = N_BACK * kv_rows
    q_refs = (q0_ref, q1_ref, q2_ref)
    kv_refs = (kv0_ref, kv1_ref, kv2_ref)
    c_refs = (c0_ref, c1_ref, c2_ref)

    def rounded(x):
        return x.astype(BF16).astype(F32)

    qs = [q_ref[rows] for q_ref in q_refs]
    for g in range(N_GROUPS):
        for t in range(tb):
            slab_ref[g, t] = c_refs[g][t].reshape(slab_rows, HEAD_DIM).astype(BF16)
            s_ref[g, t] = lax.dot_general(qs[g][t].astype(BF16), slab_ref[g, t], nt, preferred_element_type=F32)
    p_news, ls, lses = [], [], []
    for g in range(N_GROUPS):
        s_n = jnp.sum(rounded(qs[g]) * rounded(kv_refs[g][rows, 0]), axis=-1, keepdims=True)
        sc_c = s_ref[g] * SCALE + bias_ref[g][None]
        sc_n = s_n * SCALE + nbias_ref[g][None]
        mx = jnp.maximum(jnp.max(sc_c, axis=-1, keepdims=True), sc_n)
        p_c = jnp.exp(sc_c - mx)
        p_n = jnp.exp(sc_n - mx)
        l = jnp.sum(p_c, axis=-1, keepdims=True) + p_n
        p_ref[g] = pltpu.roll(p_c, N_SLOTS, 2).astype(BF16)
        p_news.append(p_n)
        ls.append(l)
        lses.append(mx + jnp.log(l))
    for g in range(N_GROUPS):
        for t in range(tb):
            og_ref[g, t] = jnp.dot(p_ref[g, t], slab_ref[g, t], preferred_element_type=F32)
    outs = [(og_ref[g] + rounded(p_news[g]) * rounded(kv_refs[g][rows, 1])) / ls[g] for g in range(N_GROUPS)]
    top = jnp.maximum(jnp.maximum(lses[0], lses[1]), lses[2])
    es = [jnp.exp(x - top) for x in lses]
    tot = es[0] + es[1] + es[2]
    o_ref[rows] = (es[0] / tot) * outs[0] + (es[1] / tot) * outs[1] + (es[2] / tot) * outs[2]


def _attn_sample(qs, kvs, caches, bias, nbias, gate_rider=None):
    db = qs[0].shape[0]
    tb = 4
    kv_rows = 2 * N_SLOTS
    n_steps = db // tb
    rider_in_specs, rider_args, rider_out_specs, rider_out_shape, gate_tile = [], [], [], [], None
    if gate_rider is not None:
        xn, w_in, gate_tile = gate_rider
        rows_per_step = xn.shape[0] // n_steps
        rider_in_specs = [pl.BlockSpec((rows_per_step, D_MODEL), lambda s: (s, 0)),
                          pl.BlockSpec((D_MODEL, GATE_TILE), lambda s: (0, _gate_weight_block(gate_tile)))]
        rider_args = [xn, w_in]
        rider_out_specs = [pl.BlockSpec((rows_per_step, GATE_TILE), lambda s: (s, 0))]
        rider_out_shape = [jax.ShapeDtypeStruct((xn.shape[0], GATE_TILE), F32)]
    views, specs = [], []
    for g, c in enumerate(caches):
        dil = DILATIONS[g]
        lc = c.shape[1]
        views.append(c.reshape(db, lc // dil, dil * kv_rows, HEAD_DIM))
        specs.append(pl.BlockSpec((tb, N_BACK, kv_rows, HEAD_DIM), lambda s: (s, 0, 0, 0)))
    q_spec = pl.BlockSpec((db, N_SLOTS, HEAD_DIM), lambda s: (0, 0, 0))
    kv_spec = pl.BlockSpec((db, 2, N_SLOTS, HEAD_DIM), lambda s: (0, 0, 0, 0))
    outs = pl.pallas_call(
        functools.partial(_attn_sample_kernel, tb=tb, gate_tile=gate_tile),
        grid=(n_steps,),
        in_specs=[pl.BlockSpec((N_GROUPS, N_SLOTS, N_BACK * kv_rows), lambda s: (0, 0, 0)),
                  pl.BlockSpec((N_GROUPS, N_SLOTS, 1), lambda s: (0, 0, 0)),
                  q_spec, q_spec, q_spec, kv_spec, kv_spec, kv_spec] + specs + rider_in_specs,
        out_specs=[pl.BlockSpec((db, N_SLOTS, HEAD_DIM), lambda s: (0, 0, 0))] + rider_out_specs,
        out_shape=[jax.ShapeDtypeStruct((db, N_SLOTS, HEAD_DIM), F32)] + rider_out_shape,
        scratch_shapes=[pltpu.VMEM((N_GROUPS, tb, N_BACK * kv_rows, HEAD_DIM), BF16),
                        pltpu.VMEM((N_GROUPS, tb, N_SLOTS, N_BACK * kv_rows), F32),
                        pltpu.VMEM((N_GROUPS, tb, N_SLOTS, N_BACK * kv_rows), BF16),
                        pltpu.VMEM((N_GROUPS, tb, N_SLOTS, HEAD_DIM), F32)],
        compiler_params=_params(("arbitrary",)),
        name="attn_sample",
    )(bias, nbias, *qs, *kvs, *views, *rider_args)
    return outs[0] if gate_rider is None else outs


def _out_kernel(*refs, tm, n_groups):
    o_refs = refs[:n_groups]
    lse_refs = refs[n_groups:2 * n_groups] if n_groups > 1 else ()
    rest = refs[len(o_refs) + len(lse_refs):]
    def project(a):
        ya = jnp.dot(a, woa_ref[...], preferred_element_type=F32)
        yb = jnp.dot(yb_ref[...], wob_ref[...], preferred_element_type=F32)
        merged = sma_ref[...] * ya + smb_ref[...] * yb
        y_ref[...] = x_ref[...] + jnp.dot(merged.astype(BF16), wo_ref[...], preferred_element_type=F32)

    if n_groups > 1:
        sga_ref, sma_ref, smb_ref, yb_ref, x_ref, woa_ref, wob_ref, wo_ref, y_ref, a2_ref, og_ref, lg_ref = rest
        step = pl.program_id(0)

        @pl.when(step == 0)
        def _():
            a2_ref[1] = jnp.zeros(a2_ref.shape[1:], BF16)

        project(a2_ref[(step + 1) % 2])
        a_ref = a2_ref.at[step % 2]
        for g in range(n_groups):
            dil = DILATIONS[g]
            sub = tm // dil
            for r in range(dil):
                rows = slice(None) if dil == 1 else pl.ds(r, sub, stride=dil)
                lg_ref[g, rows, :] = lse_refs[g][r]
                for h in range(N_SLOTS):
                    og_ref[g, h, rows, :] = o_refs[g][r, :, h * HEAD_DIM:(h + 1) * HEAD_DIM]
        lses = [lg_ref[g] for g in range(n_groups)]
        top = functools.reduce(jnp.maximum, lses)
        es = [jnp.exp(x - top) for x in lses]
        tot = functools.reduce(lambda a, b: a + b, es)
        ws = [e / tot for e in es]
        for h in range(N_SLOTS):
            sl = slice(h * HEAD_DIM, (h + 1) * HEAD_DIM)
            o = functools.reduce(lambda a, b: a + b,
                                 [ws[g][:, h:h + 1] * og_ref[g, h] for g in range(n_groups)])
            a_ref[:, sl] = (o * sga_ref[:, sl]).astype(BF16)
    else:
        sga_ref, sma_ref, smb_ref, yb_ref, x_ref, woa_ref, wob_ref, wo_ref, y_ref = rest
        project((o_refs[0][...] * sga_ref[...]).astype(BF16))


def _out_proj(os_, lses, gates, yb_in, x, w_out_a, w_out_b, w_o):
    rows = x.shape[0]

    tm = min(rows, 256)
    n_groups = len(os_)
    n_blocks = rows // tm
    pipelined = n_groups > 1
    n_steps = n_blocks + 1 if pipelined else n_blocks

    def comb_block(s):
        return jnp.minimum(s, n_blocks - 1)

    def proj_block(s):
        return jnp.maximum(s - 1, 0) if pipelined else s

    def gate_operand(tile, width, block_of):
        arr, tile0 = next((a, t0) for a, t0 in gates if t0 <= tile < t0 + a.shape[1] // GATE_TILE)
        assert (tile - tile0) * GATE_TILE % width == 0
        return arr, pl.BlockSpec((tm, width), lambda s: (block_of(s), (tile - tile0) * GATE_TILE // width))

    def row_spec(width, block_of):
        return pl.BlockSpec((tm, width), lambda s: (block_of(s), 0))

    def sub_spec(dil, width):
        return pl.BlockSpec((dil, tm // dil, width), lambda s: (0, comb_block(s), 0))

    def const_spec(shape):
        return pl.BlockSpec(shape, lambda s: (0, 0), pipeline_mode=pl.Buffered(1))

    if pipelined:
        in_specs = ([sub_spec(DILATIONS[g], D_ATTN) for g in range(n_groups)]
                    + [sub_spec(DILATIONS[g], LANES) for g in range(n_groups)])
        scratch = [pltpu.VMEM((2, tm, D_ATTN), BF16),
                   pltpu.VMEM((n_groups, N_SLOTS, tm, HEAD_DIM), F32), pltpu.VMEM((n_groups, tm, LANES), F32)]
    else:
        in_specs = [row_spec(D_ATTN, comb_block)]
        scratch = []
    sga, sga_spec = gate_operand(SILU_TILE, D_ATTN, comb_block)
    sma, sma_spec = gate_operand(0, D_MODEL, proj_block)
    smb, smb_spec = gate_operand(D_MODEL // GATE_TILE, D_MODEL, proj_block)
    in_specs += [sga_spec, sma_spec, smb_spec, row_spec(D_CONV, proj_block), row_spec(D_MODEL, proj_block),
                 const_spec((D_ATTN, D_MODEL)), const_spec((D_CONV, D_MODEL)), const_spec((D_MODEL, D_MODEL))]
    return pl.pallas_call(
        functools.partial(_out_kernel, tm=tm, n_groups=n_groups),
        grid=(n_steps,),
        in_specs=in_specs,
        out_specs=row_spec(D_MODEL, proj_block),
        out_shape=jax.ShapeDtypeStruct((rows, D_MODEL), F32),
        scratch_shapes=scratch,
        compiler_params=_params(("arbitrary",)),
        name="out_proj",
    )(*os_, *lses, sga, sma, smb, yb_in, x, w_out_a, w_out_b, w_o)


def kernel(x_prompt, x_sample, cache_kv_w128, cache_kv_w512, cache_kv_w2048, state_conv, norm_gain, w_in,
           q_norm_gain, k_norm_gain, rel_bias, conv_w, w_out_a, w_out_b, w_o):
    seq = x_prompt.shape[1]
    db = x_sample.shape[0]
    xp = x_prompt.reshape(seq, D_MODEL)
    xs = x_sample.reshape(db, D_MODEL)
    caches = (cache_kv_w128, cache_kv_w512, cache_kv_w2048)
    qk_gains = jnp.stack([jnp.tile(q_norm_gain, N_SLOTS), jnp.tile(k_norm_gain, N_SLOTS)]).reshape(2, 1, D_ATTN)
    cvec = _bias_by_offset(rel_bias)

    xn = _pre_norm(xp, norm_gain)
    qkv0, kv_p0, w_in = _qkv_proj(xn, w_in, qk_gains, 0, DILATIONS[0], BF16, min(WINDOWS[0], seq), tm=512,
                                  casts=(w_in,))
    qkv1, kv_p1, woa, wob, wo = _qkv_proj(xn, w_in, qk_gains, 1, DILATIONS[1], BF16, min(WINDOWS[1], seq),
                                          casts=(w_out_a, w_out_b, w_o))
    qkv2, kv_p2 = _qkv_proj(xn, w_in, qk_gains, 2, DILATIONS[2], BF16, min(WINDOWS[2], seq))
    qkvs, kv_p = (qkv0, qkv1, qkv2), (kv_p0, kv_p1, kv_p2)
    xn_s = _pre_norm(xs, norm_gain)
    qkvs_s, kv_s = zip(*[_qkv_proj(xn_s, w_in, qk_gains, g, 1, F32, db) for g in range(N_GROUPS)])
    yb_in_s, h_s = _branch_b(xn_s, w_in, conv_w, state=state_conv)
    gates_s = [(_gates(xn_s, w_in, 0, N_GATE_TILES), 0)]
    qs_s = [qkv[0, 0].reshape(db, N_SLOTS, HEAD_DIM) for qkv in qkvs_s]
    o_s, gates_c = _attn_sample(qs_s, kv_s, caches, *_sample_bias(cvec), gate_rider=(xn, w_in, SILU_TILE))
    y_sample = _out_proj((o_s.reshape(db, D_ATTN),), (), gates_s, yb_in_s, xs, woa, wob, wo).reshape(db, 1, D_MODEL)

    gates_a, attn0 = _gates(xn, w_in, 0, 2, attn=(qkvs[0], cvec, 0))
    gates_b, attn1 = _gates(xn, w_in, 2, 2, attn=(qkvs[1], cvec, 1))
    yb_in, h_last, attn2 = _branch_b(xn, w_in, conv_w, attn=(qkvs[2], cvec, 2))
    gates = [(gates_a, 0), (gates_b, 2), (gates_c, SILU_TILE)]
    os_, lses = zip(attn0, attn1, attn2)
    y_prompt = _out_proj(os_, lses, gates, yb_in, xp, woa, wob, wo).reshape(1, seq, D_MODEL)
    conv_p = h_last[SUBLANES - (CONV_W - 1):][None]
    conv_s = jnp.stack([state_conv[:, CONV_W - 2], h_s], axis=1)

    return (y_prompt, y_sample, kv_p[0][None], kv_p[1][None], kv_p[2][None], conv_p,
            kv_s[0][:, None], kv_s[1][:, None], kv_s[2][:, None], conv_s)
```

```python
import functools
import math
from typing import Callable, NamedTuple

import numpy as np
import jax
import jax.numpy as jnp
from jax import lax
from jax.experimental import pallas as pl
from jax.experimental.pallas import tpu as pltpu

D_MODEL = 2048
N_GROUPS = 3
DILATIONS = (1, 4, 16)
N_BACK = 128
WINDOWS = (128, 512, 2048)
N_SLOTS = 8
HEAD_DIM = 128
D_ATTN = N_SLOTS * HEAD_DIM
QKV_COLS = N_GROUPS * D_ATTN
D_CONV = D_MODEL // 2
CONV_W = 3
BLOCK = N_BACK
NUM_BUCKETS = 32
MAX_EXACT = NUM_BUCKETS // 2
MAX_DISTANCE = 2048
EPS = 1e-6
SCALE = HEAD_DIM ** -0.5

COL_K = QKV_COLS
COL_V = 2 * QKV_COLS
COL_GATE_A = 3 * QKV_COLS
COL_B = COL_GATE_A + D_ATTN
COL_C = COL_B + D_CONV
COL_XIN = COL_C + D_CONV
COL_GATE_B = COL_XIN + D_CONV
COL_MERGE_A = COL_GATE_B + D_CONV
COL_MERGE_B = COL_MERGE_A + D_MODEL

SUBLANES = 8
LANES = 128
MXU_COLS = 256
VMEM_LIMIT_BYTES = 56 * 1024 * 1024

BF16 = jnp.bfloat16
F32 = jnp.float32
NEG_INF = float("-inf")


def _params(semantics):
    return pltpu.CompilerParams(dimension_semantics=semantics, vmem_limit_bytes=VMEM_LIMIT_BYTES)


def _norm_kernel(x_ref, g_ref, o_ref):
    x = x_ref[...]
    r = lax.rsqrt(jnp.mean(x * x, axis=-1, keepdims=True) + EPS)
    o_ref[...] = ((x * r) * g_ref[...]).astype(o_ref.dtype)


def _pre_norm(x, gain):
    rows = x.shape[0]
    tm = min(rows, 512)
    return pl.pallas_call(
        _norm_kernel,
        grid=(rows // tm,),
        in_specs=[pl.BlockSpec((tm, D_MODEL), lambda m: (m, 0)),
                  pl.BlockSpec((1, D_MODEL), lambda m: (0, 0))],
        out_specs=pl.BlockSpec((tm, D_MODEL), lambda m: (m, 0)),
        out_shape=jax.ShapeDtypeStruct((rows, D_MODEL), BF16),
        compiler_params=_params(("arbitrary",)),
        name="pre_norm",
    )(x, gain.reshape(1, D_MODEL))


def _residue_stride(dil, sub):
    return sub + SUBLANES if dil > 4 else sub


def _qkv_kernel(xn_ref, w_ref, gain_ref, *refs, dil, tm, tr, m_tail0, n_casts, w_is_f32):
    cast_in, (o_ref, tail_ref), cast_out, (res_ref,), scratch = (
        refs[:n_casts], refs[n_casts:n_casts + 2], refs[n_casts + 2:2 * n_casts + 2],
        refs[2 * n_casts + 2:2 * n_casts + 3], refs[2 * n_casts + 3:])
    kind = pl.program_id(0)
    m = pl.program_id(1)
    if w_is_f32:
        w_f32_ref, (w_ref,) = w_ref, scratch

        @pl.when(m == 0)
        def _():
            w_ref[...] = w_f32_ref[...].astype(BF16)

    normed = kind < 2
    xn = xn_ref[...]
    sub = tm // dil
    pitch = _residue_stride(dil, sub)
    scatter = dil > 4
    heads_per_chunk = MXU_COLS // HEAD_DIM
    n_chunks = D_ATTN // MXU_COLS
    for j in range(n_chunks):
        p = jnp.dot(xn, w_ref[:, j * MXU_COLS:(j + 1) * MXU_COLS], preferred_element_type=F32)
        for src_ref, dst_ref in zip(cast_in, cast_out):
            part = src_ref.shape[0] // n_chunks
            dst_ref[j * part:(j + 1) * part, :] = src_ref[j * part:(j + 1) * part, :].astype(BF16)
        for hh in range(heads_per_chunk):
            h = j * heads_per_chunk + hh
            sl = slice(h * HEAD_DIM, (h + 1) * HEAD_DIM)
            ph = p[:, hh * HEAD_DIM:(hh + 1) * HEAD_DIM]
            r = lax.rsqrt(jnp.mean(ph * ph, axis=-1, keepdims=True) + EPS)
            res = (ph * jnp.where(normed, r, 1.0)) * jnp.where(normed, gain_ref[:, sl], 1.0)
            if scatter:
                for i in range(sub):
                    res_ref[h, pl.ds(i, dil, stride=pitch), :] = res[i * dil:(i + 1) * dil]
            else:
                res_ref[h] = res
            for rr in range(dil):
                if dil == 1:
                    rows = res
                elif scatter:
                    rows = res_ref[h, rr * pitch:rr * pitch + sub, :]
                else:
                    rows = res_ref[h, pl.ds(rr, sub, stride=dil), :]
                o_ref[rr, :, sl] = rows.astype(o_ref.dtype)

    @pl.when(jnp.logical_and(kind >= 1, m >= m_tail0))
    def _():
        for h in range(N_SLOTS):
            if scatter:
                i0 = (tm - tr) // dil
                for rr in range(dil):
                    tail_ref[pl.ds(rr, sub - i0, stride=dil), h, :] = res_ref[h, rr * pitch + i0:rr * pitch + sub, :]
            else:
                tail_ref[:, h, :] = res_ref[h, tm - tr:tm, :]


class _Cols(NamedTuple):
    arr: jax.Array
    col0: int

    def block(self, col_block, width):
        return col_block - self.col0 // width


def _qkv_proj(xn, w_in, qk_gains, g, dil, out_dtype, tail_rows, casts=()):
    rows = xn.shape[0]
    tm = min(rows, 1024)
    tn = D_ATTN
    tr = min(tail_rows, tm)
    m_tail0 = rows // tm - tail_rows // tr
    grid = (3, rows // tm)
    n_steps = grid[0] * grid[1]
    w_is_f32 = w_in.arr.dtype == F32

    def tail_map(kind, m):
        return (jnp.where(kind >= 1, jnp.maximum(m - m_tail0, 0), 0), jnp.maximum(kind - 1, 0), 0, 0)

    def cast_specs_of(w, col0, n_cols):
        n_blocks = max(d for d in range(1, n_steps + 1) if n_cols % (d * LANES) == 0)
        width = n_cols // n_blocks
        assert col0 % width == 0

        def block(kind, m):
            return jnp.minimum(kind * grid[1] + m, n_blocks - 1)

        return (pl.BlockSpec((w.shape[0], width), lambda kind, m: (0, col0 // width + block(kind, m))),
                pl.BlockSpec((w.shape[0], width), lambda kind, m: (0, block(kind, m))))

    cast_in_specs, cast_out_specs = zip(*[cast_specs_of(*c) for c in casts]) if casts else ((), ())
    scratch = [pltpu.VMEM((N_SLOTS, dil * _residue_stride(dil, tm // dil), HEAD_DIM), F32)]
    if w_is_f32:
        scratch.append(pltpu.VMEM((D_MODEL, tn), BF16))
    return pl.pallas_call(
        functools.partial(_qkv_kernel, dil=dil, tm=tm, tr=tr, m_tail0=m_tail0, n_casts=len(casts), w_is_f32=w_is_f32),
        grid=grid,
        in_specs=[pl.BlockSpec((tm, D_MODEL), lambda kind, m: (m, 0)),
                  pl.BlockSpec((D_MODEL, tn), lambda kind, m: (0, w_in.block(kind * N_GROUPS + g, tn))),
                  pl.BlockSpec((None, 1, tn), lambda kind, m: (jnp.minimum(kind, 1), 0, 0))] + list(cast_in_specs),
        out_specs=[pl.BlockSpec((None, dil, tm // dil, tn), lambda kind, m: (kind, 0, m, 0)),
                   pl.BlockSpec((tr, None, N_SLOTS, HEAD_DIM), tail_map)] + list(cast_out_specs),
        out_shape=[jax.ShapeDtypeStruct((3, dil, rows // dil, tn), out_dtype),
                   jax.ShapeDtypeStruct((tail_rows, 2, N_SLOTS, HEAD_DIM), F32)]
                  + [jax.ShapeDtypeStruct((w.shape[0], n_cols), BF16) for w, _, n_cols in casts],
        scratch_shapes=scratch,
        compiler_params=_params(("arbitrary", "arbitrary")),
        name=f"qkv_proj_g{g}",
    )(xn, w_in.arr, qk_gains, *[w for w, _, _ in casts])


def _branch_b_kernel(*refs, tm, per_row_state, side):
    if per_row_state:
        xn_ref, wb_ref, wc_ref, wx_ref, wg_ref, cw_ref, hm2_ref, hm1_ref, y_ref, h_ref = refs
    else:
        (xn_ref, wb_ref, wc_ref, wx_ref, wg_ref, cw_ref), refs = refs[:6], refs[6:]
        if side is not None:
            side_in, (y_ref, h_ref), side_out, (hbuf_ref,), side_scratch = (
                refs[:6], refs[6:8], refs[8:10], refs[10:11], refs[11:])
        else:
            y_ref, h_ref, hbuf_ref = refs

        @pl.when(pl.program_id(1) == 0)
        def _():
            hbuf_ref[0:SUBLANES, :] = jnp.zeros((SUBLANES, hbuf_ref.shape[1]), F32)

        if side is not None:
            _attn_side(side, pl.program_id(0) * pl.num_programs(1) + pl.program_id(1),
                       *side_in, *side_out, *side_scratch)

    xn = xn_ref[...]
    pb = jnp.dot(xn, wb_ref[...], preferred_element_type=F32)
    pc = jnp.dot(xn, wc_ref[...], preferred_element_type=F32)
    px = jnp.dot(xn, wx_ref[...], preferred_element_type=F32)
    pg = jnp.dot(xn, wg_ref[...], preferred_element_type=F32)
    h = pc * px
    if per_row_state:
        hm2 = hm2_ref[...]
        hm1 = hm1_ref[...]
        h_ref[...] = h
    else:
        hbuf_ref[SUBLANES:SUBLANES + tm, :] = h
        hm2 = hbuf_ref[pl.ds(SUBLANES - 2, tm), :]
        hm1 = hbuf_ref[pl.ds(SUBLANES - 1, tm), :]
        last = hbuf_ref[pl.ds(tm, SUBLANES), :]
        h_ref[...] = last
        hbuf_ref[0:SUBLANES, :] = last
    z = cw_ref[0:1, :] * hm2 + cw_ref[1:2, :] * hm1 + cw_ref[2:3, :] * h
    y_ref[...] = (pb * z * (pg * jax.nn.sigmoid(pg))).astype(y_ref.dtype)


def _branch_b(xn, w_in, conv_w, state=None, attn=None):
    rows = xn.shape[0]
    tm = min(rows, 1024)
    tnb = 256
    per_row_state = state is not None
    grid = (D_CONV // tnb, rows // tm)
    side = None if attn is None else _attn_side_plan(*attn, n_steps=grid[0] * grid[1],
                                                     step_of=lambda j, m: j * grid[1] + m)

    def wspec(col0):
        return pl.BlockSpec((D_MODEL, tnb), lambda j, m: (0, w_in.block(col0 // tnb + j, tnb)))

    in_specs = [pl.BlockSpec((tm, D_MODEL), lambda j, m: (m, 0)),
                wspec(COL_B), wspec(COL_C), wspec(COL_XIN), wspec(COL_GATE_B),
                pl.BlockSpec((CONV_W, tnb), lambda j, m: (0, j))]
    args = [xn, w_in.arr, w_in.arr, w_in.arr, w_in.arr, conv_w]
    scratch = []
    if per_row_state:
        st = state.reshape(rows, (CONV_W - 1) * D_CONV)
        in_specs += [pl.BlockSpec((tm, tnb), lambda j, m: (m, j)),
                     pl.BlockSpec((tm, tnb), lambda j, m: (m, D_CONV // tnb + j))]
        args += [st, st]
        h_spec = pl.BlockSpec((tm, tnb), lambda j, m: (m, j))
        h_shape = jax.ShapeDtypeStruct((rows, D_CONV), F32)
    else:
        scratch.append(pltpu.VMEM((tm + SUBLANES, tnb), F32))
        h_spec = pl.BlockSpec((SUBLANES, tnb), lambda j, m: (0, j))
        h_shape = jax.ShapeDtypeStruct((SUBLANES, D_CONV), F32)
    out_specs = [pl.BlockSpec((tm, tnb), lambda j, m: (m, j)), h_spec]
    out_shape = [jax.ShapeDtypeStruct((rows, D_CONV), BF16), h_shape]
    if side is not None:
        in_specs += side.in_specs
        args += side.args
        out_specs += side.out_specs
        out_shape += side.out_shape
        scratch += side.scratch
    outs = pl.pallas_call(
        functools.partial(_branch_b_kernel, tm=tm, per_row_state=per_row_state,
                          side=None if side is None else side.static),
        grid=grid,
        in_specs=in_specs,
        out_specs=out_specs,
        out_shape=out_shape,
        scratch_shapes=scratch,
        compiler_params=_params(("arbitrary", "arbitrary")),
        name="branch_b",
    )(*args)
    return outs if side is None else (outs[0], outs[1], side.finish(outs[2], outs[3]))


GATE_TILE = 1024
N_GATE_TILES = (2 * D_MODEL + D_ATTN) // GATE_TILE
SILU_TILE = N_GATE_TILES - 1


def _gate_columns(xn, w_ref, o_ref, is_silu):
    for j in range(w_ref.shape[1] // MXU_COLS):
        cs = slice(j * MXU_COLS, (j + 1) * MXU_COLS)
        p = jnp.dot(xn, w_ref[:, cs], preferred_element_type=F32)
        o_ref[:, cs] = jax.nn.sigmoid(p) * jnp.where(is_silu, p, 1.0)


def _gate_weight_block(tile):
    return jnp.where(tile == SILU_TILE, COL_GATE_A // GATE_TILE, COL_MERGE_A // GATE_TILE + tile)


def _gates_kernel(xn_ref, w_ref, *refs, tile0, side):
    if side is not None:
        side_in, o_ref, side_out, side_scratch = refs[:6], refs[6], refs[7:9], refs[9:]
        _attn_side(side, pl.program_id(0) * pl.num_programs(1) + pl.program_id(1),
                   *side_in, *side_out, *side_scratch)
    else:
        o_ref, = refs
    _gate_columns(xn_ref[...], w_ref, o_ref, tile0 + pl.program_id(0) == SILU_TILE)


def _gates(xn, w_in, tile0, n_tiles, attn=None):
    rows = xn.shape[0]
    tm = min(rows, 1024)
    tn = GATE_TILE
    grid = (n_tiles, rows // tm)
    side = None if attn is None else _attn_side_plan(*attn, n_steps=grid[0] * grid[1],
                                                     step_of=lambda n, m: n * grid[1] + m)

    def wmap(n, m):
        return (0, w_in.block(_gate_weight_block(tile0 + n), tn))

    in_specs = [pl.BlockSpec((tm, D_MODEL), lambda n, m: (m, 0)), pl.BlockSpec((D_MODEL, tn), wmap)]
    args = [xn, w_in.arr]
    out_specs = [pl.BlockSpec((tm, tn), lambda n, m: (m, n))]
    out_shape = [jax.ShapeDtypeStruct((rows, n_tiles * tn), F32)]
    scratch = []
    if side is not None:
        in_specs += side.in_specs
        args += side.args
        out_specs += side.out_specs
        out_shape += side.out_shape
        scratch += side.scratch
    outs = pl.pallas_call(
        functools.partial(_gates_kernel, tile0=tile0, side=None if side is None else side.static),
        grid=grid,
        in_specs=in_specs,
        out_specs=out_specs,
        out_shape=out_shape,
        scratch_shapes=scratch,
        compiler_params=_params(("arbitrary", "arbitrary")),
        name="gates",
    )(*args)
    return outs[0] if side is None else (outs[0], side.finish(outs[1], outs[2]))


def _t5_bucket(dist):
    d = np.maximum(dist, 1).astype(np.float32)
    large = MAX_EXACT + (np.log(d / np.float32(MAX_EXACT)) / np.float32(math.log(MAX_DISTANCE / MAX_EXACT))
                         * np.float32(NUM_BUCKETS - MAX_EXACT)).astype(np.int32)
    large = np.minimum(large, NUM_BUCKETS - 1)
    return np.where(dist < MAX_EXACT, dist, large)


def _bias_by_offset(rel_bias):
    steps = N_BACK - np.arange(N_BACK + 1)
    idx = np.stack([_t5_bucket(d * steps) for d in DILATIONS])
    g_idx = np.arange(N_GROUPS)[:, None]
    vals = jnp.transpose(rel_bias[idx, g_idx], (0, 2, 1)).astype(F32)
    return jnp.pad(vals, ((0, 0), (0, 0), (0, 2 * BLOCK - N_BACK - 1)), constant_values=NEG_INF)


def _sample_bias(cvec):
    kv_head = np.arange(N_BACK * 2 * N_SLOTS) % (2 * N_SLOTS)
    own_k_row = kv_head[None, :] == np.arange(N_SLOTS)[:, None]
    per_row = jnp.repeat(cvec[:, :, :N_BACK], 2 * N_SLOTS, axis=2)
    return jnp.where(own_k_row[None], per_row, NEG_INF), cvec[:, :, N_BACK:N_BACK + 1]


class _SideStatic(NamedTuple):
    per_step: int
    nb: int
    steps: int


class _SidePlan(NamedTuple):
    static: _SideStatic
    in_specs: list
    args: list
    out_specs: list
    out_shape: list
    scratch: list
    finish: Callable


def _attn_side(side, step, cvec_ref, q_ref, kp_ref, kc_ref, vp_ref, vc_ref, o_ref, lse_ref, bias_ref, s_ref, p_ref):
    @pl.when(step == 0)
    def _():
        for h in range(N_SLOTS):
            row = jnp.broadcast_to(cvec_ref[h:h + 1, :], (BLOCK, 2 * BLOCK))
            full = pltpu.roll(row, 0, 1, stride=1, stride_axis=0)
            bias_ref[1, h] = full
            bias_ref[0, h, :, :BLOCK] = jnp.full((BLOCK, BLOCK), NEG_INF, F32)
            bias_ref[0, h, :, BLOCK:] = full[:, BLOCK:]

    nt = (((1,), (1,)), ((), ()))
    heads = [slice(h * HEAD_DIM, (h + 1) * HEAD_DIM) for h in range(N_SLOTS)]
    unit0 = jnp.minimum(step, side.steps - 1) * side.per_step
    lane = lax.broadcasted_iota(jnp.int32, (BLOCK, LANES), 1)
    for u in range(side.per_step):
        rows = slice(u * BLOCK, (u + 1) * BLOCK)
        prev = slice((u - 1) * BLOCK, u * BLOCK)
        for h, sl in enumerate(heads):
            q = q_ref[rows, sl]
            k_prev = kp_ref[:, sl] if u == 0 else kc_ref[prev, sl]
            s_ref[u, h, :, :BLOCK] = lax.dot_general(q, k_prev, nt, preferred_element_type=F32)
            s_ref[u, h, :, BLOCK:] = lax.dot_general(q, kc_ref[rows, sl], nt, preferred_element_type=F32)
    ls, lses = [], []
    for u in range(side.per_step):
        has_prev = jnp.minimum((unit0 + u) % side.nb, 1)
        sc = s_ref[u] * SCALE + bias_ref[has_prev]
        mx = jnp.max(jnp.maximum(sc[:, :, :BLOCK], sc[:, :, BLOCK:]), axis=-1, keepdims=True)
        p = jnp.exp(sc - mx)
        l = jnp.sum(p[:, :, :BLOCK] + p[:, :, BLOCK:], axis=-1, keepdims=True)
        p_ref[u] = p.astype(BF16)
        ls.append(l)
        lses.append(mx + jnp.log(l))
    for u in range(side.per_step):
        rows = slice(u * BLOCK, (u + 1) * BLOCK)
        prev = slice((u - 1) * BLOCK, u * BLOCK)
        lse_all = jnp.zeros((BLOCK, LANES), F32)
        for h, sl in enumerate(heads):
            v_prev = vp_ref[:, sl] if u == 0 else vc_ref[prev, sl]
            o = (jnp.dot(p_ref[u, h, :, :BLOCK], v_prev, preferred_element_type=F32)
                 + jnp.dot(p_ref[u, h, :, BLOCK:], vc_ref[rows, sl], preferred_element_type=F32))
            o_ref[rows, sl] = o / ls[u][h]
            lse_all = jnp.where(lane == h, lses[u][h], lse_all)
        lse_ref[rows, :] = lse_all


def _attn_side_plan(qkv, cvec, g, n_steps, step_of):
    _, dil, ls, _ = qkv.shape
    nb = ls // BLOCK
    units = dil * nb
    per_step = min(d for d in range(1, units + 1) if units % d == 0 and d * n_steps >= units)
    static = _SideStatic(per_step=per_step, nb=nb, steps=units // per_step)
    flat = qkv.reshape(3, dil * ls, D_ATTN)
    rows = per_step * BLOCK

    def cur(kind):
        return pl.BlockSpec((None, rows, D_ATTN), lambda *idx: (kind, jnp.minimum(step_of(*idx), static.steps - 1), 0))

    def prev(kind):
        def index(*idx):
            first_unit = jnp.minimum(step_of(*idx), static.steps - 1) * per_step
            return (kind, jnp.maximum(first_unit - 1, 0), 0)
        return pl.BlockSpec((None, BLOCK, D_ATTN), index)

    def out_spec(width):
        return pl.BlockSpec((rows, width), lambda *idx: (jnp.minimum(step_of(*idx), static.steps - 1), 0))

    return _SidePlan(
        static=static,
        in_specs=[pl.BlockSpec((None, N_SLOTS, 2 * BLOCK), lambda *idx: (g, 0, 0)),
                  cur(0), prev(1), cur(1), prev(2), cur(2)],
        args=[cvec, flat, flat, flat, flat, flat],
        out_specs=[out_spec(D_ATTN), out_spec(LANES)],
        out_shape=[jax.ShapeDtypeStruct((dil * ls, D_ATTN), F32), jax.ShapeDtypeStruct((dil * ls, LANES), F32)],
        scratch=[pltpu.VMEM((2, N_SLOTS, BLOCK, 2 * BLOCK), F32),
                 pltpu.VMEM((per_step, N_SLOTS, BLOCK, 2 * BLOCK), F32),
                 pltpu.VMEM((per_step, N_SLOTS, BLOCK, 2 * BLOCK), BF16)],
        finish=lambda o, lse: (o.reshape(dil, ls, D_ATTN), lse.reshape(dil, ls, LANES)),
    )


def _attn_sample_kernel(bias_ref, nbias_ref, q0_ref, q1_ref, q2_ref, kv0_ref, kv1_ref, kv2_ref,
                        c0_ref, c1_ref, c2_ref, *refs, tb, gate_tile):
    if gate_tile is None:
        o_ref, slab_ref, s_ref, p_ref, og_ref = refs
    else:
        xn_ref, w_ref, o_ref, gate_ref, slab_ref, s_ref, p_ref, og_ref = refs
        _gate_columns(xn_ref[...], w_ref, gate_ref, gate_tile == SILU_TILE)
    rows = pl.ds(pl.program_id(0) * tb, tb)
    nt = (((1,), (1,)), ((), ()))
    kv_rows = 2 * N_SLOTS
    slab_rows = N_BACK * kv_rows
    q_refs = (q0_ref, q1_ref, q2_ref)
    kv_refs = (kv0_ref, kv1_ref, kv2_ref)
    c_refs = (c0_ref, c1_ref, c2_ref)

    def rounded(x):
        return x.astype(BF16).astype(F32)

    qs = [q_ref[rows] for q_ref in q_refs]
    for g in range(N_GROUPS):
        for t in range(tb):
            slab_ref[g, t] = c_refs[g][t].reshape(slab_rows, HEAD_DIM).astype(BF16)
            s_ref[g, t] = lax.dot_general(qs[g][t].astype(BF16), slab_ref[g, t], nt, preferred_element_type=F32)
    p_news, ls, lses = [], [], []
    for g in range(N_GROUPS):
        s_n = jnp.sum(rounded(qs[g]) * rounded(kv_refs[g][rows, 0]), axis=-1, keepdims=True)
        sc_c = s_ref[g] * SCALE + bias_ref[g][None]
        sc_n = s_n * SCALE + nbias_ref[g][None]
        mx = jnp.maximum(jnp.max(sc_c, axis=-1, keepdims=True), sc_n)
        p_c = jnp.exp(sc_c - mx)
        p_n = jnp.exp(sc_n - mx)
        l = jnp.sum(p_c, axis=-1, keepdims=True) + p_n
        p_ref[g] = pltpu.roll(p_c, N_SLOTS, 2).astype(BF16)
        p_news.append(p_n)
        ls.append(l)
        lses.append(mx + jnp.log(l))
    for g in range(N_GROUPS):
        for t in range(tb):
            og_ref[g, t] = jnp.dot(p_ref[g, t], slab_ref[g, t], preferred_element_type=F32)
    outs = [(og_ref[g] + rounded(p_news[g]) * rounded(kv_refs[g][rows, 1])) / ls[g] for g in range(N_GROUPS)]
    top = jnp.maximum(jnp.maximum(lses[0], lses[1]), lses[2])
    es = [jnp.exp(x - top) for x in lses]
    tot = es[0] + es[1] + es[2]
    o_ref[rows] = (es[0] / tot) * outs[0] + (es[1] / tot) * outs[1] + (es[2] / tot) * outs[2]


def _attn_sample(qs, kvs, caches, bias, nbias, gate_rider=None):
    db = qs[0].shape[0]
    tb = 4
    kv_rows = 2 * N_SLOTS
    n_steps = db // tb
    rider_in_specs, rider_args, rider_out_specs, rider_out_shape, gate_tile = [], [], [], [], None
    if gate_rider is not None:
        xn, w_in, gate_tile = gate_rider
        rows_per_step = xn.shape[0] // n_steps
        rider_in_specs = [pl.BlockSpec((rows_per_step, D_MODEL), lambda s: (s, 0)),
                          pl.BlockSpec((D_MODEL, GATE_TILE),
                                       lambda s: (0, w_in.block(_gate_weight_block(gate_tile), GATE_TILE)))]
        rider_args = [xn, w_in.arr]
        rider_out_specs = [pl.BlockSpec((rows_per_step, GATE_TILE), lambda s: (s, 0))]
        rider_out_shape = [jax.ShapeDtypeStruct((xn.shape[0], GATE_TILE), F32)]
    views, specs = [], []
    for g, c in enumerate(caches):
        dil = DILATIONS[g]
        lc = c.shape[1]
        views.append(c.reshape(db, lc // dil, dil * kv_rows, HEAD_DIM))
        specs.append(pl.BlockSpec((tb, N_BACK, kv_rows, HEAD_DIM), lambda s: (s, 0, 0, 0)))
    q_spec = pl.BlockSpec((db, N_SLOTS, HEAD_DIM), lambda s: (0, 0, 0))
    kv_spec = pl.BlockSpec((db, 2, N_SLOTS, HEAD_DIM), lambda s: (0, 0, 0, 0))
    outs = pl.pallas_call(
        functools.partial(_attn_sample_kernel, tb=tb, gate_tile=gate_tile),
        grid=(n_steps,),
        in_specs=[pl.BlockSpec((N_GROUPS, N_SLOTS, N_BACK * kv_rows), lambda s: (0, 0, 0)),
                  pl.BlockSpec((N_GROUPS, N_SLOTS, 1), lambda s: (0, 0, 0)),
                  q_spec, q_spec, q_spec, kv_spec, kv_spec, kv_spec] + specs + rider_in_specs,
        out_specs=[pl.BlockSpec((db, N_SLOTS, HEAD_DIM), lambda s: (0, 0, 0))] + rider_out_specs,
        out_shape=[jax.ShapeDtypeStruct((db, N_SLOTS, HEAD_DIM), F32)] + rider_out_shape,
        scratch_shapes=[pltpu.VMEM((N_GROUPS, tb, N_BACK * kv_rows, HEAD_DIM), BF16),
                        pltpu.VMEM((N_GROUPS, tb, N_SLOTS, N_BACK * kv_rows), F32),
                        pltpu.VMEM((N_GROUPS, tb, N_SLOTS, N_BACK * kv_rows), BF16),
                        pltpu.VMEM((N_GROUPS, tb, N_SLOTS, HEAD_DIM), F32)],
        compiler_params=_params(("arbitrary",)),
        name="attn_sample",
    )(bias, nbias, *qs, *kvs, *views, *rider_args)
    return outs[0] if gate_rider is None else outs


def _out_kernel(*refs, tm, n_groups):
    o_refs = refs[:n_groups]
    lse_refs = refs[n_groups:2 * n_groups] if n_groups > 1 else ()
    rest = refs[len(o_refs) + len(lse_refs):]
    def project(a):
        ya = jnp.dot(a, woa_ref[...], preferred_element_type=F32)
        yb = jnp.dot(yb_ref[...], wob_ref[...], preferred_element_type=F32)
        merged = sma_ref[...] * ya + smb_ref[...] * yb
        y_ref[...] = x_ref[...] + jnp.dot(merged.astype(BF16), wo_ref[...], preferred_element_type=F32)

    if n_groups > 1:
        sga_ref, sma_ref, smb_ref, yb_ref, x_ref, woa_ref, wob_ref, wo_ref, y_ref, a2_ref, og_ref, lg_ref = rest
        step = pl.program_id(0)

        @pl.when(step == 0)
        def _():
            a2_ref[1] = jnp.zeros(a2_ref.shape[1:], BF16)

        project(a2_ref[(step + 1) % 2])
        a_ref = a2_ref.at[step % 2]
        for g in range(n_groups):
            dil = DILATIONS[g]
            sub = tm // dil
            for r in range(dil):
                rows = slice(None) if dil == 1 else pl.ds(r, sub, stride=dil)
                lg_ref[g, rows, :] = lse_refs[g][r]
                for h in range(N_SLOTS):
                    og_ref[g, h, rows, :] = o_refs[g][r, :, h * HEAD_DIM:(h + 1) * HEAD_DIM]
        lses = [lg_ref[g] for g in range(n_groups)]
        top = functools.reduce(jnp.maximum, lses)
        es = [jnp.exp(x - top) for x in lses]
        tot = functools.reduce(lambda a, b: a + b, es)
        ws = [e / tot for e in es]
        for h in range(N_SLOTS):
            sl = slice(h * HEAD_DIM, (h + 1) * HEAD_DIM)
            o = functools.reduce(lambda a, b: a + b,
                                 [ws[g][:, h:h + 1] * og_ref[g, h] for g in range(n_groups)])
            a_ref[:, sl] = (o * sga_ref[:, sl]).astype(BF16)
    else:
        sga_ref, sma_ref, smb_ref, yb_ref, x_ref, woa_ref, wob_ref, wo_ref, y_ref = rest
        project((o_refs[0][...] * sga_ref[...]).astype(BF16))


def _out_proj(os_, lses, gates, yb_in, x, w_out_a, w_out_b, w_o):
    rows = x.shape[0]

    tm = min(rows, 256)
    n_groups = len(os_)
    n_blocks = rows // tm
    pipelined = n_groups > 1
    n_steps = n_blocks + 1 if pipelined else n_blocks

    def comb_block(s):
        return jnp.minimum(s, n_blocks - 1)

    def proj_block(s):
        return jnp.maximum(s - 1, 0) if pipelined else s

    def gate_operand(tile, width, block_of):
        arr, tile0 = next((a, t0) for a, t0 in gates if t0 <= tile < t0 + a.shape[1] // GATE_TILE)
        assert (tile - tile0) * GATE_TILE % width == 0
        return arr, pl.BlockSpec((tm, width), lambda s: (block_of(s), (tile - tile0) * GATE_TILE // width))

    def row_spec(width, block_of):
        return pl.BlockSpec((tm, width), lambda s: (block_of(s), 0))

    def sub_spec(dil, width):
        return pl.BlockSpec((dil, tm // dil, width), lambda s: (0, comb_block(s), 0))

    def const_spec(shape):
        return pl.BlockSpec(shape, lambda s: (0, 0), pipeline_mode=pl.Buffered(1))

    if pipelined:
        in_specs = ([sub_spec(DILATIONS[g], D_ATTN) for g in range(n_groups)]
                    + [sub_spec(DILATIONS[g], LANES) for g in range(n_groups)])
        scratch = [pltpu.VMEM((2, tm, D_ATTN), BF16),
                   pltpu.VMEM((n_groups, N_SLOTS, tm, HEAD_DIM), F32), pltpu.VMEM((n_groups, tm, LANES), F32)]
    else:
        in_specs = [row_spec(D_ATTN, comb_block)]
        scratch = []
    sga, sga_spec = gate_operand(SILU_TILE, D_ATTN, comb_block)
    sma, sma_spec = gate_operand(0, D_MODEL, proj_block)
    smb, smb_spec = gate_operand(D_MODEL // GATE_TILE, D_MODEL, proj_block)
    in_specs += [sga_spec, sma_spec, smb_spec, row_spec(D_CONV, proj_block), row_spec(D_MODEL, proj_block),
                 const_spec((D_ATTN, D_MODEL)), const_spec((D_CONV, D_MODEL)), const_spec((D_MODEL, D_MODEL))]
    return pl.pallas_call(
        functools.partial(_out_kernel, tm=tm, n_groups=n_groups),
        grid=(n_steps,),
        in_specs=in_specs,
        out_specs=row_spec(D_MODEL, proj_block),
        out_shape=jax.ShapeDtypeStruct((rows, D_MODEL), F32),
        scratch_shapes=scratch,
        compiler_params=_params(("arbitrary",)),
        name="out_proj",
    )(*os_, *lses, sga, sma, smb, yb_in, x, w_out_a, w_out_b, w_o)


def kernel(x_prompt, x_sample, cache_kv_w128, cache_kv_w512, cache_kv_w2048, state_conv, norm_gain, w_in,
           q_norm_gain, k_norm_gain, rel_bias, conv_w, w_out_a, w_out_b, w_o):
    seq = x_prompt.shape[1]
    db = x_sample.shape[0]
    xp = x_prompt.reshape(seq, D_MODEL)
    xs = x_sample.reshape(db, D_MODEL)
    caches = (cache_kv_w128, cache_kv_w512, cache_kv_w2048)
    qk_gains = jnp.stack([jnp.tile(q_norm_gain, N_SLOTS), jnp.tile(k_norm_gain, N_SLOTS)]).reshape(2, 1, D_ATTN)
    cvec = _bias_by_offset(rel_bias)

    xn = _pre_norm(xp, norm_gain)
    qkv0, kv_p0, w_qkv = _qkv_proj(xn, _Cols(w_in, 0), qk_gains, 0, DILATIONS[0], BF16, min(WINDOWS[0], seq),
                                   casts=((w_in, 0, COL_GATE_A),))
    w_qkv = _Cols(w_qkv, 0)
    qkv1, kv_p1, w_rest, woa, wob, wo = _qkv_proj(
        xn, w_qkv, qk_gains, 1, DILATIONS[1], BF16, min(WINDOWS[1], seq),
        casts=((w_in, COL_GATE_A, w_in.shape[1] - COL_GATE_A),
               (w_out_a, 0, D_MODEL), (w_out_b, 0, D_MODEL), (w_o, 0, D_MODEL)))
    w_rest = _Cols(w_rest, COL_GATE_A)
    qkv2, kv_p2 = _qkv_proj(xn, w_qkv, qk_gains, 2, DILATIONS[2], BF16, min(WINDOWS[2], seq))
    qkvs, kv_p = (qkv0, qkv1, qkv2), (kv_p0, kv_p1, kv_p2)
    xn_s = _pre_norm(xs, norm_gain)
    qkvs_s, kv_s = zip(*[_qkv_proj(xn_s, w_qkv, qk_gains, g, 1, F32, db) for g in range(N_GROUPS)])
    yb_in_s, h_s = _branch_b(xn_s, w_rest, conv_w, state=state_conv)
    gates_s = [(_gates(xn_s, w_rest, 0, N_GATE_TILES), 0)]
    qs_s = [qkv[0, 0].reshape(db, N_SLOTS, HEAD_DIM) for qkv in qkvs_s]
    o_s, gates_c = _attn_sample(qs_s, kv_s, caches, *_sample_bias(cvec), gate_rider=(xn, w_rest, SILU_TILE))
    y_sample = _out_proj((o_s.reshape(db, D_ATTN),), (), gates_s, yb_in_s, xs, woa, wob, wo).reshape(db, 1, D_MODEL)

    gates_a, attn0 = _gates(xn, w_rest, 0, 2, attn=(qkvs[0], cvec, 0))
    gates_b, attn1 = _gates(xn, w_rest, 2, 2, attn=(qkvs[1], cvec, 1))
    yb_in, h_last, attn2 = _branch_b(xn, w_rest, conv_w, attn=(qkvs[2], cvec, 2))
    gates = [(gates_a, 0), (gates_b, 2), (gates_c, SILU_TILE)]
    os_, lses = zip(attn0, attn1, attn2)
    y_prompt = _out_proj(os_, lses, gates, yb_in, xp, woa, wob, wo).reshape(1, seq, D_MODEL)
    conv_p = h_last[SUBLANES - (CONV_W - 1):][None]
    conv_s = jnp.stack([state_conv[:, CONV_W - 2], h_s], axis=1)

    return (y_prompt, y_sample, kv_p[0][None], kv_p[1][None], kv_p[2][None], conv_p,
            kv_s[0][:, None], kv_s[1][:, None], kv_s[2][:, None], conv_s)
```

```python
import functools
import math
from typing import Callable, NamedTuple

import numpy as np
import jax
import jax.numpy as jnp
from jax import lax
from jax.experimental import pallas as pl
from jax.experimental.pallas import tpu as pltpu

D_MODEL = 2048
N_GROUPS = 3
DILATIONS = (1, 4, 16)
N_BACK = 128
WINDOWS = (128, 512, 2048)
N_SLOTS = 8
HEAD_DIM = 128
D_ATTN = N_SLOTS * HEAD_DIM
QKV_COLS = N_GROUPS * D_ATTN
D_CONV = D_MODEL // 2
CONV_W = 3
BLOCK = N_BACK
NUM_BUCKETS = 32
MAX_EXACT = NUM_BUCKETS // 2
MAX_DISTANCE = 2048
EPS = 1e-6
SCALE = HEAD_DIM ** -0.5

COL_K = QKV_COLS
COL_V = 2 * QKV_COLS
COL_GATE_A = 3 * QKV_COLS
COL_B = COL_GATE_A + D_ATTN
COL_C = COL_B + D_CONV
COL_XIN = COL_C + D_CONV
COL_GATE_B = COL_XIN + D_CONV
COL_MERGE_A = COL_GATE_B + D_CONV
COL_MERGE_B = COL_MERGE_A + D_MODEL

SUBLANES = 8
LANES = 128
MXU_COLS = 256
VMEM_LIMIT_BYTES = 56 * 1024 * 1024

BF16 = jnp.bfloat16
F32 = jnp.float32
NEG_INF = float("-inf")


def _params(semantics):
    return pltpu.CompilerParams(dimension_semantics=semantics, vmem_limit_bytes=VMEM_LIMIT_BYTES)


def _norm_kernel(x_ref, g_ref, o_ref):
    x = x_ref[...]
    r = lax.rsqrt(jnp.mean(x * x, axis=-1, keepdims=True) + EPS)
    o_ref[...] = ((x * r) * g_ref[...]).astype(o_ref.dtype)


def _pre_norm(x, gain):
    rows = x.shape[0]
    tm = min(rows, 512)
    return pl.pallas_call(
        _norm_kernel,
        grid=(rows // tm,),
        in_specs=[pl.BlockSpec((tm, D_MODEL), lambda m: (m, 0)),
                  pl.BlockSpec((1, D_MODEL), lambda m: (0, 0))],
        out_specs=pl.BlockSpec((tm, D_MODEL), lambda m: (m, 0)),
        out_shape=jax.ShapeDtypeStruct((rows, D_MODEL), BF16),
        compiler_params=_params(("arbitrary",)),
        name="pre_norm",
    )(x, gain.reshape(1, D_MODEL))


ROW_STRIDE = 4


def _qkv_kernel(xn_ref, w_ref, gain_ref, *refs, dil, tm, tr, m_tail0, n_casts, w_is_f32):
    two_stage = dil > ROW_STRIDE
    n_scratch = 2 if two_stage else 1
    cast_in, (o_ref, tail_ref), cast_out, own_scratch, w_scratch = (
        refs[:n_casts], refs[n_casts:n_casts + 2], refs[n_casts + 2:2 * n_casts + 2],
        refs[2 * n_casts + 2:2 * n_casts + 2 + n_scratch], refs[2 * n_casts + 2 + n_scratch:])
    res_ref = own_scratch[0]
    kind = pl.program_id(0)
    m = pl.program_id(1)
    if w_is_f32:
        w_f32_ref, (w_ref,) = w_ref, w_scratch

        @pl.when(m == 0)
        def _():
            w_ref[...] = w_f32_ref[...].astype(BF16)

    normed = kind < 2
    xn = xn_ref[...]
    sub = tm // dil
    part = tm // ROW_STRIDE
    heads_per_chunk = MXU_COLS // HEAD_DIM
    n_chunks = D_ATTN // MXU_COLS
    for j in range(n_chunks):
        p = jnp.dot(xn, w_ref[:, j * MXU_COLS:(j + 1) * MXU_COLS], preferred_element_type=F32)
        for src_ref, dst_ref in zip(cast_in, cast_out):
            share = src_ref.shape[0] // n_chunks
            dst_ref[j * share:(j + 1) * share, :] = src_ref[j * share:(j + 1) * share, :].astype(BF16)
        for hh in range(heads_per_chunk):
            h = j * heads_per_chunk + hh
            sl = slice(h * HEAD_DIM, (h + 1) * HEAD_DIM)
            ph = p[:, hh * HEAD_DIM:(hh + 1) * HEAD_DIM]
            r = lax.rsqrt(jnp.mean(ph * ph, axis=-1, keepdims=True) + EPS)
            res = (ph * jnp.where(normed, r, 1.0)) * jnp.where(normed, gain_ref[:, sl], 1.0)
            res_ref[h] = res
            if two_stage:
                stage_ref = own_scratch[1].at[h % 2]
                for r1 in range(ROW_STRIDE):
                    stage_ref[r1 * part:(r1 + 1) * part, :] = res_ref[h, pl.ds(r1, part, stride=ROW_STRIDE), :]
            for rr in range(dil):
                if dil == 1:
                    rows = res
                elif two_stage:
                    r1, r2 = rr % ROW_STRIDE, rr // ROW_STRIDE
                    rows = stage_ref[pl.ds(r1 * part + r2, sub, stride=ROW_STRIDE), :]
                else:
                    rows = res_ref[h, pl.ds(rr, sub, stride=dil), :]
                o_ref[rr, :, sl] = rows.astype(o_ref.dtype)

    @pl.when(jnp.logical_and(kind >= 1, m >= m_tail0))
    def _():
        for h in range(N_SLOTS):
            tail_ref[:, h, :] = res_ref[h, tm - tr:tm, :]


class _Cols(NamedTuple):
    arr: jax.Array
    col0: int

    def block(self, col_block, width):
        return col_block - self.col0 // width


def _qkv_proj(xn, w_in, qk_gains, g, dil, out_dtype, tail_rows, casts=()):
    rows = xn.shape[0]
    tm = min(rows, 1024)
    tn = D_ATTN
    tr = min(tail_rows, tm)
    m_tail0 = rows // tm - tail_rows // tr
    grid = (3, rows // tm)
    n_steps = grid[0] * grid[1]
    w_is_f32 = w_in.arr.dtype == F32

    def tail_map(kind, m):
        return (jnp.where(kind >= 1, jnp.maximum(m - m_tail0, 0), 0), jnp.maximum(kind - 1, 0), 0, 0)

    def cast_specs_of(w, col0, n_cols):
        n_blocks = max(d for d in range(1, n_steps + 1) if n_cols % (d * LANES) == 0)
        width = n_cols // n_blocks
        assert col0 % width == 0

        def block(kind, m):
            return jnp.minimum(kind * grid[1] + m, n_blocks - 1)

        return (pl.BlockSpec((w.shape[0], width), lambda kind, m: (0, col0 // width + block(kind, m))),
                pl.BlockSpec((w.shape[0], width), lambda kind, m: (0, block(kind, m))))

    cast_in_specs, cast_out_specs = zip(*[cast_specs_of(*c) for c in casts]) if casts else ((), ())
    assert dil <= ROW_STRIDE or dil == ROW_STRIDE * ROW_STRIDE
    scratch = [pltpu.VMEM((N_SLOTS, tm, HEAD_DIM), F32)]
    if dil > ROW_STRIDE:
        scratch.append(pltpu.VMEM((2, tm, HEAD_DIM), F32))
    if w_is_f32:
        scratch.append(pltpu.VMEM((D_MODEL, tn), BF16))
    return pl.pallas_call(
        functools.partial(_qkv_kernel, dil=dil, tm=tm, tr=tr, m_tail0=m_tail0, n_casts=len(casts), w_is_f32=w_is_f32),
        grid=grid,
        in_specs=[pl.BlockSpec((tm, D_MODEL), lambda kind, m: (m, 0)),
                  pl.BlockSpec((D_MODEL, tn), lambda kind, m: (0, w_in.block(kind * N_GROUPS + g, tn))),
                  pl.BlockSpec((None, 1, tn), lambda kind, m: (jnp.minimum(kind, 1), 0, 0))] + list(cast_in_specs),
        out_specs=[pl.BlockSpec((None, dil, tm // dil, tn), lambda kind, m: (kind, 0, m, 0)),
                   pl.BlockSpec((tr, None, N_SLOTS, HEAD_DIM), tail_map)] + list(cast_out_specs),
        out_shape=[jax.ShapeDtypeStruct((3, dil, rows // dil, tn), out_dtype),
                   jax.ShapeDtypeStruct((tail_rows, 2, N_SLOTS, HEAD_DIM), F32)]
                  + [jax.ShapeDtypeStruct((w.shape[0], n_cols), BF16) for w, _, n_cols in casts],
        scratch_shapes=scratch,
        compiler_params=_params(("arbitrary", "arbitrary")),
        name=f"qkv_proj_g{g}",
    )(xn, w_in.arr, qk_gains, *[w for w, _, _ in casts])


def _branch_b_kernel(*refs, tm, per_row_state, side):
    if per_row_state:
        xn_ref, wb_ref, wc_ref, wx_ref, wg_ref, cw_ref, hm2_ref, hm1_ref, y_ref, h_ref = refs
    else:
        (xn_ref, wb_ref, wc_ref, wx_ref, wg_ref, cw_ref), refs = refs[:6], refs[6:]
        if side is not None:
            side_in, (y_ref, h_ref), side_out, (hbuf_ref,), side_scratch = (
                refs[:6], refs[6:8], refs[8:10], refs[10:11], refs[11:])
        else:
            y_ref, h_ref, hbuf_ref = refs

        @pl.when(pl.program_id(1) == 0)
        def _():
            hbuf_ref[0:SUBLANES, :] = jnp.zeros((SUBLANES, hbuf_ref.shape[1]), F32)

        if side is not None:
            _attn_side(side, pl.program_id(0) * pl.num_programs(1) + pl.program_id(1),
                       *side_in, *side_out, *side_scratch)

    xn = xn_ref[...]
    pb = jnp.dot(xn, wb_ref[...], preferred_element_type=F32)
    pc = jnp.dot(xn, wc_ref[...], preferred_element_type=F32)
    px = jnp.dot(xn, wx_ref[...], preferred_element_type=F32)
    pg = jnp.dot(xn, wg_ref[...], preferred_element_type=F32)
    h = pc * px
    if per_row_state:
        hm2 = hm2_ref[...]
        hm1 = hm1_ref[...]
        h_ref[...] = h
    else:
        hbuf_ref[SUBLANES:SUBLANES + tm, :] = h
        hm2 = hbuf_ref[pl.ds(SUBLANES - 2, tm), :]
        hm1 = hbuf_ref[pl.ds(SUBLANES - 1, tm), :]
        last = hbuf_ref[pl.ds(tm, SUBLANES), :]
        h_ref[...] = last
        hbuf_ref[0:SUBLANES, :] = last
    z = cw_ref[0:1, :] * hm2 + cw_ref[1:2, :] * hm1 + cw_ref[2:3, :] * h
    y_ref[...] = (pb * z * (pg * jax.nn.sigmoid(pg))).astype(y_ref.dtype)


def _branch_b(xn, w_in, conv_w, state=None, attn=None):
    rows = xn.shape[0]
    tm = min(rows, 1024)
    tnb = 256
    per_row_state = state is not None
    grid = (D_CONV // tnb, rows // tm)
    side = None if attn is None else _attn_side_plan(*attn, n_steps=grid[0] * grid[1],
                                                     step_of=lambda j, m: j * grid[1] + m)

    def wspec(col0):
        return pl.BlockSpec((D_MODEL, tnb), lambda j, m: (0, w_in.block(col0 // tnb + j, tnb)))

    in_specs = [pl.BlockSpec((tm, D_MODEL), lambda j, m: (m, 0)),
                wspec(COL_B), wspec(COL_C), wspec(COL_XIN), wspec(COL_GATE_B),
                pl.BlockSpec((CONV_W, tnb), lambda j, m: (0, j))]
    args = [xn, w_in.arr, w_in.arr, w_in.arr, w_in.arr, conv_w]
    scratch = []
    if per_row_state:
        st = state.reshape(rows, (CONV_W - 1) * D_CONV)
        in_specs += [pl.BlockSpec((tm, tnb), lambda j, m: (m, j)),
                     pl.BlockSpec((tm, tnb), lambda j, m: (m, D_CONV // tnb + j))]
        args += [st, st]
        h_spec = pl.BlockSpec((tm, tnb), lambda j, m: (m, j))
        h_shape = jax.ShapeDtypeStruct((rows, D_CONV), F32)
    else:
        scratch.append(pltpu.VMEM((tm + SUBLANES, tnb), F32))
        h_spec = pl.BlockSpec((SUBLANES, tnb), lambda j, m: (0, j))
        h_shape = jax.ShapeDtypeStruct((SUBLANES, D_CONV), F32)
    out_specs = [pl.BlockSpec((tm, tnb), lambda j, m: (m, j)), h_spec]
    out_shape = [jax.ShapeDtypeStruct((rows, D_CONV), BF16), h_shape]
    if side is not None:
        in_specs += side.in_specs
        args += side.args
        out_specs += side.out_specs
        out_shape += side.out_shape
        scratch += side.scratch
    outs = pl.pallas_call(
        functools.partial(_branch_b_kernel, tm=tm, per_row_state=per_row_state,
                          side=None if side is None else side.static),
        grid=grid,
        in_specs=in_specs,
        out_specs=out_specs,
        out_shape=out_shape,
        scratch_shapes=scratch,
        compiler_params=_params(("arbitrary", "arbitrary")),
        name="branch_b",
    )(*args)
    return outs if side is None else (outs[0], outs[1], side.finish(outs[2], outs[3]))


GATE_TILE = 1024
N_GATE_TILES = (2 * D_MODEL + D_ATTN) // GATE_TILE
SILU_TILE = N_GATE_TILES - 1


def _gate_columns(xn, w_ref, o_ref, is_silu):
    for j in range(w_ref.shape[1] // MXU_COLS):
        cs = slice(j * MXU_COLS, (j + 1) * MXU_COLS)
        p = jnp.dot(xn, w_ref[:, cs], preferred_element_type=F32)
        o_ref[:, cs] = jax.nn.sigmoid(p) * jnp.where(is_silu, p, 1.0)


def _gate_weight_block(tile):
    return jnp.where(tile == SILU_TILE, COL_GATE_A // GATE_TILE, COL_MERGE_A // GATE_TILE + tile)


def _gates_kernel(xn_ref, w_ref, *refs, tile0, side):
    if side is not None:
        side_in, o_ref, side_out, side_scratch = refs[:6], refs[6], refs[7:9], refs[9:]
        _attn_side(side, pl.program_id(0) * pl.num_programs(1) + pl.program_id(1),
                   *side_in, *side_out, *side_scratch)
    else:
        o_ref, = refs
    _gate_columns(xn_ref[...], w_ref, o_ref, tile0 + pl.program_id(0) == SILU_TILE)


def _gates(xn, w_in, tile0, n_tiles, attn=None):
    rows = xn.shape[0]
    tm = min(rows, 1024)
    tn = GATE_TILE
    grid = (n_tiles, rows // tm)
    side = None if attn is None else _attn_side_plan(*attn, n_steps=grid[0] * grid[1],
                                                     step_of=lambda n, m: n * grid[1] + m)

    def wmap(n, m):
        return (0, w_in.block(_gate_weight_block(tile0 + n), tn))

    in_specs = [pl.BlockSpec((tm, D_MODEL), lambda n, m: (m, 0)), pl.BlockSpec((D_MODEL, tn), wmap)]
    args = [xn, w_in.arr]
    out_specs = [pl.BlockSpec((tm, tn), lambda n, m: (m, n))]
    out_shape = [jax.ShapeDtypeStruct((rows, n_tiles * tn), F32)]
    scratch = []
    if side is not None:
        in_specs += side.in_specs
        args += side.args
        out_specs += side.out_specs
        out_shape += side.out_shape
        scratch += side.scratch
    outs = pl.pallas_call(
        functools.partial(_gates_kernel, tile0=tile0, side=None if side is None else side.static),
        grid=grid,
        in_specs=in_specs,
        out_specs=out_specs,
        out_shape=out_shape,
        scratch_shapes=scratch,
        compiler_params=_params(("arbitrary", "arbitrary")),
        name="gates",
    )(*args)
    return outs[0] if side is None else (outs[0], side.finish(outs[1], outs[2]))


def _t5_bucket(dist):
    d = np.maximum(dist, 1).astype(np.float32)
    large = MAX_EXACT + (np.log(d / np.float32(MAX_EXACT)) / np.float32(math.log(MAX_DISTANCE / MAX_EXACT))
                         * np.float32(NUM_BUCKETS - MAX_EXACT)).astype(np.int32)
    large = np.minimum(large, NUM_BUCKETS - 1)
    return np.where(dist < MAX_EXACT, dist, large)


def _bias_by_offset(rel_bias):
    steps = N_BACK - np.arange(N_BACK + 1)
    idx = np.stack([_t5_bucket(d * steps) for d in DILATIONS])
    g_idx = np.arange(N_GROUPS)[:, None]
    vals = jnp.transpose(rel_bias[idx, g_idx], (0, 2, 1)).astype(F32)
    return jnp.pad(vals, ((0, 0), (0, 0), (0, 2 * BLOCK - N_BACK - 1)), constant_values=NEG_INF)


def _sample_bias(cvec):
    kv_head = np.arange(N_BACK * 2 * N_SLOTS) % (2 * N_SLOTS)
    own_k_row = kv_head[None, :] == np.arange(N_SLOTS)[:, None]
    per_row = jnp.repeat(cvec[:, :, :N_BACK], 2 * N_SLOTS, axis=2)
    return jnp.where(own_k_row[None], per_row, NEG_INF), cvec[:, :, N_BACK:N_BACK + 1]


class _SideStatic(NamedTuple):
    per_step: int
    nb: int
    steps: int


class _SidePlan(NamedTuple):
    static: _SideStatic
    in_specs: list
    args: list
    out_specs: list
    out_shape: list
    scratch: list
    finish: Callable


def _attn_side(side, step, cvec_ref, q_ref, kp_ref, kc_ref, vp_ref, vc_ref, o_ref, lse_ref, bias_ref, s_ref, p_ref):
    @pl.when(step == 0)
    def _():
        for h in range(N_SLOTS):
            row = jnp.broadcast_to(cvec_ref[h:h + 1, :], (BLOCK, 2 * BLOCK))
            full = pltpu.roll(row, 0, 1, stride=1, stride_axis=0)
            bias_ref[1, h] = full
            bias_ref[0, h, :, :BLOCK] = jnp.full((BLOCK, BLOCK), NEG_INF, F32)
            bias_ref[0, h, :, BLOCK:] = full[:, BLOCK:]

    nt = (((1,), (1,)), ((), ()))
    heads = [slice(h * HEAD_DIM, (h + 1) * HEAD_DIM) for h in range(N_SLOTS)]
    unit0 = jnp.minimum(step, side.steps - 1) * side.per_step
    lane = lax.broadcasted_iota(jnp.int32, (BLOCK, LANES), 1)
    for u in range(side.per_step):
        rows = slice(u * BLOCK, (u + 1) * BLOCK)
        prev = slice((u - 1) * BLOCK, u * BLOCK)
        for h, sl in enumerate(heads):
            q = q_ref[rows, sl]
            k_prev = kp_ref[:, sl] if u == 0 else kc_ref[prev, sl]
            s_ref[u, h, :, :BLOCK] = lax.dot_general(q, k_prev, nt, preferred_element_type=F32)
            s_ref[u, h, :, BLOCK:] = lax.dot_general(q, kc_ref[rows, sl], nt, preferred_element_type=F32)
    ls, lses = [], []
    for u in range(side.per_step):
        has_prev = jnp.minimum((unit0 + u) % side.nb, 1)
        sc = s_ref[u] * SCALE + bias_ref[has_prev]
        mx = jnp.max(jnp.maximum(sc[:, :, :BLOCK], sc[:, :, BLOCK:]), axis=-1, keepdims=True)
        p = jnp.exp(sc - mx)
        l = jnp.sum(p[:, :, :BLOCK] + p[:, :, BLOCK:], axis=-1, keepdims=True)
        p_ref[u] = p.astype(BF16)
        ls.append(l)
        lses.append(mx + jnp.log(l))
    for u in range(side.per_step):
        rows = slice(u * BLOCK, (u + 1) * BLOCK)
        prev = slice((u - 1) * BLOCK, u * BLOCK)
        lse_all = jnp.zeros((BLOCK, LANES), F32)
        for h, sl in enumerate(heads):
            v_prev = vp_ref[:, sl] if u == 0 else vc_ref[prev, sl]
            o = (jnp.dot(p_ref[u, h, :, :BLOCK], v_prev, preferred_element_type=F32)
                 + jnp.dot(p_ref[u, h, :, BLOCK:], vc_ref[rows, sl], preferred_element_type=F32))
            o_ref[rows, sl] = o / ls[u][h]
            lse_all = jnp.where(lane == h, lses[u][h], lse_all)
        lse_ref[rows, :] = lse_all


def _attn_side_plan(qkv, cvec, g, n_steps, step_of):
    _, dil, ls, _ = qkv.shape
    nb = ls // BLOCK
    units = dil * nb
    per_step = min(d for d in range(1, units + 1) if units % d == 0 and d * n_steps >= units)
    static = _SideStatic(per_step=per_step, nb=nb, steps=units // per_step)
    flat = qkv.reshape(3, dil * ls, D_ATTN)
    rows = per_step * BLOCK

    def cur(kind):
        return pl.BlockSpec((None, rows, D_ATTN), lambda *idx: (kind, jnp.minimum(step_of(*idx), static.steps - 1), 0))

    def prev(kind):
        def index(*idx):
            first_unit = jnp.minimum(step_of(*idx), static.steps - 1) * per_step
            return (kind, jnp.maximum(first_unit - 1, 0), 0)
        return pl.BlockSpec((None, BLOCK, D_ATTN), index)

    def out_spec(width):
        return pl.BlockSpec((rows, width), lambda *idx: (jnp.minimum(step_of(*idx), static.steps - 1), 0))

    return _SidePlan(
        static=static,
        in_specs=[pl.BlockSpec((None, N_SLOTS, 2 * BLOCK), lambda *idx: (g, 0, 0)),
                  cur(0), prev(1), cur(1), prev(2), cur(2)],
        args=[cvec, flat, flat, flat, flat, flat],
        out_specs=[out_spec(D_ATTN), out_spec(LANES)],
        out_shape=[jax.ShapeDtypeStruct((dil * ls, D_ATTN), F32), jax.ShapeDtypeStruct((dil * ls, LANES), F32)],
        scratch=[pltpu.VMEM((2, N_SLOTS, BLOCK, 2 * BLOCK), F32),
                 pltpu.VMEM((per_step, N_SLOTS, BLOCK, 2 * BLOCK), F32),
                 pltpu.VMEM((per_step, N_SLOTS, BLOCK, 2 * BLOCK), BF16)],
        finish=lambda o, lse: (o.reshape(dil, ls, D_ATTN), lse.reshape(dil, ls, LANES)),
    )


def _attn_sample_kernel(bias_ref, nbias_ref, q0_ref, q1_ref, q2_ref, kv0_ref, kv1_ref, kv2_ref,
                        c0_ref, c1_ref, c2_ref, *refs, tb, gate_tile):
    if gate_tile is None:
        o_ref, slab_ref, s_ref, p_ref, og_ref = refs
    else:
        xn_ref, w_ref, o_ref, gate_ref, slab_ref, s_ref, p_ref, og_ref = refs
        _gate_columns(xn_ref[...], w_ref, gate_ref, gate_tile == SILU_TILE)
    rows = pl.ds(pl.program_id(0) * tb, tb)
    nt = (((1,), (1,)), ((), ()))
    kv_rows = 2 * N_SLOTS
    slab_rows = N_BACK * kv_rows
    q_refs = (q0_ref, q1_ref, q2_ref)
    kv_refs = (kv0_ref, kv1_ref, kv2_ref)
    c_refs = (c0_ref, c1_ref, c2_ref)

    def rounded(x):
        return x.astype(BF16).astype(F32)

    qs = [q_ref[rows] for q_ref in q_refs]
    for g in range(N_GROUPS):
        for t in range(tb):
            slab_ref[g, t] = c_refs[g][t].reshape(slab_rows, HEAD_DIM).astype(BF16)
            s_ref[g, t] = lax.dot_general(qs[g][t].astype(BF16), slab_ref[g, t], nt, preferred_element_type=F32)
    p_news, ls, lses = [], [], []
    for g in range(N_GROUPS):
        s_n = jnp.sum(rounded(qs[g]) * rounded(kv_refs[g][rows, 0]), axis=-1, keepdims=True)
        sc_c = s_ref[g] * SCALE + bias_ref[g][None]
        sc_n = s_n * SCALE + nbias_ref[g][None]
        mx = jnp.maximum(jnp.max(sc_c, axis=-1, keepdims=True), sc_n)
        p_c = jnp.exp(sc_c - mx)
        p_n = jnp.exp(sc_n - mx)
        l = jnp.sum(p_c, axis=-1, keepdims=True) + p_n
        p_ref[g] = pltpu.roll(p_c, N_SLOTS, 2).astype(BF16)
        p_news.append(p_n)
        ls.append(l)
        lses.append(mx + jnp.log(l))
    for g in range(N_GROUPS):
        for t in range(tb):
            og_ref[g, t] = jnp.dot(p_ref[g, t], slab_ref[g, t], preferred_element_type=F32)
    outs = [(og_ref[g] + rounded(p_news[g]) * rounded(kv_refs[g][rows, 1])) / ls[g] for g in range(N_GROUPS)]
    top = jnp.maximum(jnp.maximum(lses[0], lses[1]), lses[2])
    es = [jnp.exp(x - top) for x in lses]
    tot = es[0] + es[1] + es[2]
    o_ref[rows] = (es[0] / tot) * outs[0] + (es[1] / tot) * outs[1] + (es[2] / tot) * outs[2]


def _attn_sample(qs, kvs, caches, bias, nbias, gate_rider=None):
    db = qs[0].shape[0]
    tb = 4
    kv_rows = 2 * N_SLOTS
    n_steps = db // tb
    rider_in_specs, rider_args, rider_out_specs, rider_out_shape, gate_tile = [], [], [], [], None
    if gate_rider is not None:
        xn, w_in, gate_tile = gate_rider
        rows_per_step = xn.shape[0] // n_steps
        rider_in_specs = [pl.BlockSpec((rows_per_step, D_MODEL), lambda s: (s, 0)),
                          pl.BlockSpec((D_MODEL, GATE_TILE),
                                       lambda s: (0, w_in.block(_gate_weight_block(gate_tile), GATE_TILE)))]
        rider_args = [xn, w_in.arr]
        rider_out_specs = [pl.BlockSpec((rows_per_step, GATE_TILE), lambda s: (s, 0))]
        rider_out_shape = [jax.ShapeDtypeStruct((xn.shape[0], GATE_TILE), F32)]
    views, specs = [], []
    for g, c in enumerate(caches):
        dil = DILATIONS[g]
        lc = c.shape[1]
        views.append(c.reshape(db, lc // dil, dil * kv_rows, HEAD_DIM))
        specs.append(pl.BlockSpec((tb, N_BACK, kv_rows, HEAD_DIM), lambda s: (s, 0, 0, 0)))
    q_spec = pl.BlockSpec((db, N_SLOTS, HEAD_DIM), lambda s: (0, 0, 0))
    kv_spec = pl.BlockSpec((db, 2, N_SLOTS, HEAD_DIM), lambda s: (0, 0, 0, 0))
    outs = pl.pallas_call(
        functools.partial(_attn_sample_kernel, tb=tb, gate_tile=gate_tile),
        grid=(n_steps,),
        in_specs=[pl.BlockSpec((N_GROUPS, N_SLOTS, N_BACK * kv_rows), lambda s: (0, 0, 0)),
                  pl.BlockSpec((N_GROUPS, N_SLOTS, 1), lambda s: (0, 0, 0)),
                  q_spec, q_spec, q_spec, kv_spec, kv_spec, kv_spec] + specs + rider_in_specs,
        out_specs=[pl.BlockSpec((db, N_SLOTS, HEAD_DIM), lambda s: (0, 0, 0))] + rider_out_specs,
        out_shape=[jax.ShapeDtypeStruct((db, N_SLOTS, HEAD_DIM), F32)] + rider_out_shape,
        scratch_shapes=[pltpu.VMEM((N_GROUPS, tb, N_BACK * kv_rows, HEAD_DIM), BF16),
                        pltpu.VMEM((N_GROUPS, tb, N_SLOTS, N_BACK * kv_rows), F32),
                        pltpu.VMEM((N_GROUPS, tb, N_SLOTS, N_BACK * kv_rows), BF16),
                        pltpu.VMEM((N_GROUPS, tb, N_SLOTS, HEAD_DIM), F32)],
        compiler_params=_params(("arbitrary",)),
        name="attn_sample",
    )(bias, nbias, *qs, *kvs, *views, *rider_args)
    return outs[0] if gate_rider is None else outs


def _out_kernel(*refs, tm, n_groups):
    o_refs = refs[:n_groups]
    lse_refs = refs[n_groups:2 * n_groups] if n_groups > 1 else ()
    rest = refs[len(o_refs) + len(lse_refs):]
    def project(a):
        ya = jnp.dot(a, woa_ref[...], preferred_element_type=F32)
        yb = jnp.dot(yb_ref[...], wob_ref[...], preferred_element_type=F32)
        merged = sma_ref[...] * ya + smb_ref[...] * yb
        y_ref[...] = x_ref[...] + jnp.dot(merged.astype(BF16), wo_ref[...], preferred_element_type=F32)

    if n_groups > 1:
        sga_ref, sma_ref, smb_ref, yb_ref, x_ref, woa_ref, wob_ref, wo_ref, y_ref, a2_ref, og_ref, lg_ref = rest
        step = pl.program_id(0)

        @pl.when(step == 0)
        def _():
            a2_ref[1] = jnp.zeros(a2_ref.shape[1:], BF16)

        project(a2_ref[(step + 1) % 2])
        a_ref = a2_ref.at[step % 2]
        for g in range(n_groups):
            dil = DILATIONS[g]
            sub = tm // dil
            for r in range(dil):
                rows = slice(None) if dil == 1 else pl.ds(r, sub, stride=dil)
                lg_ref[g, rows, :] = lse_refs[g][r]
                for h in range(N_SLOTS):
                    og_ref[g, h, rows, :] = o_refs[g][r, :, h * HEAD_DIM:(h + 1) * HEAD_DIM]
        lses = [lg_ref[g] for g in range(n_groups)]
        top = functools.reduce(jnp.maximum, lses)
        es = [jnp.exp(x - top) for x in lses]
        tot = functools.reduce(lambda a, b: a + b, es)
        ws = [e / tot for e in es]
        for h in range(N_SLOTS):
            sl = slice(h * HEAD_DIM, (h + 1) * HEAD_DIM)
            o = functools.reduce(lambda a, b: a + b,
                                 [ws[g][:, h:h + 1] * og_ref[g, h] for g in range(n_groups)])
            a_ref[:, sl] = (o * sga_ref[:, sl]).astype(BF16)
    else:
        sga_ref, sma_ref, smb_ref, yb_ref, x_ref, woa_ref, wob_ref, wo_ref, y_ref = rest
        project((o_refs[0][...] * sga_ref[...]).astype(BF16))


def _out_proj(os_, lses, gates, yb_in, x, w_out_a, w_out_b, w_o):
    rows = x.shape[0]

    tm = min(rows, 256)
    n_groups = len(os_)
    n_blocks = rows // tm
    pipelined = n_groups > 1
    n_steps = n_blocks + 1 if pipelined else n_blocks

    def comb_block(s):
        return jnp.minimum(s, n_blocks - 1)

    def proj_block(s):
        return jnp.maximum(s - 1, 0) if pipelined else s

    def gate_operand(tile, width, block_of):
        arr, tile0 = next((a, t0) for a, t0 in gates if t0 <= tile < t0 + a.shape[1] // GATE_TILE)
        assert (tile - tile0) * GATE_TILE % width == 0
        return arr, pl.BlockSpec((tm, width), lambda s: (block_of(s), (tile - tile0) * GATE_TILE // width))

    def row_spec(width, block_of):
        return pl.BlockSpec((tm, width), lambda s: (block_of(s), 0))

    def sub_spec(dil, width):
        return pl.BlockSpec((dil, tm // dil, width), lambda s: (0, comb_block(s), 0))

    def const_spec(shape):
        return pl.BlockSpec(shape, lambda s: (0, 0), pipeline_mode=pl.Buffered(1))

    if pipelined:
        in_specs = ([sub_spec(DILATIONS[g], D_ATTN) for g in range(n_groups)]
                    + [sub_spec(DILATIONS[g], LANES) for g in range(n_groups)])
        scratch = [pltpu.VMEM((2, tm, D_ATTN), BF16),
                   pltpu.VMEM((n_groups, N_SLOTS, tm, HEAD_DIM), F32), pltpu.VMEM((n_groups, tm, LANES), F32)]
    else:
        in_specs = [row_spec(D_ATTN, comb_block)]
        scratch = []
    sga, sga_spec = gate_operand(SILU_TILE, D_ATTN, comb_block)
    sma, sma_spec = gate_operand(0, D_MODEL, proj_block)
    smb, smb_spec = gate_operand(D_MODEL // GATE_TILE, D_MODEL, proj_block)
    in_specs += [sga_spec, sma_spec, smb_spec, row_spec(D_CONV, proj_block), row_spec(D_MODEL, proj_block),
                 const_spec((D_ATTN, D_MODEL)), const_spec((D_CONV, D_MODEL)), const_spec((D_MODEL, D_MODEL))]
    return pl.pallas_call(
        functools.partial(_out_kernel, tm=tm, n_groups=n_groups),
        grid=(n_steps,),
        in_specs=in_specs,
        out_specs=row_spec(D_MODEL, proj_block),
        out_shape=jax.ShapeDtypeStruct((rows, D_MODEL), F32),
        scratch_shapes=scratch,
        compiler_params=_params(("arbitrary",)),
        name="out_proj",
    )(*os_, *lses, sga, sma, smb, yb_in, x, w_out_a, w_out_b, w_o)


def kernel(x_prompt, x_sample, cache_kv_w128, cache_kv_w512, cache_kv_w2048, state_conv, norm_gain, w_in,
           q_norm_gain, k_norm_gain, rel_bias, conv_w, w_out_a, w_out_b, w_o):
    seq = x_prompt.shape[1]
    db = x_sample.shape[0]
    xp = x_prompt.reshape(seq, D_MODEL)
    xs = x_sample.reshape(db, D_MODEL)
    caches = (cache_kv_w128, cache_kv_w512, cache_kv_w2048)
    qk_gains = jnp.stack([jnp.tile(q_norm_gain, N_SLOTS), jnp.tile(k_norm_gain, N_SLOTS)]).reshape(2, 1, D_ATTN)
    cvec = _bias_by_offset(rel_bias)

    xn = _pre_norm(xp, norm_gain)
    qkv0, kv_p0, w_qkv = _qkv_proj(xn, _Cols(w_in, 0), qk_gains, 0, DILATIONS[0], BF16, min(WINDOWS[0], seq),
                                   casts=((w_in, 0, COL_GATE_A),))
    w_qkv = _Cols(w_qkv, 0)
    qkv1, kv_p1, w_rest, woa, wob, wo = _qkv_proj(
        xn, w_qkv, qk_gains, 1, DILATIONS[1], BF16, min(WINDOWS[1], seq),
        casts=((w_in, COL_GATE_A, w_in.shape[1] - COL_GATE_A),
               (w_out_a, 0, D_MODEL), (w_out_b, 0, D_MODEL), (w_o, 0, D_MODEL)))
    w_rest = _Cols(w_rest, COL_GATE_A)
    qkv2, kv_p2 = _qkv_proj(xn, w_qkv, qk_gains, 2, DILATIONS[2], BF16, min(WINDOWS[2], seq))
    qkvs, kv_p = (qkv0, qkv1, qkv2), (kv_p0, kv_p1, kv_p2)
    xn_s = _pre_norm(xs, norm_gain)
    qkvs_s, kv_s = zip(*[_qkv_proj(xn_s, w_qkv, qk_gains, g, 1, F32, db) for g in range(N_GROUPS)])
    yb_in_s, h_s = _branch_b(xn_s, w_rest, conv_w, state=state_conv)
    gates_s = [(_gates(xn_s, w_rest, 0, N_GATE_TILES), 0)]
    qs_s = [qkv[0, 0].reshape(db, N_SLOTS, HEAD_DIM) for qkv in qkvs_s]
    o_s, gates_c = _attn_sample(qs_s, kv_s, caches, *_sample_bias(cvec), gate_rider=(xn, w_rest, SILU_TILE))
    y_sample = _out_proj((o_s.reshape(db, D_ATTN),), (), gates_s, yb_in_s, xs, woa, wob, wo).reshape(db, 1, D_MODEL)

    gates_a, attn0 = _gates(xn, w_rest, 0, 2, attn=(qkvs[0], cvec, 0))
    gates_b, attn1 = _gates(xn, w_rest, 2, 2, attn=(qkvs[1], cvec, 1))
    yb_in, h_last, attn2 = _branch_b(xn, w_rest, conv_w, attn=(qkvs[2], cvec, 2))
    gates = [(gates_a, 0), (gates_b, 2), (gates_c, SILU_TILE)]
    os_, lses = zip(attn0, attn1, attn2)
    y_prompt = _out_proj(os_, lses, gates, yb_in, xp, woa, wob, wo).reshape(1, seq, D_MODEL)
    conv_p = h_last[SUBLANES - (CONV_W - 1):][None]
    conv_s = jnp.stack([state_conv[:, CONV_W - 2], h_s], axis=1)

    return (y_prompt, y_sample, kv_p[0][None], kv_p[1][None], kv_p[2][None], conv_p,
            kv_s[0][:, None], kv_s[1][:, None], kv_s[2][:, None], conv_s)
```

```python
import functools
import math
from typing import Callable, NamedTuple

import numpy as np
import jax
import jax.numpy as jnp
from jax import lax
from jax.experimental import pallas as pl
from jax.experimental.pallas import tpu as pltpu

D_MODEL = 2048
N_GROUPS = 3
DILATIONS = (1, 4, 16)
N_BACK = 128
WINDOWS = (128, 512, 2048)
N_SLOTS = 8
HEAD_DIM = 128
D_ATTN = N_SLOTS * HEAD_DIM
QKV_COLS = N_GROUPS * D_ATTN
D_CONV = D_MODEL // 2
CONV_W = 3
BLOCK = N_BACK
NUM_BUCKETS = 32
MAX_EXACT = NUM_BUCKETS // 2
MAX_DISTANCE = 2048
EPS = 1e-6
SCALE = HEAD_DIM ** -0.5

COL_K = QKV_COLS
COL_V = 2 * QKV_COLS
COL_GATE_A = 3 * QKV_COLS
COL_B = COL_GATE_A + D_ATTN
COL_C = COL_B + D_CONV
COL_XIN = COL_C + D_CONV
COL_GATE_B = COL_XIN + D_CONV
COL_MERGE_A = COL_GATE_B + D_CONV
COL_MERGE_B = COL_MERGE_A + D_MODEL

SUBLANES = 8
LANES = 128
MXU_COLS = 256
VMEM_LIMIT_BYTES = 56 * 1024 * 1024

BF16 = jnp.bfloat16
F32 = jnp.float32
NEG_INF = float("-inf")


def _params(semantics):
    return pltpu.CompilerParams(dimension_semantics=semantics, vmem_limit_bytes=VMEM_LIMIT_BYTES)


def _norm_kernel(x_ref, g_ref, o_ref):
    x = x_ref[...]
    r = lax.rsqrt(jnp.mean(x * x, axis=-1, keepdims=True) + EPS)
    o_ref[...] = ((x * r) * g_ref[...]).astype(o_ref.dtype)


def _pre_norm(x, gain):
    rows = x.shape[0]
    tm = min(rows, 512)
    return pl.pallas_call(
        _norm_kernel,
        grid=(rows // tm,),
        in_specs=[pl.BlockSpec((tm, D_MODEL), lambda m: (m, 0)),
                  pl.BlockSpec((1, D_MODEL), lambda m: (0, 0))],
        out_specs=pl.BlockSpec((tm, D_MODEL), lambda m: (m, 0)),
        out_shape=jax.ShapeDtypeStruct((rows, D_MODEL), BF16),
        compiler_params=_params(("arbitrary",)),
        name="pre_norm",
    )(x, gain.reshape(1, D_MODEL))


ROW_STRIDE = 4


def _qkv_kernel(xn_ref, w_ref, gain_ref, *refs, dil, tm, tr, m_tail0, n_casts, w_is_f32):
    two_stage = dil > ROW_STRIDE
    n_scratch = 2 if two_stage else 1
    cast_in, (o_ref, tail_ref), cast_out, own_scratch, w_scratch = (
        refs[:n_casts], refs[n_casts:n_casts + 2], refs[n_casts + 2:2 * n_casts + 2],
        refs[2 * n_casts + 2:2 * n_casts + 2 + n_scratch], refs[2 * n_casts + 2 + n_scratch:])
    res_ref = own_scratch[0]
    kind = pl.program_id(0)
    m = pl.program_id(1)
    if w_is_f32:
        w_f32_ref, (w_ref,) = w_ref, w_scratch

        @pl.when(m == 0)
        def _():
            w_ref[...] = w_f32_ref[...].astype(BF16)

    normed = kind < 2
    xn = xn_ref[...]
    sub = tm // dil
    part = tm // ROW_STRIDE
    heads_per_chunk = MXU_COLS // HEAD_DIM
    n_chunks = D_ATTN // MXU_COLS
    for j in range(n_chunks):
        p = jnp.dot(xn, w_ref[:, j * MXU_COLS:(j + 1) * MXU_COLS], preferred_element_type=F32)
        for src_ref, dst_ref in zip(cast_in, cast_out):
            share = src_ref.shape[0] // n_chunks
            dst_ref[j * share:(j + 1) * share, :] = src_ref[j * share:(j + 1) * share, :].astype(BF16)
        for hh in range(heads_per_chunk):
            h = j * heads_per_chunk + hh
            sl = slice(h * HEAD_DIM, (h + 1) * HEAD_DIM)
            ph = p[:, hh * HEAD_DIM:(hh + 1) * HEAD_DIM]
            r = lax.rsqrt(jnp.mean(ph * ph, axis=-1, keepdims=True) + EPS)
            res = (ph * jnp.where(normed, r, 1.0)) * jnp.where(normed, gain_ref[:, sl], 1.0)
            res_ref[h] = res
            if two_stage:
                stage_ref = own_scratch[1].at[h % 2]
                for r1 in range(ROW_STRIDE):
                    stage_ref[r1 * part:(r1 + 1) * part, :] = res_ref[h, pl.ds(r1, part, stride=ROW_STRIDE), :]
            for rr in range(dil):
                if dil == 1:
                    rows = res
                elif two_stage:
                    r1, r2 = rr % ROW_STRIDE, rr // ROW_STRIDE
                    rows = stage_ref[pl.ds(r1 * part + r2, sub, stride=ROW_STRIDE), :]
                else:
                    rows = res_ref[h, pl.ds(rr, sub, stride=dil), :]
                o_ref[rr, :, sl] = rows.astype(o_ref.dtype)

    @pl.when(jnp.logical_and(kind >= 1, m >= m_tail0))
    def _():
        for h in range(N_SLOTS):
            tail_ref[:, h, :] = res_ref[h, tm - tr:tm, :]


class _Cols(NamedTuple):
    arr: jax.Array
    col0: int

    def block(self, col_block, width):
        return col_block - self.col0 // width


def _qkv_proj(xn, w_in, qk_gains, g, dil, out_dtype, tail_rows, casts=()):
    rows = xn.shape[0]
    tm = min(rows, 1024)
    tn = D_ATTN
    tr = min(tail_rows, tm)
    m_tail0 = rows // tm - tail_rows // tr
    grid = (3, rows // tm)
    n_steps = grid[0] * grid[1]
    w_is_f32 = w_in.arr.dtype == F32

    def tail_map(kind, m):
        return (jnp.where(kind >= 1, jnp.maximum(m - m_tail0, 0), 0), jnp.maximum(kind - 1, 0), 0, 0)

    def cast_specs_of(w, col0, n_cols):
        n_blocks = max(d for d in range(1, n_steps + 1) if n_cols % (d * LANES) == 0)
        width = n_cols // n_blocks
        assert col0 % width == 0

        def block(kind, m):
            return jnp.minimum(kind * grid[1] + m, n_blocks - 1)

        return (pl.BlockSpec((w.shape[0], width), lambda kind, m: (0, col0 // width + block(kind, m))),
                pl.BlockSpec((w.shape[0], width), lambda kind, m: (0, block(kind, m))))

    cast_in_specs, cast_out_specs = zip(*[cast_specs_of(*c) for c in casts]) if casts else ((), ())
    assert dil <= ROW_STRIDE or dil == ROW_STRIDE * ROW_STRIDE
    scratch = [pltpu.VMEM((N_SLOTS, tm, HEAD_DIM), F32)]
    if dil > ROW_STRIDE:
        scratch.append(pltpu.VMEM((2, tm, HEAD_DIM), F32))
    if w_is_f32:
        scratch.append(pltpu.VMEM((D_MODEL, tn), BF16))
    return pl.pallas_call(
        functools.partial(_qkv_kernel, dil=dil, tm=tm, tr=tr, m_tail0=m_tail0, n_casts=len(casts), w_is_f32=w_is_f32),
        grid=grid,
        in_specs=[pl.BlockSpec((tm, D_MODEL), lambda kind, m: (m, 0)),
                  pl.BlockSpec((D_MODEL, tn), lambda kind, m: (0, w_in.block(kind * N_GROUPS + g, tn))),
                  pl.BlockSpec((None, 1, tn), lambda kind, m: (jnp.minimum(kind, 1), 0, 0))] + list(cast_in_specs),
        out_specs=[pl.BlockSpec((None, dil, tm // dil, tn), lambda kind, m: (kind, 0, m, 0)),
                   pl.BlockSpec((tr, None, N_SLOTS, HEAD_DIM), tail_map)] + list(cast_out_specs),
        out_shape=[jax.ShapeDtypeStruct((3, dil, rows // dil, tn), out_dtype),
                   jax.ShapeDtypeStruct((tail_rows, 2, N_SLOTS, HEAD_DIM), F32)]
                  + [jax.ShapeDtypeStruct((w.shape[0], n_cols), BF16) for w, _, n_cols in casts],
        scratch_shapes=scratch,
        compiler_params=_params(("arbitrary", "arbitrary")),
        name=f"qkv_proj_g{g}",
    )(xn, w_in.arr, qk_gains, *[w for w, _, _ in casts])


def _branch_b_kernel(xn_ref, wb_ref, wc_ref, wx_ref, wg_ref, cw_ref, *refs, tm, side):
    side_in, (y_ref, h_ref), side_out, (hbuf_ref,), side_scratch = (
        refs[:6], refs[6:8], refs[8:10], refs[10:11], refs[11:])

    @pl.when(pl.program_id(1) == 0)
    def _():
        hbuf_ref[0:SUBLANES, :] = jnp.zeros((SUBLANES, hbuf_ref.shape[1]), F32)

    _attn_side(side, pl.program_id(0) * pl.num_programs(1) + pl.program_id(1), *side_in, *side_out, *side_scratch)
    xn = xn_ref[...]
    pb = jnp.dot(xn, wb_ref[...], preferred_element_type=F32)
    pc = jnp.dot(xn, wc_ref[...], preferred_element_type=F32)
    px = jnp.dot(xn, wx_ref[...], preferred_element_type=F32)
    pg = jnp.dot(xn, wg_ref[...], preferred_element_type=F32)
    h = pc * px
    hbuf_ref[SUBLANES:SUBLANES + tm, :] = h
    hm2 = hbuf_ref[pl.ds(SUBLANES - 2, tm), :]
    hm1 = hbuf_ref[pl.ds(SUBLANES - 1, tm), :]
    last = hbuf_ref[pl.ds(tm, SUBLANES), :]
    h_ref[...] = last
    hbuf_ref[0:SUBLANES, :] = last
    z = cw_ref[0:1, :] * hm2 + cw_ref[1:2, :] * hm1 + cw_ref[2:3, :] * h
    y_ref[...] = (pb * z * (pg * jax.nn.sigmoid(pg))).astype(y_ref.dtype)


def _branch_b(xn, w_in, conv_w, attn):
    rows = xn.shape[0]
    tm = min(rows, 1024)
    tnb = 256
    grid = (D_CONV // tnb, rows // tm)
    side = _attn_side_plan(*attn, n_steps=grid[0] * grid[1], step_of=lambda j, m: j * grid[1] + m)

    def wspec(col0):
        return pl.BlockSpec((D_MODEL, tnb), lambda j, m: (0, w_in.block(col0 // tnb + j, tnb)))

    y, h_last, o, lse = pl.pallas_call(
        functools.partial(_branch_b_kernel, tm=tm, side=side.static),
        grid=grid,
        in_specs=[pl.BlockSpec((tm, D_MODEL), lambda j, m: (m, 0)),
                  wspec(COL_B), wspec(COL_C), wspec(COL_XIN), wspec(COL_GATE_B),
                  pl.BlockSpec((CONV_W, tnb), lambda j, m: (0, j))] + side.in_specs,
        out_specs=[pl.BlockSpec((tm, tnb), lambda j, m: (m, j)),
                   pl.BlockSpec((SUBLANES, tnb), lambda j, m: (0, j))] + side.out_specs,
        out_shape=[jax.ShapeDtypeStruct((rows, D_CONV), BF16),
                   jax.ShapeDtypeStruct((SUBLANES, D_CONV), F32)] + side.out_shape,
        scratch_shapes=[pltpu.VMEM((tm + SUBLANES, tnb), F32)] + side.scratch,
        compiler_params=_params(("arbitrary", "arbitrary")),
        name="branch_b",
    )(xn, w_in.arr, w_in.arr, w_in.arr, w_in.arr, conv_w, *side.args)
    return y, h_last, side.finish(o, lse)


GATE_TILE = 1024
N_GATE_TILES = (2 * D_MODEL + D_ATTN) // GATE_TILE
SILU_TILE = N_GATE_TILES - 1


def _gate_columns(xn, w_ref, o_ref, is_silu):
    for j in range(w_ref.shape[1] // MXU_COLS):
        cs = slice(j * MXU_COLS, (j + 1) * MXU_COLS)
        p = jnp.dot(xn, w_ref[:, cs], preferred_element_type=F32)
        o_ref[:, cs] = jax.nn.sigmoid(p) * jnp.where(is_silu, p, 1.0)


def _gate_weight_block(tile):
    return jnp.where(tile == SILU_TILE, COL_GATE_A // GATE_TILE, COL_MERGE_A // GATE_TILE + tile)


def _gates_kernel(xn_ref, w_ref, *refs, tile0, side):
    if side is not None:
        side_in, o_ref, side_out, side_scratch = refs[:6], refs[6], refs[7:9], refs[9:]
        _attn_side(side, pl.program_id(0) * pl.num_programs(1) + pl.program_id(1),
                   *side_in, *side_out, *side_scratch)
    else:
        o_ref, = refs
    _gate_columns(xn_ref[...], w_ref, o_ref, tile0 + pl.program_id(0) == SILU_TILE)


def _gates(xn, w_in, tile0, n_tiles, attn=None):
    rows = xn.shape[0]
    tm = min(rows, 1024)
    tn = GATE_TILE
    grid = (n_tiles, rows // tm)
    side = None if attn is None else _attn_side_plan(*attn, n_steps=grid[0] * grid[1],
                                                     step_of=lambda n, m: n * grid[1] + m)

    def wmap(n, m):
        return (0, w_in.block(_gate_weight_block(tile0 + n), tn))

    in_specs = [pl.BlockSpec((tm, D_MODEL), lambda n, m: (m, 0)), pl.BlockSpec((D_MODEL, tn), wmap)]
    args = [xn, w_in.arr]
    out_specs = [pl.BlockSpec((tm, tn), lambda n, m: (m, n))]
    out_shape = [jax.ShapeDtypeStruct((rows, n_tiles * tn), F32)]
    scratch = []
    if side is not None:
        in_specs += side.in_specs
        args += side.args
        out_specs += side.out_specs
        out_shape += side.out_shape
        scratch += side.scratch
    outs = pl.pallas_call(
        functools.partial(_gates_kernel, tile0=tile0, side=None if side is None else side.static),
        grid=grid,
        in_specs=in_specs,
        out_specs=out_specs,
        out_shape=out_shape,
        scratch_shapes=scratch,
        compiler_params=_params(("arbitrary", "arbitrary")),
        name="gates",
    )(*args)
    return outs[0] if side is None else (outs[0], side.finish(outs[1], outs[2]))


PROJ_TILE = 1024
N_QKV_TILES = COL_GATE_A // PROJ_TILE
N_PROJ_TILES = COL_MERGE_B // PROJ_TILE + D_MODEL // PROJ_TILE
TILE_GATE_A, TILE_B, TILE_C, TILE_XIN, TILE_GATE_B, TILE_MERGE_A = (
    c // PROJ_TILE for c in (COL_GATE_A, COL_B, COL_C, COL_XIN, COL_GATE_B, COL_MERGE_A))


def _sample_proj_kernel(xn_ref, wq_ref, wr_ref, gain_ref, cw_ref, st_ref,
                        q_ref, kv0_ref, kv1_ref, kv2_ref, gates_ref, y_ref, h_ref, bcx_ref):
    n = pl.program_id(0)
    kv_refs = (kv0_ref, kv1_ref, kv2_ref)
    heads = [slice(h * HEAD_DIM, (h + 1) * HEAD_DIM) for h in range(N_SLOTS)]

    def project(w_ref):
        return jnp.dot(xn_ref[...], w_ref[...], preferred_element_type=F32)

    def store_heads(dst_ref, p, gain_row):
        for h, sl in enumerate(heads):
            ph = p[:, sl]
            if gain_row is not None:
                r = lax.rsqrt(jnp.mean(ph * ph, axis=-1, keepdims=True) + EPS)
                ph = (ph * r) * gain_ref[gain_row, :, sl]
            dst_ref[:, h, :] = ph

    for t in range(N_PROJ_TILES):
        @pl.when(n == t)
        def _(t=t):
            if t < N_QKV_TILES:
                kind, g = divmod(t, N_GROUPS)
                dst_ref = q_ref if kind == 0 else kv_refs[g]
                store_heads(dst_ref, project(wq_ref), kind if kind < 2 else None)
            elif t == TILE_GATE_A:
                p = project(wr_ref)
                gates_ref[...] = p * jax.nn.sigmoid(p)
            elif t in (TILE_B, TILE_C, TILE_XIN):
                bcx_ref[t - TILE_B] = project(wr_ref)
            elif t == TILE_GATE_B:
                pg = project(wr_ref)
                h = bcx_ref[1] * bcx_ref[2]
                z = (cw_ref[0:1, :] * st_ref[:, 0:D_CONV] + cw_ref[1:2, :] * st_ref[:, D_CONV:2 * D_CONV]
                     + cw_ref[2:3, :] * h)
                h_ref[...] = h
                y_ref[...] = (bcx_ref[0] * z * (pg * jax.nn.sigmoid(pg))).astype(y_ref.dtype)
            else:
                gates_ref[...] = jax.nn.sigmoid(project(wr_ref))


def _sample_proj(xn, w_qkv, w_rest, qk_gains, conv_w, state):
    rows = xn.shape[0]
    tn = PROJ_TILE
    assert w_qkv.col0 == 0 and w_rest.col0 == COL_GATE_A

    def const(shape):
        return pl.BlockSpec(shape, lambda n: (0,) * len(shape))

    def gate_block(n):
        return jnp.where(n >= TILE_MERGE_A, n - TILE_MERGE_A, SILU_TILE)

    head_block = (rows, None, N_SLOTS, HEAD_DIM)
    kv_specs = [pl.BlockSpec(head_block, lambda n, g=g: (0, jnp.where(n >= 2 * N_GROUPS + g, 1, 0), 0, 0))
                for g in range(N_GROUPS)]
    return pl.pallas_call(
        _sample_proj_kernel,
        grid=(N_PROJ_TILES,),
        in_specs=[const((rows, D_MODEL)),
                  pl.BlockSpec((D_MODEL, tn), lambda n: (0, jnp.minimum(n, N_QKV_TILES - 1))),
                  pl.BlockSpec((D_MODEL, tn), lambda n: (0, jnp.maximum(n - N_QKV_TILES, 0))),
                  const((2, 1, D_ATTN)), const((CONV_W, D_CONV)), const((rows, (CONV_W - 1) * D_CONV))],
        out_specs=[pl.BlockSpec(head_block, lambda n: (0, jnp.minimum(n, N_GROUPS - 1), 0, 0))] + kv_specs
                  + [pl.BlockSpec((rows, GATE_TILE), lambda n: (0, gate_block(n))),
                     const((rows, D_CONV)), const((rows, D_CONV))],
        out_shape=[jax.ShapeDtypeStruct((rows, N_GROUPS, N_SLOTS, HEAD_DIM), F32)]
                  + [jax.ShapeDtypeStruct((rows, 2, N_SLOTS, HEAD_DIM), F32)] * N_GROUPS
                  + [jax.ShapeDtypeStruct((rows, N_GATE_TILES * GATE_TILE), F32),
                     jax.ShapeDtypeStruct((rows, D_CONV), BF16), jax.ShapeDtypeStruct((rows, D_CONV), F32)],
        scratch_shapes=[pltpu.VMEM((3, rows, D_CONV), F32)],
        compiler_params=_params(("arbitrary",)),
        name="sample_proj",
    )(xn, w_qkv.arr, w_rest.arr, qk_gains, conv_w, state.reshape(rows, (CONV_W - 1) * D_CONV))


def _t5_bucket(dist):
    d = np.maximum(dist, 1).astype(np.float32)
    large = MAX_EXACT + (np.log(d / np.float32(MAX_EXACT)) / np.float32(math.log(MAX_DISTANCE / MAX_EXACT))
                         * np.float32(NUM_BUCKETS - MAX_EXACT)).astype(np.int32)
    large = np.minimum(large, NUM_BUCKETS - 1)
    return np.where(dist < MAX_EXACT, dist, large)


def _bias_by_offset(rel_bias):
    steps = N_BACK - np.arange(N_BACK + 1)
    idx = np.stack([_t5_bucket(d * steps) for d in DILATIONS])
    g_idx = np.arange(N_GROUPS)[:, None]
    vals = jnp.transpose(rel_bias[idx, g_idx], (0, 2, 1)).astype(F32)
    return jnp.pad(vals, ((0, 0), (0, 0), (0, 2 * BLOCK - N_BACK - 1)), constant_values=NEG_INF)


def _sample_bias(cvec):
    kv_head = np.arange(N_BACK * 2 * N_SLOTS) % (2 * N_SLOTS)
    own_k_row = kv_head[None, :] == np.arange(N_SLOTS)[:, None]
    per_row = jnp.repeat(cvec[:, :, :N_BACK], 2 * N_SLOTS, axis=2)
    return jnp.where(own_k_row[None], per_row, NEG_INF), cvec[:, :, N_BACK:N_BACK + 1]


class _SideStatic(NamedTuple):
    per_step: int
    nb: int
    steps: int


class _SidePlan(NamedTuple):
    static: _SideStatic
    in_specs: list
    args: list
    out_specs: list
    out_shape: list
    scratch: list
    finish: Callable


def _attn_side(side, step, cvec_ref, q_ref, kp_ref, kc_ref, vp_ref, vc_ref, o_ref, lse_ref, bias_ref, s_ref, p_ref):
    @pl.when(step == 0)
    def _():
        for h in range(N_SLOTS):
            row = jnp.broadcast_to(cvec_ref[h:h + 1, :], (BLOCK, 2 * BLOCK))
            full = pltpu.roll(row, 0, 1, stride=1, stride_axis=0)
            bias_ref[1, h] = full
            bias_ref[0, h, :, :BLOCK] = jnp.full((BLOCK, BLOCK), NEG_INF, F32)
            bias_ref[0, h, :, BLOCK:] = full[:, BLOCK:]

    nt = (((1,), (1,)), ((), ()))
    heads = [slice(h * HEAD_DIM, (h + 1) * HEAD_DIM) for h in range(N_SLOTS)]
    unit0 = jnp.minimum(step, side.steps - 1) * side.per_step
    lane = lax.broadcasted_iota(jnp.int32, (BLOCK, LANES), 1)
    for u in range(side.per_step):
        rows = slice(u * BLOCK, (u + 1) * BLOCK)
        prev = slice((u - 1) * BLOCK, u * BLOCK)
        for h, sl in enumerate(heads):
            q = q_ref[rows, sl]
            k_prev = kp_ref[:, sl] if u == 0 else kc_ref[prev, sl]
            s_ref[u, h, :, :BLOCK] = lax.dot_general(q, k_prev, nt, preferred_element_type=F32)
            s_ref[u, h, :, BLOCK:] = lax.dot_general(q, kc_ref[rows, sl], nt, preferred_element_type=F32)
    ls, lses = [], []
    for u in range(side.per_step):
        has_prev = jnp.minimum((unit0 + u) % side.nb, 1)
        sc = s_ref[u] * SCALE + bias_ref[has_prev]
        mx = jnp.max(jnp.maximum(sc[:, :, :BLOCK], sc[:, :, BLOCK:]), axis=-1, keepdims=True)
        p = jnp.exp(sc - mx)
        l = jnp.sum(p[:, :, :BLOCK] + p[:, :, BLOCK:], axis=-1, keepdims=True)
        p_ref[u] = p.astype(BF16)
        ls.append(l)
        lses.append(mx + jnp.log(l))
    for u in range(side.per_step):
        rows = slice(u * BLOCK, (u + 1) * BLOCK)
        prev = slice((u - 1) * BLOCK, u * BLOCK)
        lse_all = jnp.zeros((BLOCK, LANES), F32)
        for h, sl in enumerate(heads):
            v_prev = vp_ref[:, sl] if u == 0 else vc_ref[prev, sl]
            o = (jnp.dot(p_ref[u, h, :, :BLOCK], v_prev, preferred_element_type=F32)
                 + jnp.dot(p_ref[u, h, :, BLOCK:], vc_ref[rows, sl], preferred_element_type=F32))
            o_ref[rows, sl] = o / ls[u][h]
            lse_all = jnp.where(lane == h, lses[u][h], lse_all)
        lse_ref[rows, :] = lse_all


def _attn_side_plan(qkv, cvec, g, n_steps, step_of):
    _, dil, ls, _ = qkv.shape
    nb = ls // BLOCK
    units = dil * nb
    per_step = min(d for d in range(1, units + 1) if units % d == 0 and d * n_steps >= units)
    static = _SideStatic(per_step=per_step, nb=nb, steps=units // per_step)
    flat = qkv.reshape(3, dil * ls, D_ATTN)
    rows = per_step * BLOCK

    def cur(kind):
        return pl.BlockSpec((None, rows, D_ATTN), lambda *idx: (kind, jnp.minimum(step_of(*idx), static.steps - 1), 0))

    def prev(kind):
        def index(*idx):
            first_unit = jnp.minimum(step_of(*idx), static.steps - 1) * per_step
            return (kind, jnp.maximum(first_unit - 1, 0), 0)
        return pl.BlockSpec((None, BLOCK, D_ATTN), index)

    def out_spec(width):
        return pl.BlockSpec((rows, width), lambda *idx: (jnp.minimum(step_of(*idx), static.steps - 1), 0))

    return _SidePlan(
        static=static,
        in_specs=[pl.BlockSpec((None, N_SLOTS, 2 * BLOCK), lambda *idx: (g, 0, 0)),
                  cur(0), prev(1), cur(1), prev(2), cur(2)],
        args=[cvec, flat, flat, flat, flat, flat],
        out_specs=[out_spec(D_ATTN), out_spec(LANES)],
        out_shape=[jax.ShapeDtypeStruct((dil * ls, D_ATTN), F32), jax.ShapeDtypeStruct((dil * ls, LANES), F32)],
        scratch=[pltpu.VMEM((2, N_SLOTS, BLOCK, 2 * BLOCK), F32),
                 pltpu.VMEM((per_step, N_SLOTS, BLOCK, 2 * BLOCK), F32),
                 pltpu.VMEM((per_step, N_SLOTS, BLOCK, 2 * BLOCK), BF16)],
        finish=lambda o, lse: (o.reshape(dil, ls, D_ATTN), lse.reshape(dil, ls, LANES)),
    )


def _attn_sample_kernel(bias_ref, nbias_ref, q0_ref, q1_ref, q2_ref, kv0_ref, kv1_ref, kv2_ref,
                        c0_ref, c1_ref, c2_ref, *refs, tb, gate_tile):
    if gate_tile is None:
        o_ref, slab_ref, s_ref, p_ref, og_ref = refs
    else:
        xn_ref, w_ref, o_ref, gate_ref, slab_ref, s_ref, p_ref, og_ref = refs
        _gate_columns(xn_ref[...], w_ref, gate_ref, gate_tile == SILU_TILE)
    rows = pl.ds(pl.program_id(0) * tb, tb)
    nt = (((1,), (1,)), ((), ()))
    kv_rows = 2 * N_SLOTS
    slab_rows = N_BACK * kv_rows
    q_refs = (q0_ref, q1_ref, q2_ref)
    kv_refs = (kv0_ref, kv1_ref, kv2_ref)
    c_refs = (c0_ref, c1_ref, c2_ref)

    def rounded(x):
        return x.astype(BF16).astype(F32)

    qs = [q_ref[rows] for q_ref in q_refs]
    for g in range(N_GROUPS):
        for t in range(tb):
            slab_ref[g, t] = c_refs[g][t].reshape(slab_rows, HEAD_DIM).astype(BF16)
            s_ref[g, t] = lax.dot_general(qs[g][t].astype(BF16), slab_ref[g, t], nt, preferred_element_type=F32)
    p_news, ls, lses = [], [], []
    for g in range(N_GROUPS):
        s_n = jnp.sum(rounded(qs[g]) * rounded(kv_refs[g][rows, 0]), axis=-1, keepdims=True)
        sc_c = s_ref[g] * SCALE + bias_ref[g][None]
        sc_n = s_n * SCALE + nbias_ref[g][None]
        mx = jnp.maximum(jnp.max(sc_c, axis=-1, keepdims=True), sc_n)
        p_c = jnp.exp(sc_c - mx)
        p_n = jnp.exp(sc_n - mx)
        l = jnp.sum(p_c, axis=-1, keepdims=True) + p_n
        p_ref[g] = pltpu.roll(p_c, N_SLOTS, 2).astype(BF16)
        p_news.append(p_n)
        ls.append(l)
        lses.append(mx + jnp.log(l))
    for g in range(N_GROUPS):
        for t in range(tb):
            og_ref[g, t] = jnp.dot(p_ref[g, t], slab_ref[g, t], preferred_element_type=F32)
    outs = [(og_ref[g] + rounded(p_news[g]) * rounded(kv_refs[g][rows, 1])) / ls[g] for g in range(N_GROUPS)]
    top = jnp.maximum(jnp.maximum(lses[0], lses[1]), lses[2])
    es = [jnp.exp(x - top) for x in lses]
    tot = es[0] + es[1] + es[2]
    o_ref[rows] = (es[0] / tot) * outs[0] + (es[1] / tot) * outs[1] + (es[2] / tot) * outs[2]


def _attn_sample(qs, kvs, caches, bias, nbias, gate_rider=None):
    db = qs.shape[0]
    tb = 4
    kv_rows = 2 * N_SLOTS
    n_steps = db // tb
    rider_in_specs, rider_args, rider_out_specs, rider_out_shape, gate_tile = [], [], [], [], None
    if gate_rider is not None:
        xn, w_in, gate_tile = gate_rider
        rows_per_step = xn.shape[0] // n_steps
        rider_in_specs = [pl.BlockSpec((rows_per_step, D_MODEL), lambda s: (s, 0)),
                          pl.BlockSpec((D_MODEL, GATE_TILE),
                                       lambda s: (0, w_in.block(_gate_weight_block(gate_tile), GATE_TILE)))]
        rider_args = [xn, w_in.arr]
        rider_out_specs = [pl.BlockSpec((rows_per_step, GATE_TILE), lambda s: (s, 0))]
        rider_out_shape = [jax.ShapeDtypeStruct((xn.shape[0], GATE_TILE), F32)]
    views, specs = [], []
    for g, c in enumerate(caches):
        dil = DILATIONS[g]
        lc = c.shape[1]
        views.append(c.reshape(db, lc // dil, dil * kv_rows, HEAD_DIM))
        specs.append(pl.BlockSpec((tb, N_BACK, kv_rows, HEAD_DIM), lambda s: (s, 0, 0, 0)))
    q_specs = [pl.BlockSpec((db, None, N_SLOTS, HEAD_DIM), lambda s, g=g: (0, g, 0, 0)) for g in range(N_GROUPS)]
    kv_spec = pl.BlockSpec((db, 2, N_SLOTS, HEAD_DIM), lambda s: (0, 0, 0, 0))
    outs = pl.pallas_call(
        functools.partial(_attn_sample_kernel, tb=tb, gate_tile=gate_tile),
        grid=(n_steps,),
        in_specs=[pl.BlockSpec((N_GROUPS, N_SLOTS, N_BACK * kv_rows), lambda s: (0, 0, 0)),
                  pl.BlockSpec((N_GROUPS, N_SLOTS, 1), lambda s: (0, 0, 0)),
                  *q_specs, kv_spec, kv_spec, kv_spec] + specs + rider_in_specs,
        out_specs=[pl.BlockSpec((db, N_SLOTS, HEAD_DIM), lambda s: (0, 0, 0))] + rider_out_specs,
        out_shape=[jax.ShapeDtypeStruct((db, N_SLOTS, HEAD_DIM), F32)] + rider_out_shape,
        scratch_shapes=[pltpu.VMEM((N_GROUPS, tb, N_BACK * kv_rows, HEAD_DIM), BF16),
                        pltpu.VMEM((N_GROUPS, tb, N_SLOTS, N_BACK * kv_rows), F32),
                        pltpu.VMEM((N_GROUPS, tb, N_SLOTS, N_BACK * kv_rows), BF16),
                        pltpu.VMEM((N_GROUPS, tb, N_SLOTS, HEAD_DIM), F32)],
        compiler_params=_params(("arbitrary",)),
        name="attn_sample",
    )(bias, nbias, qs, qs, qs, *kvs, *views, *rider_args)
    return outs[0] if gate_rider is None else outs


def _out_kernel(*refs, tm, n_groups):
    o_refs = refs[:n_groups]
    lse_refs = refs[n_groups:2 * n_groups] if n_groups > 1 else ()
    rest = refs[len(o_refs) + len(lse_refs):]
    def project(a):
        ya = jnp.dot(a, woa_ref[...], preferred_element_type=F32)
        yb = jnp.dot(yb_ref[...], wob_ref[...], preferred_element_type=F32)
        merged = sma_ref[...] * ya + smb_ref[...] * yb
        y_ref[...] = x_ref[...] + jnp.dot(merged.astype(BF16), wo_ref[...], preferred_element_type=F32)

    if n_groups > 1:
        sga_ref, sma_ref, smb_ref, yb_ref, x_ref, woa_ref, wob_ref, wo_ref, y_ref, a2_ref, og_ref, lg_ref = rest
        step = pl.program_id(0)

        @pl.when(step == 0)
        def _():
            a2_ref[1] = jnp.zeros(a2_ref.shape[1:], BF16)

        project(a2_ref[(step + 1) % 2])
        a_ref = a2_ref.at[step % 2]
        for g in range(n_groups):
            dil = DILATIONS[g]
            sub = tm // dil
            for r in range(dil):
                rows = slice(None) if dil == 1 else pl.ds(r, sub, stride=dil)
                lg_ref[g, rows, :] = lse_refs[g][r]
                for h in range(N_SLOTS):
                    og_ref[g, h, rows, :] = o_refs[g][r, :, h * HEAD_DIM:(h + 1) * HEAD_DIM]
        lses = [lg_ref[g] for g in range(n_groups)]
        top = functools.reduce(jnp.maximum, lses)
        es = [jnp.exp(x - top) for x in lses]
        tot = functools.reduce(lambda a, b: a + b, es)
        ws = [e / tot for e in es]
        for h in range(N_SLOTS):
            sl = slice(h * HEAD_DIM, (h + 1) * HEAD_DIM)
            o = functools.reduce(lambda a, b: a + b,
                                 [ws[g][:, h:h + 1] * og_ref[g, h] for g in range(n_groups)])
            a_ref[:, sl] = (o * sga_ref[:, sl]).astype(BF16)
    else:
        sga_ref, sma_ref, smb_ref, yb_ref, x_ref, woa_ref, wob_ref, wo_ref, y_ref = rest
        project((o_refs[0][...] * sga_ref[...]).astype(BF16))


def _out_proj(os_, lses, gates, yb_in, x, w_out_a, w_out_b, w_o):
    rows = x.shape[0]

    tm = min(rows, 256)
    n_groups = len(os_)
    n_blocks = rows // tm
    pipelined = n_groups > 1
    n_steps = n_blocks + 1 if pipelined else n_blocks

    def comb_block(s):
        return jnp.minimum(s, n_blocks - 1)

    def proj_block(s):
        return jnp.maximum(s - 1, 0) if pipelined else s

    def gate_operand(tile, width, block_of):
        arr, tile0 = next((a, t0) for a, t0 in gates if t0 <= tile < t0 + a.shape[1] // GATE_TILE)
        assert (tile - tile0) * GATE_TILE % width == 0
        return arr, pl.BlockSpec((tm, width), lambda s: (block_of(s), (tile - tile0) * GATE_TILE // width))

    def row_spec(width, block_of):
        return pl.BlockSpec((tm, width), lambda s: (block_of(s), 0))

    def sub_spec(dil, width):
        return pl.BlockSpec((dil, tm // dil, width), lambda s: (0, comb_block(s), 0))

    def const_spec(shape):
        return pl.BlockSpec(shape, lambda s: (0, 0), pipeline_mode=pl.Buffered(1))

    if pipelined:
        in_specs = ([sub_spec(DILATIONS[g], D_ATTN) for g in range(n_groups)]
                    + [sub_spec(DILATIONS[g], LANES) for g in range(n_groups)])
        scratch = [pltpu.VMEM((2, tm, D_ATTN), BF16),
                   pltpu.VMEM((n_groups, N_SLOTS, tm, HEAD_DIM), F32), pltpu.VMEM((n_groups, tm, LANES), F32)]
    else:
        in_specs = [row_spec(D_ATTN, comb_block)]
        scratch = []
    sga, sga_spec = gate_operand(SILU_TILE, D_ATTN, comb_block)
    sma, sma_spec = gate_operand(0, D_MODEL, proj_block)
    smb, smb_spec = gate_operand(D_MODEL // GATE_TILE, D_MODEL, proj_block)
    in_specs += [sga_spec, sma_spec, smb_spec, row_spec(D_CONV, proj_block), row_spec(D_MODEL, proj_block),
                 const_spec((D_ATTN, D_MODEL)), const_spec((D_CONV, D_MODEL)), const_spec((D_MODEL, D_MODEL))]
    return pl.pallas_call(
        functools.partial(_out_kernel, tm=tm, n_groups=n_groups),
        grid=(n_steps,),
        in_specs=in_specs,
        out_specs=row_spec(D_MODEL, proj_block),
        out_shape=jax.ShapeDtypeStruct((rows, D_MODEL), F32),
        scratch_shapes=scratch,
        compiler_params=_params(("arbitrary",)),
        name="out_proj",
    )(*os_, *lses, sga, sma, smb, yb_in, x, w_out_a, w_out_b, w_o)


def kernel(x_prompt, x_sample, cache_kv_w128, cache_kv_w512, cache_kv_w2048, state_conv, norm_gain, w_in,
           q_norm_gain, k_norm_gain, rel_bias, conv_w, w_out_a, w_out_b, w_o):
    seq = x_prompt.shape[1]
    db = x_sample.shape[0]
    xp = x_prompt.reshape(seq, D_MODEL)
    xs = x_sample.reshape(db, D_MODEL)
    caches = (cache_kv_w128, cache_kv_w512, cache_kv_w2048)
    qk_gains = jnp.stack([jnp.tile(q_norm_gain, N_SLOTS), jnp.tile(k_norm_gain, N_SLOTS)]).reshape(2, 1, D_ATTN)
    cvec = _bias_by_offset(rel_bias)

    xn = _pre_norm(xp, norm_gain)
    qkv0, kv_p0, w_qkv = _qkv_proj(xn, _Cols(w_in, 0), qk_gains, 0, DILATIONS[0], BF16, min(WINDOWS[0], seq),
                                   casts=((w_in, 0, COL_GATE_A),))
    w_qkv = _Cols(w_qkv, 0)
    qkv1, kv_p1, w_rest, woa, wob, wo = _qkv_proj(
        xn, w_qkv, qk_gains, 1, DILATIONS[1], BF16, min(WINDOWS[1], seq),
        casts=((w_in, COL_GATE_A, w_in.shape[1] - COL_GATE_A),
               (w_out_a, 0, D_MODEL), (w_out_b, 0, D_MODEL), (w_o, 0, D_MODEL)))
    w_rest = _Cols(w_rest, COL_GATE_A)
    qkv2, kv_p2 = _qkv_proj(xn, w_qkv, qk_gains, 2, DILATIONS[2], BF16, min(WINDOWS[2], seq))
    qkvs, kv_p = (qkv0, qkv1, qkv2), (kv_p0, kv_p1, kv_p2)
    xn_s = _pre_norm(xs, norm_gain)
    q_s, *kv_s, gates_s, yb_in_s, h_s = _sample_proj(xn_s, w_qkv, w_rest, qk_gains, conv_w, state_conv)
    o_s, gates_c = _attn_sample(q_s, kv_s, caches, *_sample_bias(cvec), gate_rider=(xn, w_rest, SILU_TILE))
    y_sample = _out_proj((o_s.reshape(db, D_ATTN),), (), [(gates_s, 0)], yb_in_s, xs, woa, wob, wo)
    y_sample = y_sample.reshape(db, 1, D_MODEL)

    gates_a, attn0 = _gates(xn, w_rest, 0, 2, attn=(qkvs[0], cvec, 0))
    gates_b, attn1 = _gates(xn, w_rest, 2, 2, attn=(qkvs[1], cvec, 1))
    yb_in, h_last, attn2 = _branch_b(xn, w_rest, conv_w, attn=(qkvs[2], cvec, 2))
    gates = [(gates_a, 0), (gates_b, 2), (gates_c, SILU_TILE)]
    os_, lses = zip(attn0, attn1, attn2)
    y_prompt = _out_proj(os_, lses, gates, yb_in, xp, woa, wob, wo).reshape(1, seq, D_MODEL)
    conv_p = h_last[SUBLANES - (CONV_W - 1):][None]
    conv_s = jnp.stack([state_conv[:, CONV_W - 2], h_s], axis=1)

    return (y_prompt, y_sample, kv_p[0][None], kv_p[1][None], kv_p[2][None], conv_p,
            kv_s[0][:, None], kv_s[1][:, None], kv_s[2][:, None], conv_s)
```

```python
import functools
import math
from typing import Callable, NamedTuple

import numpy as np
import jax
import jax.numpy as jnp
from jax import lax
from jax.experimental import pallas as pl
from jax.experimental.pallas import tpu as pltpu

D_MODEL = 2048
N_GROUPS = 3
DILATIONS = (1, 4, 16)
N_BACK = 128
WINDOWS = (128, 512, 2048)
N_SLOTS = 8
HEAD_DIM = 128
D_ATTN = N_SLOTS * HEAD_DIM
QKV_COLS = N_GROUPS * D_ATTN
D_CONV = D_MODEL // 2
CONV_W = 3
BLOCK = N_BACK
NUM_BUCKETS = 32
MAX_EXACT = NUM_BUCKETS // 2
MAX_DISTANCE = 2048
EPS = 1e-6
SCALE = HEAD_DIM ** -0.5

COL_K = QKV_COLS
COL_V = 2 * QKV_COLS
COL_GATE_A = 3 * QKV_COLS
COL_B = COL_GATE_A + D_ATTN
COL_C = COL_B + D_CONV
COL_XIN = COL_C + D_CONV
COL_GATE_B = COL_XIN + D_CONV
COL_MERGE_A = COL_GATE_B + D_CONV
COL_MERGE_B = COL_MERGE_A + D_MODEL

SUBLANES = 8
LANES = 128
MXU_COLS = 256
VMEM_LIMIT_BYTES = 56 * 1024 * 1024

BF16 = jnp.bfloat16
F32 = jnp.float32
NEG_INF = float("-inf")


def _params(semantics):
    return pltpu.CompilerParams(dimension_semantics=semantics, vmem_limit_bytes=VMEM_LIMIT_BYTES)


def _norm_kernel(x_ref, g_ref, o_ref):
    x = x_ref[...]
    r = lax.rsqrt(jnp.mean(x * x, axis=-1, keepdims=True) + EPS)
    o_ref[...] = ((x * r) * g_ref[...]).astype(o_ref.dtype)


def _pre_norm(x, gain):
    rows = x.shape[0]
    tm = min(rows, 512)
    return pl.pallas_call(
        _norm_kernel,
        grid=(rows // tm,),
        in_specs=[pl.BlockSpec((tm, D_MODEL), lambda m: (m, 0)),
                  pl.BlockSpec((1, D_MODEL), lambda m: (0, 0))],
        out_specs=pl.BlockSpec((tm, D_MODEL), lambda m: (m, 0)),
        out_shape=jax.ShapeDtypeStruct((rows, D_MODEL), BF16),
        compiler_params=_params(("arbitrary",)),
        name="pre_norm",
    )(x, gain.reshape(1, D_MODEL))


ROW_STRIDE = 4


def _qkv_kernel(xn_ref, w_ref, gain_ref, *refs, dil, tm, tr, m_tail0, n_casts, w_is_f32):
    two_stage = dil > ROW_STRIDE
    n_scratch = 2 if two_stage else 1
    cast_in, (o_ref, tail_ref), cast_out, own_scratch, w_scratch = (
        refs[:n_casts], refs[n_casts:n_casts + 2], refs[n_casts + 2:2 * n_casts + 2],
        refs[2 * n_casts + 2:2 * n_casts + 2 + n_scratch], refs[2 * n_casts + 2 + n_scratch:])
    res_ref = own_scratch[0]
    kind = pl.program_id(0)
    m = pl.program_id(1)
    if w_is_f32:
        w_f32_ref, (w_ref,) = w_ref, w_scratch

        @pl.when(m == 0)
        def _():
            w_ref[...] = w_f32_ref[...].astype(BF16)

    normed = kind < 2
    xn = xn_ref[...]
    sub = tm // dil
    part = tm // ROW_STRIDE
    heads_per_chunk = MXU_COLS // HEAD_DIM
    n_chunks = D_ATTN // MXU_COLS
    for j in range(n_chunks):
        p = jnp.dot(xn, w_ref[:, j * MXU_COLS:(j + 1) * MXU_COLS], preferred_element_type=F32)
        for src_ref, dst_ref in zip(cast_in, cast_out):
            share = src_ref.shape[0] // n_chunks
            dst_ref[j * share:(j + 1) * share, :] = src_ref[j * share:(j + 1) * share, :].astype(BF16)
        for hh in range(heads_per_chunk):
            h = j * heads_per_chunk + hh
            sl = slice(h * HEAD_DIM, (h + 1) * HEAD_DIM)
            ph = p[:, hh * HEAD_DIM:(hh + 1) * HEAD_DIM]
            r = lax.rsqrt(jnp.mean(ph * ph, axis=-1, keepdims=True) + EPS)
            res = (ph * jnp.where(normed, r, 1.0)) * jnp.where(normed, gain_ref[:, sl], 1.0)
            res_ref[h] = res
            if two_stage:
                stage_ref = own_scratch[1].at[h % 2]
                for r1 in range(ROW_STRIDE):
                    stage_ref[r1 * part:(r1 + 1) * part, :] = res_ref[h, pl.ds(r1, part, stride=ROW_STRIDE), :]
            for rr in range(dil):
                if dil == 1:
                    rows = res
                elif two_stage:
                    r1, r2 = rr % ROW_STRIDE, rr // ROW_STRIDE
                    rows = stage_ref[pl.ds(r1 * part + r2, sub, stride=ROW_STRIDE), :]
                else:
                    rows = res_ref[h, pl.ds(rr, sub, stride=dil), :]
                o_ref[rr, :, sl] = rows.astype(o_ref.dtype)

    @pl.when(jnp.logical_and(kind >= 1, m >= m_tail0))
    def _():
        for h in range(N_SLOTS):
            tail_ref[:, h, :] = res_ref[h, tm - tr:tm, :]


class _Cols(NamedTuple):
    arr: jax.Array
    col0: int

    def block(self, col_block, width):
        return col_block - self.col0 // width


def _qkv_proj(xn, w_in, qk_gains, g, dil, out_dtype, tail_rows, casts=()):
    rows = xn.shape[0]
    tm = min(rows, 1024)
    tn = D_ATTN
    tr = min(tail_rows, tm)
    m_tail0 = rows // tm - tail_rows // tr
    grid = (3, rows // tm)
    n_steps = grid[0] * grid[1]
    w_is_f32 = w_in.arr.dtype == F32

    def tail_map(kind, m):
        return (jnp.where(kind >= 1, jnp.maximum(m - m_tail0, 0), 0), jnp.maximum(kind - 1, 0), 0, 0)

    def cast_specs_of(w, col0, n_cols):
        n_blocks = max(d for d in range(1, n_steps + 1) if n_cols % (d * LANES) == 0)
        width = n_cols // n_blocks
        assert col0 % width == 0

        def block(kind, m):
            return jnp.minimum(kind * grid[1] + m, n_blocks - 1)

        return (pl.BlockSpec((w.shape[0], width), lambda kind, m: (0, col0 // width + block(kind, m))),
                pl.BlockSpec((w.shape[0], width), lambda kind, m: (0, block(kind, m))))

    cast_in_specs, cast_out_specs = zip(*[cast_specs_of(*c) for c in casts]) if casts else ((), ())
    assert dil <= ROW_STRIDE or dil == ROW_STRIDE * ROW_STRIDE
    scratch = [pltpu.VMEM((N_SLOTS, tm, HEAD_DIM), F32)]
    if dil > ROW_STRIDE:
        scratch.append(pltpu.VMEM((2, tm, HEAD_DIM), F32))
    if w_is_f32:
        scratch.append(pltpu.VMEM((D_MODEL, tn), BF16))
    return pl.pallas_call(
        functools.partial(_qkv_kernel, dil=dil, tm=tm, tr=tr, m_tail0=m_tail0, n_casts=len(casts), w_is_f32=w_is_f32),
        grid=grid,
        in_specs=[pl.BlockSpec((tm, D_MODEL), lambda kind, m: (m, 0)),
                  pl.BlockSpec((D_MODEL, tn), lambda kind, m: (0, w_in.block(kind * N_GROUPS + g, tn))),
                  pl.BlockSpec((None, 1, tn), lambda kind, m: (jnp.minimum(kind, 1), 0, 0))] + list(cast_in_specs),
        out_specs=[pl.BlockSpec((None, dil, tm // dil, tn), lambda kind, m: (kind, 0, m, 0)),
                   pl.BlockSpec((tr, None, N_SLOTS, HEAD_DIM), tail_map)] + list(cast_out_specs),
        out_shape=[jax.ShapeDtypeStruct((3, dil, rows // dil, tn), out_dtype),
                   jax.ShapeDtypeStruct((tail_rows, 2, N_SLOTS, HEAD_DIM), F32)]
                  + [jax.ShapeDtypeStruct((w.shape[0], n_cols), BF16) for w, _, n_cols in casts],
        scratch_shapes=scratch,
        compiler_params=_params(("arbitrary", "arbitrary")),
        name=f"qkv_proj_g{g}",
    )(xn, w_in.arr, qk_gains, *[w for w, _, _ in casts])


def _branch_b_kernel(xn_ref, wb_ref, wc_ref, wx_ref, wg_ref, cw_ref, *refs, tm, side):
    side_in, (y_ref, h_ref), side_out, (hbuf_ref,), side_scratch = (
        refs[:6], refs[6:8], refs[8:10], refs[10:11], refs[11:])

    @pl.when(pl.program_id(1) == 0)
    def _():
        hbuf_ref[0:SUBLANES, :] = jnp.zeros((SUBLANES, hbuf_ref.shape[1]), F32)

    _attn_side(side, pl.program_id(0) * pl.num_programs(1) + pl.program_id(1), *side_in, *side_out, *side_scratch)
    xn = xn_ref[...]
    pb = jnp.dot(xn, wb_ref[...], preferred_element_type=F32)
    pc = jnp.dot(xn, wc_ref[...], preferred_element_type=F32)
    px = jnp.dot(xn, wx_ref[...], preferred_element_type=F32)
    pg = jnp.dot(xn, wg_ref[...], preferred_element_type=F32)
    h = pc * px
    hbuf_ref[SUBLANES:SUBLANES + tm, :] = h
    hm2 = hbuf_ref[pl.ds(SUBLANES - 2, tm), :]
    hm1 = hbuf_ref[pl.ds(SUBLANES - 1, tm), :]
    last = hbuf_ref[pl.ds(tm, SUBLANES), :]
    h_ref[...] = last
    hbuf_ref[0:SUBLANES, :] = last
    z = cw_ref[0:1, :] * hm2 + cw_ref[1:2, :] * hm1 + cw_ref[2:3, :] * h
    y_ref[...] = (pb * z * (pg * jax.nn.sigmoid(pg))).astype(y_ref.dtype)


def _branch_b(xn, w_in, conv_w, attn):
    rows = xn.shape[0]
    tm = min(rows, 1024)
    tnb = 256
    grid = (D_CONV // tnb, rows // tm)
    side = _attn_side_plan(*attn, n_steps=grid[0] * grid[1], step_of=lambda j, m: j * grid[1] + m)

    def wspec(col0):
        return pl.BlockSpec((D_MODEL, tnb), lambda j, m: (0, w_in.block(col0 // tnb + j, tnb)))

    y, h_last, o, lse = pl.pallas_call(
        functools.partial(_branch_b_kernel, tm=tm, side=side.static),
        grid=grid,
        in_specs=[pl.BlockSpec((tm, D_MODEL), lambda j, m: (m, 0)),
                  wspec(COL_B), wspec(COL_C), wspec(COL_XIN), wspec(COL_GATE_B),
                  pl.BlockSpec((CONV_W, tnb), lambda j, m: (0, j))] + side.in_specs,
        out_specs=[pl.BlockSpec((tm, tnb), lambda j, m: (m, j)),
                   pl.BlockSpec((SUBLANES, tnb), lambda j, m: (0, j))] + side.out_specs,
        out_shape=[jax.ShapeDtypeStruct((rows, D_CONV), BF16),
                   jax.ShapeDtypeStruct((SUBLANES, D_CONV), F32)] + side.out_shape,
        scratch_shapes=[pltpu.VMEM((tm + SUBLANES, tnb), F32)] + side.scratch,
        compiler_params=_params(("arbitrary", "arbitrary")),
        name="branch_b",
    )(xn, w_in.arr, w_in.arr, w_in.arr, w_in.arr, conv_w, *side.args)
    return y, h_last, side.finish(o, lse)


GATE_TILE = 1024
N_GATE_TILES = (2 * D_MODEL + D_ATTN) // GATE_TILE
SILU_TILE = N_GATE_TILES - 1


def _gate_columns(xn, w_ref, o_ref, is_silu):
    for j in range(w_ref.shape[1] // MXU_COLS):
        cs = slice(j * MXU_COLS, (j + 1) * MXU_COLS)
        p = jnp.dot(xn, w_ref[:, cs], preferred_element_type=F32)
        o_ref[:, cs] = jax.nn.sigmoid(p) * jnp.where(is_silu, p, 1.0)


def _gate_weight_block(tile):
    return jnp.where(tile == SILU_TILE, COL_GATE_A // GATE_TILE, COL_MERGE_A // GATE_TILE + tile)


def _gates_kernel(xn_ref, w_ref, *refs, tile0, side):
    side_in, o_ref, side_out, side_scratch = refs[:6], refs[6], refs[7:9], refs[9:]
    _attn_side(side, pl.program_id(0) * pl.num_programs(1) + pl.program_id(1), *side_in, *side_out, *side_scratch)
    _gate_columns(xn_ref[...], w_ref, o_ref, tile0 + pl.program_id(0) == SILU_TILE)


def _gates(xn, w_in, tile0, n_tiles, attn):
    rows = xn.shape[0]
    tm = min(rows, 1024)
    tn = GATE_TILE
    grid = (n_tiles, rows // tm)
    side = _attn_side_plan(*attn, n_steps=grid[0] * grid[1], step_of=lambda n, m: n * grid[1] + m)

    def wmap(n, m):
        return (0, w_in.block(_gate_weight_block(tile0 + n), tn))

    gates, o, lse = pl.pallas_call(
        functools.partial(_gates_kernel, tile0=tile0, side=side.static),
        grid=grid,
        in_specs=[pl.BlockSpec((tm, D_MODEL), lambda n, m: (m, 0)), pl.BlockSpec((D_MODEL, tn), wmap)] + side.in_specs,
        out_specs=[pl.BlockSpec((tm, tn), lambda n, m: (m, n))] + side.out_specs,
        out_shape=[jax.ShapeDtypeStruct((rows, n_tiles * tn), F32)] + side.out_shape,
        scratch_shapes=side.scratch,
        compiler_params=_params(("arbitrary", "arbitrary")),
        name="gates",
    )(xn, w_in.arr, *side.args)
    return gates, side.finish(o, lse)


PROJ_TILE = 1024
N_QKV_TILES = COL_GATE_A // PROJ_TILE
N_PROJ_TILES = COL_MERGE_B // PROJ_TILE + D_MODEL // PROJ_TILE
TILE_GATE_A, TILE_B, TILE_C, TILE_XIN, TILE_GATE_B, TILE_MERGE_A = (
    c // PROJ_TILE for c in (COL_GATE_A, COL_B, COL_C, COL_XIN, COL_GATE_B, COL_MERGE_A))


def _sample_proj_kernel(xn_ref, wq_ref, wr_ref, gain_ref, cw_ref, st_ref,
                        q_ref, kv0_ref, kv1_ref, kv2_ref, gates_ref, y_ref, h_ref, bcx_ref):
    n = pl.program_id(0)
    kv_refs = (kv0_ref, kv1_ref, kv2_ref)
    heads = [slice(h * HEAD_DIM, (h + 1) * HEAD_DIM) for h in range(N_SLOTS)]

    def project(w_ref):
        return jnp.dot(xn_ref[...], w_ref[...], preferred_element_type=F32)

    def store_heads(dst_ref, p, gain_row):
        for h, sl in enumerate(heads):
            ph = p[:, sl]
            if gain_row is not None:
                r = lax.rsqrt(jnp.mean(ph * ph, axis=-1, keepdims=True) + EPS)
                ph = (ph * r) * gain_ref[gain_row, :, sl]
            dst_ref[:, h, :] = ph

    for t in range(N_PROJ_TILES):
        @pl.when(n == t)
        def _(t=t):
            if t < N_QKV_TILES:
                kind, g = divmod(t, N_GROUPS)
                dst_ref = q_ref if kind == 0 else kv_refs[g]
                store_heads(dst_ref, project(wq_ref), kind if kind < 2 else None)
            elif t == TILE_GATE_A:
                p = project(wr_ref)
                gates_ref[...] = p * jax.nn.sigmoid(p)
            elif t in (TILE_B, TILE_C, TILE_XIN):
                bcx_ref[t - TILE_B] = project(wr_ref)
            elif t == TILE_GATE_B:
                pg = project(wr_ref)
                h = bcx_ref[1] * bcx_ref[2]
                z = (cw_ref[0:1, :] * st_ref[:, 0:D_CONV] + cw_ref[1:2, :] * st_ref[:, D_CONV:2 * D_CONV]
                     + cw_ref[2:3, :] * h)
                h_ref[...] = h
                y_ref[...] = (bcx_ref[0] * z * (pg * jax.nn.sigmoid(pg))).astype(y_ref.dtype)
            else:
                gates_ref[...] = jax.nn.sigmoid(project(wr_ref))


def _sample_proj(xn, w_qkv, w_rest, qk_gains, conv_w, state):
    rows = xn.shape[0]
    tn = PROJ_TILE
    assert w_qkv.col0 == 0 and w_rest.col0 == COL_GATE_A

    def const(shape):
        return pl.BlockSpec(shape, lambda n: (0,) * len(shape))

    def gate_block(n):
        return jnp.where(n >= TILE_MERGE_A, n - TILE_MERGE_A, SILU_TILE)

    head_block = (rows, None, N_SLOTS, HEAD_DIM)
    kv_specs = [pl.BlockSpec(head_block, lambda n, g=g: (0, jnp.where(n >= 2 * N_GROUPS + g, 1, 0), 0, 0))
                for g in range(N_GROUPS)]
    return pl.pallas_call(
        _sample_proj_kernel,
        grid=(N_PROJ_TILES,),
        in_specs=[const((rows, D_MODEL)),
                  pl.BlockSpec((D_MODEL, tn), lambda n: (0, jnp.minimum(n, N_QKV_TILES - 1))),
                  pl.BlockSpec((D_MODEL, tn), lambda n: (0, jnp.maximum(n - N_QKV_TILES, 0))),
                  const((2, 1, D_ATTN)), const((CONV_W, D_CONV)), const((rows, (CONV_W - 1) * D_CONV))],
        out_specs=[pl.BlockSpec(head_block, lambda n: (0, jnp.minimum(n, N_GROUPS - 1), 0, 0))] + kv_specs
                  + [pl.BlockSpec((rows, GATE_TILE), lambda n: (0, gate_block(n))),
                     const((rows, D_CONV)), const((rows, D_CONV))],
        out_shape=[jax.ShapeDtypeStruct((rows, N_GROUPS, N_SLOTS, HEAD_DIM), F32)]
                  + [jax.ShapeDtypeStruct((rows, 2, N_SLOTS, HEAD_DIM), F32)] * N_GROUPS
                  + [jax.ShapeDtypeStruct((rows, N_GATE_TILES * GATE_TILE), F32),
                     jax.ShapeDtypeStruct((rows, D_CONV), BF16), jax.ShapeDtypeStruct((rows, D_CONV), F32)],
        scratch_shapes=[pltpu.VMEM((3, rows, D_CONV), F32)],
        compiler_params=_params(("arbitrary",)),
        name="sample_proj",
    )(xn, w_qkv.arr, w_rest.arr, qk_gains, conv_w, state.reshape(rows, (CONV_W - 1) * D_CONV))


def _t5_bucket(dist):
    d = np.maximum(dist, 1).astype(np.float32)
    large = MAX_EXACT + (np.log(d / np.float32(MAX_EXACT)) / np.float32(math.log(MAX_DISTANCE / MAX_EXACT))
                         * np.float32(NUM_BUCKETS - MAX_EXACT)).astype(np.int32)
    large = np.minimum(large, NUM_BUCKETS - 1)
    return np.where(dist < MAX_EXACT, dist, large)


def _bias_by_offset(rel_bias):
    steps = N_BACK - np.arange(N_BACK + 1)
    idx = np.stack([_t5_bucket(d * steps) for d in DILATIONS])
    g_idx = np.arange(N_GROUPS)[:, None]
    vals = jnp.transpose(rel_bias[idx, g_idx], (0, 2, 1)).astype(F32)
    return jnp.pad(vals, ((0, 0), (0, 0), (0, 2 * BLOCK - N_BACK - 1)), constant_values=NEG_INF)


def _sample_bias(cvec):
    kv_head = np.arange(N_BACK * 2 * N_SLOTS) % (2 * N_SLOTS)
    own_k_row = kv_head[None, :] == np.arange(N_SLOTS)[:, None]
    per_row = jnp.repeat(cvec[:, :, :N_BACK], 2 * N_SLOTS, axis=2)
    return jnp.where(own_k_row[None], per_row, NEG_INF), cvec[:, :, N_BACK:N_BACK + 1]


class _SideStatic(NamedTuple):
    per_step: int
    nb: int
    steps: int


class _SidePlan(NamedTuple):
    static: _SideStatic
    in_specs: list
    args: list
    out_specs: list
    out_shape: list
    scratch: list
    finish: Callable


def _attn_side(side, step, cvec_ref, q_ref, kp_ref, kc_ref, vp_ref, vc_ref, o_ref, lse_ref, bias_ref, s_ref, p_ref):
    @pl.when(step == 0)
    def _():
        for h in range(N_SLOTS):
            row = jnp.broadcast_to(cvec_ref[h:h + 1, :], (BLOCK, 2 * BLOCK))
            full = pltpu.roll(row, 0, 1, stride=1, stride_axis=0)
            bias_ref[1, h] = full
            bias_ref[0, h, :, :BLOCK] = jnp.full((BLOCK, BLOCK), NEG_INF, F32)
            bias_ref[0, h, :, BLOCK:] = full[:, BLOCK:]

    nt = (((1,), (1,)), ((), ()))
    heads = [slice(h * HEAD_DIM, (h + 1) * HEAD_DIM) for h in range(N_SLOTS)]
    unit0 = jnp.minimum(step, side.steps - 1) * side.per_step
    lane = lax.broadcasted_iota(jnp.int32, (BLOCK, LANES), 1)
    for u in range(side.per_step):
        rows = slice(u * BLOCK, (u + 1) * BLOCK)
        prev = slice((u - 1) * BLOCK, u * BLOCK)
        for h, sl in enumerate(heads):
            q = q_ref[rows, sl]
            k_prev = kp_ref[:, sl] if u == 0 else kc_ref[prev, sl]
            s_ref[u, h, :, :BLOCK] = lax.dot_general(q, k_prev, nt, preferred_element_type=F32)
            s_ref[u, h, :, BLOCK:] = lax.dot_general(q, kc_ref[rows, sl], nt, preferred_element_type=F32)
    ls, lses = [], []
    for u in range(side.per_step):
        has_prev = jnp.minimum((unit0 + u) % side.nb, 1)
        sc = s_ref[u] * SCALE + bias_ref[has_prev]
        mx = jnp.max(jnp.maximum(sc[:, :, :BLOCK], sc[:, :, BLOCK:]), axis=-1, keepdims=True)
        p = jnp.exp(sc - mx)
        l = jnp.sum(p[:, :, :BLOCK] + p[:, :, BLOCK:], axis=-1, keepdims=True)
        p_ref[u] = p.astype(BF16)
        ls.append(l)
        lses.append(mx + jnp.log(l))
    for u in range(side.per_step):
        rows = slice(u * BLOCK, (u + 1) * BLOCK)
        prev = slice((u - 1) * BLOCK, u * BLOCK)
        lse_all = jnp.zeros((BLOCK, LANES), F32)
        for h, sl in enumerate(heads):
            v_prev = vp_ref[:, sl] if u == 0 else vc_ref[prev, sl]
            o = (jnp.dot(p_ref[u, h, :, :BLOCK], v_prev, preferred_element_type=F32)
                 + jnp.dot(p_ref[u, h, :, BLOCK:], vc_ref[rows, sl], preferred_element_type=F32))
            o_ref[rows, sl] = o / ls[u][h]
            lse_all = jnp.where(lane == h, lses[u][h], lse_all)
        lse_ref[rows, :] = lse_all


def _attn_side_plan(qkv, cvec, g, n_steps, step_of):
    _, dil, ls, _ = qkv.shape
    nb = ls // BLOCK
    units = dil * nb
    per_step = min(d for d in range(1, units + 1) if units % d == 0 and d * n_steps >= units)
    static = _SideStatic(per_step=per_step, nb=nb, steps=units // per_step)
    flat = qkv.reshape(3, dil * ls, D_ATTN)
    rows = per_step * BLOCK

    def cur(kind):
        return pl.BlockSpec((None, rows, D_ATTN), lambda *idx: (kind, jnp.minimum(step_of(*idx), static.steps - 1), 0))

    def prev(kind):
        def index(*idx):
            first_unit = jnp.minimum(step_of(*idx), static.steps - 1) * per_step
            return (kind, jnp.maximum(first_unit - 1, 0), 0)
        return pl.BlockSpec((None, BLOCK, D_ATTN), index)

    def out_spec(width):
        return pl.BlockSpec((rows, width), lambda *idx: (jnp.minimum(step_of(*idx), static.steps - 1), 0))

    return _SidePlan(
        static=static,
        in_specs=[pl.BlockSpec((None, N_SLOTS, 2 * BLOCK), lambda *idx: (g, 0, 0)),
                  cur(0), prev(1), cur(1), prev(2), cur(2)],
        args=[cvec, flat, flat, flat, flat, flat],
        out_specs=[out_spec(D_ATTN), out_spec(LANES)],
        out_shape=[jax.ShapeDtypeStruct((dil * ls, D_ATTN), F32), jax.ShapeDtypeStruct((dil * ls, LANES), F32)],
        scratch=[pltpu.VMEM((2, N_SLOTS, BLOCK, 2 * BLOCK), F32),
                 pltpu.VMEM((per_step, N_SLOTS, BLOCK, 2 * BLOCK), F32),
                 pltpu.VMEM((per_step, N_SLOTS, BLOCK, 2 * BLOCK), BF16)],
        finish=lambda o, lse: (o.reshape(dil, ls, D_ATTN), lse.reshape(dil, ls, LANES)),
    )


def _attn_sample_kernel(bias_ref, nbias_ref, q0_ref, q1_ref, q2_ref, kv0_ref, kv1_ref, kv2_ref,
                        c0_ref, c1_ref, c2_ref, *refs, tb, gate_tile):
    if gate_tile is None:
        o_ref, slab_ref, s_ref, p_ref, og_ref = refs
    else:
        xn_ref, w_ref, o_ref, gate_ref, slab_ref, s_ref, p_ref, og_ref = refs
        _gate_columns(xn_ref[...], w_ref, gate_ref, gate_tile == SILU_TILE)
    rows = pl.ds(pl.program_id(0) * tb, tb)
    nt = (((1,), (1,)), ((), ()))
    kv_rows = 2 * N_SLOTS
    slab_rows = N_BACK * kv_rows
    q_refs = (q0_ref, q1_ref, q2_ref)
    kv_refs = (kv0_ref, kv1_ref, kv2_ref)
    c_refs = (c0_ref, c1_ref, c2_ref)

    def rounded(x):
        return x.astype(BF16).astype(F32)

    qs = [q_ref[rows] for q_ref in q_refs]
    for g in range(N_GROUPS):
        for t in range(tb):
            slab_ref[g, t] = c_refs[g][t].reshape(slab_rows, HEAD_DIM).astype(BF16)
            s_ref[g, t] = lax.dot_general(qs[g][t].astype(BF16), slab_ref[g, t], nt, preferred_element_type=F32)
    p_news, ls, lses = [], [], []
    for g in range(N_GROUPS):
        s_n = jnp.sum(rounded(qs[g]) * rounded(kv_refs[g][rows, 0]), axis=-1, keepdims=True)
        bias = bias_ref[g][None]
        sc_c = jnp.where(bias > NEG_INF, s_ref[g] * SCALE + bias, NEG_INF)
        sc_n = s_n * SCALE + nbias_ref[g][None]
        mx = jnp.maximum(jnp.max(sc_c, axis=-1, keepdims=True), sc_n)
        p_c = jnp.exp(sc_c - mx)
        p_n = jnp.exp(sc_n - mx)
        l = jnp.sum(p_c, axis=-1, keepdims=True) + p_n
        p_ref[g] = pltpu.roll(p_c, N_SLOTS, 2).astype(BF16)
        p_news.append(p_n)
        ls.append(l)
        lses.append(mx + jnp.log(l))
    for g in range(N_GROUPS):
        for t in range(tb):
            og_ref[g, t] = jnp.dot(p_ref[g, t], slab_ref[g, t], preferred_element_type=F32)
    outs = [(og_ref[g] + rounded(p_news[g]) * rounded(kv_refs[g][rows, 1])) / ls[g] for g in range(N_GROUPS)]
    top = jnp.maximum(jnp.maximum(lses[0], lses[1]), lses[2])
    es = [jnp.exp(x - top) for x in lses]
    tot = es[0] + es[1] + es[2]
    o_ref[rows] = (es[0] / tot) * outs[0] + (es[1] / tot) * outs[1] + (es[2] / tot) * outs[2]


def _attn_sample(qs, kvs, caches, bias, nbias, gate_rider=None):
    db = qs.shape[0]
    tb = 4
    kv_rows = 2 * N_SLOTS
    n_steps = db // tb
    rider_in_specs, rider_args, rider_out_specs, rider_out_shape, gate_tile = [], [], [], [], None
    if gate_rider is not None:
        xn, w_in, gate_tile = gate_rider
        rows_per_step = xn.shape[0] // n_steps
        rider_in_specs = [pl.BlockSpec((rows_per_step, D_MODEL), lambda s: (s, 0)),
                          pl.BlockSpec((D_MODEL, GATE_TILE),
                                       lambda s: (0, w_in.block(_gate_weight_block(gate_tile), GATE_TILE)))]
        rider_args = [xn, w_in.arr]
        rider_out_specs = [pl.BlockSpec((rows_per_step, GATE_TILE), lambda s: (s, 0))]
        rider_out_shape = [jax.ShapeDtypeStruct((xn.shape[0], GATE_TILE), F32)]
    views, specs = [], []
    for g, c in enumerate(caches):
        dil = DILATIONS[g]
        lc = c.shape[1]
        views.append(c.reshape(db, lc // dil, dil * kv_rows, HEAD_DIM))
        specs.append(pl.BlockSpec((tb, N_BACK, kv_rows, HEAD_DIM), lambda s: (s, 0, 0, 0)))
    q_specs = [pl.BlockSpec((db, None, N_SLOTS, HEAD_DIM), lambda s, g=g: (0, g, 0, 0)) for g in range(N_GROUPS)]
    kv_spec = pl.BlockSpec((db, 2, N_SLOTS, HEAD_DIM), lambda s: (0, 0, 0, 0))
    outs = pl.pallas_call(
        functools.partial(_attn_sample_kernel, tb=tb, gate_tile=gate_tile),
        grid=(n_steps,),
        in_specs=[pl.BlockSpec((N_GROUPS, N_SLOTS, N_BACK * kv_rows), lambda s: (0, 0, 0)),
                  pl.BlockSpec((N_GROUPS, N_SLOTS, 1), lambda s: (0, 0, 0)),
                  *q_specs, kv_spec, kv_spec, kv_spec] + specs + rider_in_specs,
        out_specs=[pl.BlockSpec((db, N_SLOTS, HEAD_DIM), lambda s: (0, 0, 0))] + rider_out_specs,
        out_shape=[jax.ShapeDtypeStruct((db, N_SLOTS, HEAD_DIM), F32)] + rider_out_shape,
        scratch_shapes=[pltpu.VMEM((N_GROUPS, tb, N_BACK * kv_rows, HEAD_DIM), BF16),
                        pltpu.VMEM((N_GROUPS, tb, N_SLOTS, N_BACK * kv_rows), F32),
                        pltpu.VMEM((N_GROUPS, tb, N_SLOTS, N_BACK * kv_rows), BF16),
                        pltpu.VMEM((N_GROUPS, tb, N_SLOTS, HEAD_DIM), F32)],
        compiler_params=_params(("arbitrary",)),
        name="attn_sample",
    )(bias, nbias, qs, qs, qs, *kvs, *views, *rider_args)
    return outs[0] if gate_rider is None else outs


def _out_kernel(*refs, tm, n_groups):
    o_refs = refs[:n_groups]
    lse_refs = refs[n_groups:2 * n_groups] if n_groups > 1 else ()
    rest = refs[len(o_refs) + len(lse_refs):]
    def project(a):
        ya = jnp.dot(a, woa_ref[...], preferred_element_type=F32)
        yb = jnp.dot(yb_ref[...], wob_ref[...], preferred_element_type=F32)
        merged = sma_ref[...] * ya + smb_ref[...] * yb
        y_ref[...] = x_ref[...] + jnp.dot(merged.astype(BF16), wo_ref[...], preferred_element_type=F32)

    if n_groups > 1:
        sga_ref, sma_ref, smb_ref, yb_ref, x_ref, woa_ref, wob_ref, wo_ref, y_ref, a2_ref, og_ref, lg_ref = rest
        step = pl.program_id(0)

        @pl.when(step == 0)
        def _():
            a2_ref[1] = jnp.zeros(a2_ref.shape[1:], BF16)

        project(a2_ref[(step + 1) % 2])
        a_ref = a2_ref.at[step % 2]
        for g in range(n_groups):
            dil = DILATIONS[g]
            sub = tm // dil
            for r in range(dil):
                rows = slice(None) if dil == 1 else pl.ds(r, sub, stride=dil)
                lg_ref[g, rows, :] = lse_refs[g][r]
                for h in range(N_SLOTS):
                    og_ref[g, h, rows, :] = o_refs[g][r, :, h * HEAD_DIM:(h + 1) * HEAD_DIM]
        lses = [lg_ref[g] for g in range(n_groups)]
        top = functools.reduce(jnp.maximum, lses)
        es = [jnp.exp(x - top) for x in lses]
        tot = functools.reduce(lambda a, b: a + b, es)
        ws = [e / tot for e in es]
        for h in range(N_SLOTS):
            sl = slice(h * HEAD_DIM, (h + 1) * HEAD_DIM)
            o = functools.reduce(lambda a, b: a + b,
                                 [ws[g][:, h:h + 1] * og_ref[g, h] for g in range(n_groups)])
            a_ref[:, sl] = (o * sga_ref[:, sl]).astype(BF16)
    else:
        sga_ref, sma_ref, smb_ref, yb_ref, x_ref, woa_ref, wob_ref, wo_ref, y_ref = rest
        project((o_refs[0][...] * sga_ref[...]).astype(BF16))


def _out_proj(os_, lses, gates, yb_in, x, w_out_a, w_out_b, w_o):
    rows = x.shape[0]

    tm = min(rows, 256)
    n_groups = len(os_)
    n_blocks = rows // tm
    pipelined = n_groups > 1
    n_steps = n_blocks + 1 if pipelined else n_blocks

    def comb_block(s):
        return jnp.minimum(s, n_blocks - 1)

    def proj_block(s):
        return jnp.maximum(s - 1, 0) if pipelined else s

    def gate_operand(tile, width, block_of):
        arr, tile0 = next((a, t0) for a, t0 in gates if t0 <= tile < t0 + a.shape[1] // GATE_TILE)
        assert (tile - tile0) * GATE_TILE % width == 0
        return arr, pl.BlockSpec((tm, width), lambda s: (block_of(s), (tile - tile0) * GATE_TILE // width))

    def row_spec(width, block_of):
        return pl.BlockSpec((tm, width), lambda s: (block_of(s), 0))

    def sub_spec(dil, width):
        return pl.BlockSpec((dil, tm // dil, width), lambda s: (0, comb_block(s), 0))

    def const_spec(shape):
        return pl.BlockSpec(shape, lambda s: (0, 0), pipeline_mode=pl.Buffered(1))

    if pipelined:
        in_specs = ([sub_spec(DILATIONS[g], D_ATTN) for g in range(n_groups)]
                    + [sub_spec(DILATIONS[g], LANES) for g in range(n_groups)])
        scratch = [pltpu.VMEM((2, tm, D_ATTN), BF16),
                   pltpu.VMEM((n_groups, N_SLOTS, tm, HEAD_DIM), F32), pltpu.VMEM((n_groups, tm, LANES), F32)]
    else:
        in_specs = [row_spec(D_ATTN, comb_block)]
        scratch = []
    sga, sga_spec = gate_operand(SILU_TILE, D_ATTN, comb_block)
    sma, sma_spec = gate_operand(0, D_MODEL, proj_block)
    smb, smb_spec = gate_operand(D_MODEL // GATE_TILE, D_MODEL, proj_block)
    in_specs += [sga_spec, sma_spec, smb_spec, row_spec(D_CONV, proj_block), row_spec(D_MODEL, proj_block),
                 const_spec((D_ATTN, D_MODEL)), const_spec((D_CONV, D_MODEL)), const_spec((D_MODEL, D_MODEL))]
    return pl.pallas_call(
        functools.partial(_out_kernel, tm=tm, n_groups=n_groups),
        grid=(n_steps,),
        in_specs=in_specs,
        out_specs=row_spec(D_MODEL, proj_block),
        out_shape=jax.ShapeDtypeStruct((rows, D_MODEL), F32),
        scratch_shapes=scratch,
        compiler_params=_params(("arbitrary",)),
        name="out_proj",
    )(*os_, *lses, sga, sma, smb, yb_in, x, w_out_a, w_out_b, w_o)


def kernel(x_prompt, x_sample, cache_kv_w128, cache_kv_w512, cache_kv_w2048, state_conv, norm_gain, w_in,
           q_norm_gain, k_norm_gain, rel_bias, conv_w, w_out_a, w_out_b, w_o):
    seq = x_prompt.shape[1]
    db = x_sample.shape[0]
    xp = x_prompt.reshape(seq, D_MODEL)
    xs = x_sample.reshape(db, D_MODEL)
    caches = (cache_kv_w128, cache_kv_w512, cache_kv_w2048)
    qk_gains = jnp.stack([jnp.tile(q_norm_gain, N_SLOTS), jnp.tile(k_norm_gain, N_SLOTS)]).reshape(2, 1, D_ATTN)
    cvec = _bias_by_offset(rel_bias)

    xn = _pre_norm(xp, norm_gain)
    qkv0, kv_p0, w_qkv = _qkv_proj(xn, _Cols(w_in, 0), qk_gains, 0, DILATIONS[0], BF16, min(WINDOWS[0], seq),
                                   casts=((w_in, 0, COL_GATE_A),))
    w_qkv = _Cols(w_qkv, 0)
    qkv1, kv_p1, w_rest, woa, wob, wo = _qkv_proj(
        xn, w_qkv, qk_gains, 1, DILATIONS[1], BF16, min(WINDOWS[1], seq),
        casts=((w_in, COL_GATE_A, w_in.shape[1] - COL_GATE_A),
               (w_out_a, 0, D_MODEL), (w_out_b, 0, D_MODEL), (w_o, 0, D_MODEL)))
    w_rest = _Cols(w_rest, COL_GATE_A)
    qkv2, kv_p2 = _qkv_proj(xn, w_qkv, qk_gains, 2, DILATIONS[2], BF16, min(WINDOWS[2], seq))
    qkvs, kv_p = (qkv0, qkv1, qkv2), (kv_p0, kv_p1, kv_p2)
    xn_s = _pre_norm(xs, norm_gain)
    q_s, *kv_s, gates_s, yb_in_s, h_s = _sample_proj(xn_s, w_qkv, w_rest, qk_gains, conv_w, state_conv)
    o_s, gates_c = _attn_sample(q_s, kv_s, caches, *_sample_bias(cvec), gate_rider=(xn, w_rest, SILU_TILE))
    y_sample = _out_proj((o_s.reshape(db, D_ATTN),), (), [(gates_s, 0)], yb_in_s, xs, woa, wob, wo)
    y_sample = y_sample.reshape(db, 1, D_MODEL)

    gates_a, attn0 = _gates(xn, w_rest, 0, 2, attn=(qkvs[0], cvec, 0))
    gates_b, attn1 = _gates(xn, w_rest, 2, 2, attn=(qkvs[1], cvec, 1))
    yb_in, h_last, attn2 = _branch_b(xn, w_rest, conv_w, attn=(qkvs[2], cvec, 2))
    gates = [(gates_a, 0), (gates_b, 2), (gates_c, SILU_TILE)]
    os_, lses = zip(attn0, attn1, attn2)
    y_prompt = _out_proj(os_, lses, gates, yb_in, xp, woa, wob, wo).reshape(1, seq, D_MODEL)
    conv_p = h_last[SUBLANES - (CONV_W - 1):][None]
    conv_s = jnp.stack([state_conv[:, CONV_W - 2], h_s], axis=1)

    return (y_prompt, y_sample, kv_p[0][None], kv_p[1][None], kv_p[2][None], conv_p,
            kv_s[0][:, None], kv_s[1][:, None], kv_s[2][:, None], conv_s)
```

```python
import functools
import math
from typing import Callable, NamedTuple

import numpy as np
import jax
import jax.numpy as jnp
from jax import lax
from jax.experimental import pallas as pl
from jax.experimental.pallas import tpu as pltpu

D_MODEL = 2048
N_GROUPS = 3
DILATIONS = (1, 4, 16)
N_BACK = 128
WINDOWS = (128, 512, 2048)
N_SLOTS = 8
HEAD_DIM = 128
D_ATTN = N_SLOTS * HEAD_DIM
QKV_COLS = N_GROUPS * D_ATTN
D_CONV = D_MODEL // 2
CONV_W = 3
BLOCK = N_BACK
NUM_BUCKETS = 32
MAX_EXACT = NUM_BUCKETS // 2
MAX_DISTANCE = 2048
EPS = 1e-6
SCALE = HEAD_DIM ** -0.5

COL_K = QKV_COLS
COL_V = 2 * QKV_COLS
COL_GATE_A = 3 * QKV_COLS
COL_B = COL_GATE_A + D_ATTN
COL_C = COL_B + D_CONV
COL_XIN = COL_C + D_CONV
COL_GATE_B = COL_XIN + D_CONV
COL_MERGE_A = COL_GATE_B + D_CONV
COL_MERGE_B = COL_MERGE_A + D_MODEL

SUBLANES = 8
LANES = 128
MXU_COLS = 256
VMEM_LIMIT_BYTES = 56 * 1024 * 1024

BF16 = jnp.bfloat16
F32 = jnp.float32
NEG_INF = float("-inf")


def _params(semantics):
    return pltpu.CompilerParams(dimension_semantics=semantics, vmem_limit_bytes=VMEM_LIMIT_BYTES)


def _norm_kernel(x_ref, g_ref, o_ref):
    x = x_ref[...]
    r = lax.rsqrt(jnp.mean(x * x, axis=-1, keepdims=True) + EPS)
    o_ref[...] = ((x * r) * g_ref[...]).astype(o_ref.dtype)


def _pre_norm(x, gain):
    rows = x.shape[0]
    tm = min(rows, 512)
    return pl.pallas_call(
        _norm_kernel,
        grid=(rows // tm,),
        in_specs=[pl.BlockSpec((tm, D_MODEL), lambda m: (m, 0)),
                  pl.BlockSpec((1, D_MODEL), lambda m: (0, 0))],
        out_specs=pl.BlockSpec((tm, D_MODEL), lambda m: (m, 0)),
        out_shape=jax.ShapeDtypeStruct((rows, D_MODEL), BF16),
        compiler_params=_params(("arbitrary",)),
        name="pre_norm",
    )(x, gain.reshape(1, D_MODEL))


ROW_STRIDE = 4


def _qkv_kernel(xn_ref, w_ref, gain_ref, *refs, dil, tm, tr, m_tail0, n_casts, w_is_f32):
    two_stage = dil > ROW_STRIDE
    n_scratch = 2 if two_stage else 1
    cast_in, (o_ref, tail_ref), cast_out, own_scratch, w_scratch = (
        refs[:n_casts], refs[n_casts:n_casts + 2], refs[n_casts + 2:2 * n_casts + 2],
        refs[2 * n_casts + 2:2 * n_casts + 2 + n_scratch], refs[2 * n_casts + 2 + n_scratch:])
    res_ref = own_scratch[0]
    kind = pl.program_id(0)
    m = pl.program_id(1)
    if w_is_f32:
        w_f32_ref, (w_ref,) = w_ref, w_scratch

        @pl.when(m == 0)
        def _():
            w_ref[...] = w_f32_ref[...].astype(BF16)

    normed = kind < 2
    xn = xn_ref[...]
    sub = tm // dil
    part = tm // ROW_STRIDE
    heads_per_chunk = MXU_COLS // HEAD_DIM
    n_chunks = D_ATTN // MXU_COLS
    for j in range(n_chunks):
        p = jnp.dot(xn, w_ref[:, j * MXU_COLS:(j + 1) * MXU_COLS], preferred_element_type=F32)
        for src_ref, dst_ref in zip(cast_in, cast_out):
            share = src_ref.shape[0] // n_chunks
            dst_ref[j * share:(j + 1) * share, :] = src_ref[j * share:(j + 1) * share, :].astype(BF16)
        for hh in range(heads_per_chunk):
            h = j * heads_per_chunk + hh
            sl = slice(h * HEAD_DIM, (h + 1) * HEAD_DIM)
            ph = p[:, hh * HEAD_DIM:(hh + 1) * HEAD_DIM]
            r = lax.rsqrt(jnp.mean(ph * ph, axis=-1, keepdims=True) + EPS)
            res = (ph * jnp.where(normed, r, 1.0)) * jnp.where(normed, gain_ref[:, sl], 1.0)
            res_ref[h] = res
            if two_stage:
                stage_ref = own_scratch[1].at[h % 2]
                for r1 in range(ROW_STRIDE):
                    stage_ref[r1 * part:(r1 + 1) * part, :] = res_ref[h, pl.ds(r1, part, stride=ROW_STRIDE), :]
            for rr in range(dil):
                if dil == 1:
                    rows = res
                elif two_stage:
                    r1, r2 = rr % ROW_STRIDE, rr // ROW_STRIDE
                    rows = stage_ref[pl.ds(r1 * part + r2, sub, stride=ROW_STRIDE), :]
                else:
                    rows = res_ref[h, pl.ds(rr, sub, stride=dil), :]
                o_ref[rr, :, sl] = rows.astype(o_ref.dtype)

    @pl.when(jnp.logical_and(kind >= 1, m >= m_tail0))
    def _():
        for h in range(N_SLOTS):
            tail_ref[:, h, :] = res_ref[h, tm - tr:tm, :]


class _Cols(NamedTuple):
    arr: jax.Array
    col0: int

    def block(self, col_block, width):
        return col_block - self.col0 // width


def _qkv_proj(xn, w_in, qk_gains, g, dil, out_dtype, tail_rows, casts=()):
    rows = xn.shape[0]
    tm = min(rows, 1024)
    tn = D_ATTN
    tr = min(tail_rows, tm)
    m_tail0 = rows // tm - tail_rows // tr
    grid = (3, rows // tm)
    n_steps = grid[0] * grid[1]
    w_is_f32 = w_in.arr.dtype == F32

    def tail_map(kind, m):
        return (jnp.where(kind >= 1, jnp.maximum(m - m_tail0, 0), 0), jnp.maximum(kind - 1, 0), 0, 0)

    def cast_specs_of(w, col0, n_cols):
        n_blocks = max(d for d in range(1, n_steps + 1) if n_cols % (d * LANES) == 0)
        width = n_cols // n_blocks
        assert col0 % width == 0

        def block(kind, m):
            return jnp.minimum(kind * grid[1] + m, n_blocks - 1)

        return (pl.BlockSpec((w.shape[0], width), lambda kind, m: (0, col0 // width + block(kind, m))),
                pl.BlockSpec((w.shape[0], width), lambda kind, m: (0, block(kind, m))))

    cast_in_specs, cast_out_specs = zip(*[cast_specs_of(*c) for c in casts]) if casts else ((), ())
    assert dil <= ROW_STRIDE or dil == ROW_STRIDE * ROW_STRIDE
    scratch = [pltpu.VMEM((N_SLOTS, tm, HEAD_DIM), F32)]
    if dil > ROW_STRIDE:
        scratch.append(pltpu.VMEM((2, tm, HEAD_DIM), F32))
    if w_is_f32:
        scratch.append(pltpu.VMEM((D_MODEL, tn), BF16))
    return pl.pallas_call(
        functools.partial(_qkv_kernel, dil=dil, tm=tm, tr=tr, m_tail0=m_tail0, n_casts=len(casts), w_is_f32=w_is_f32),
        grid=grid,
        in_specs=[pl.BlockSpec((tm, D_MODEL), lambda kind, m: (m, 0)),
                  pl.BlockSpec((D_MODEL, tn), lambda kind, m: (0, w_in.block(kind * N_GROUPS + g, tn))),
                  pl.BlockSpec((None, 1, tn), lambda kind, m: (jnp.minimum(kind, 1), 0, 0))] + list(cast_in_specs),
        out_specs=[pl.BlockSpec((None, dil, tm // dil, tn), lambda kind, m: (kind, 0, m, 0)),
                   pl.BlockSpec((tr, None, N_SLOTS, HEAD_DIM), tail_map)] + list(cast_out_specs),
        out_shape=[jax.ShapeDtypeStruct((3, dil, rows // dil, tn), out_dtype),
                   jax.ShapeDtypeStruct((tail_rows, 2, N_SLOTS, HEAD_DIM), F32)]
                  + [jax.ShapeDtypeStruct((w.shape[0], n_cols), BF16) for w, _, n_cols in casts],
        scratch_shapes=scratch,
        compiler_params=_params(("arbitrary", "arbitrary")),
        name=f"qkv_proj_g{g}",
    )(xn, w_in.arr, qk_gains, *[w for w, _, _ in casts])


def _branch_b_kernel(xn_ref, wb_ref, wc_ref, wx_ref, wg_ref, cw_ref, *refs, tm, side):
    side_in, (y_ref, h_ref), side_out, (hbuf_ref,), side_scratch = (
        refs[:6], refs[6:8], refs[8:10], refs[10:11], refs[11:])

    @pl.when(pl.program_id(1) == 0)
    def _():
        hbuf_ref[0:SUBLANES, :] = jnp.zeros((SUBLANES, hbuf_ref.shape[1]), F32)

    _attn_side(side, pl.program_id(0) * pl.num_programs(1) + pl.program_id(1), *side_in, *side_out, *side_scratch)
    xn = xn_ref[...]
    pb = jnp.dot(xn, wb_ref[...], preferred_element_type=F32)
    pc = jnp.dot(xn, wc_ref[...], preferred_element_type=F32)
    px = jnp.dot(xn, wx_ref[...], preferred_element_type=F32)
    pg = jnp.dot(xn, wg_ref[...], preferred_element_type=F32)
    h = pc * px
    hbuf_ref[SUBLANES:SUBLANES + tm, :] = h
    hm2 = hbuf_ref[pl.ds(SUBLANES - 2, tm), :]
    hm1 = hbuf_ref[pl.ds(SUBLANES - 1, tm), :]
    last = hbuf_ref[pl.ds(tm, SUBLANES), :]
    h_ref[...] = last
    hbuf_ref[0:SUBLANES, :] = last
    z = cw_ref[0:1, :] * hm2 + cw_ref[1:2, :] * hm1 + cw_ref[2:3, :] * h
    y_ref[...] = (pb * z * (pg * jax.nn.sigmoid(pg))).astype(y_ref.dtype)


def _branch_b(xn, w_in, conv_w, attn):
    rows = xn.shape[0]
    tm = min(rows, 1024)
    tnb = 256
    grid = (D_CONV // tnb, rows // tm)
    side = _attn_side_plan(*attn, n_steps=grid[0] * grid[1], step_of=lambda j, m: j * grid[1] + m)

    def wspec(col0):
        return pl.BlockSpec((D_MODEL, tnb), lambda j, m: (0, w_in.block(col0 // tnb + j, tnb)))

    y, h_last, o, lse = pl.pallas_call(
        functools.partial(_branch_b_kernel, tm=tm, side=side.static),
        grid=grid,
        in_specs=[pl.BlockSpec((tm, D_MODEL), lambda j, m: (m, 0)),
                  wspec(COL_B), wspec(COL_C), wspec(COL_XIN), wspec(COL_GATE_B),
                  pl.BlockSpec((CONV_W, tnb), lambda j, m: (0, j))] + side.in_specs,
        out_specs=[pl.BlockSpec((tm, tnb), lambda j, m: (m, j)),
                   pl.BlockSpec((SUBLANES, tnb), lambda j, m: (0, j))] + side.out_specs,
        out_shape=[jax.ShapeDtypeStruct((rows, D_CONV), BF16),
                   jax.ShapeDtypeStruct((SUBLANES, D_CONV), F32)] + side.out_shape,
        scratch_shapes=[pltpu.VMEM((tm + SUBLANES, tnb), F32)] + side.scratch,
        compiler_params=_params(("arbitrary", "arbitrary")),
        name="branch_b",
    )(xn, w_in.arr, w_in.arr, w_in.arr, w_in.arr, conv_w, *side.args)
    return y, h_last, side.finish(o, lse)


GATE_TILE = 1024
N_GATE_TILES = (2 * D_MODEL + D_ATTN) // GATE_TILE
SILU_TILE = N_GATE_TILES - 1


def _gate_columns(xn, w_ref, o_ref, is_silu):
    for j in range(w_ref.shape[1] // MXU_COLS):
        cs = slice(j * MXU_COLS, (j + 1) * MXU_COLS)
        p = jnp.dot(xn, w_ref[:, cs], preferred_element_type=F32)
        o_ref[:, cs] = jax.nn.sigmoid(p) * jnp.where(is_silu, p, 1.0)


def _gate_weight_block(tile):
    return jnp.where(tile == SILU_TILE, COL_GATE_A // GATE_TILE, COL_MERGE_A // GATE_TILE + tile)


def _gates_kernel(xn_ref, w_ref, *refs, tile0, side):
    side_in, o_ref, side_out, side_scratch = refs[:6], refs[6], refs[7:9], refs[9:]
    _attn_side(side, pl.program_id(0) * pl.num_programs(1) + pl.program_id(1), *side_in, *side_out, *side_scratch)
    _gate_columns(xn_ref[...], w_ref, o_ref, tile0 + pl.program_id(0) == SILU_TILE)


def _gates(xn, w_in, tile0, n_tiles, attn):
    rows = xn.shape[0]
    tm = min(rows, 1024)
    tn = GATE_TILE
    grid = (n_tiles, rows // tm)
    side = _attn_side_plan(*attn, n_steps=grid[0] * grid[1], step_of=lambda n, m: n * grid[1] + m)

    def wmap(n, m):
        return (0, w_in.block(_gate_weight_block(tile0 + n), tn))

    gates, o, lse = pl.pallas_call(
        functools.partial(_gates_kernel, tile0=tile0, side=side.static),
        grid=grid,
        in_specs=[pl.BlockSpec((tm, D_MODEL), lambda n, m: (m, 0)), pl.BlockSpec((D_MODEL, tn), wmap)] + side.in_specs,
        out_specs=[pl.BlockSpec((tm, tn), lambda n, m: (m, n))] + side.out_specs,
        out_shape=[jax.ShapeDtypeStruct((rows, n_tiles * tn), F32)] + side.out_shape,
        scratch_shapes=side.scratch,
        compiler_params=_params(("arbitrary", "arbitrary")),
        name="gates",
    )(xn, w_in.arr, *side.args)
    return gates, side.finish(o, lse)


PROJ_TILE = 1024
N_QKV_TILES = COL_GATE_A // PROJ_TILE
N_PROJ_TILES = COL_MERGE_B // PROJ_TILE + D_MODEL // PROJ_TILE
TILE_GATE_A, TILE_B, TILE_C, TILE_XIN, TILE_GATE_B, TILE_MERGE_A = (
    c // PROJ_TILE for c in (COL_GATE_A, COL_B, COL_C, COL_XIN, COL_GATE_B, COL_MERGE_A))


def _sample_proj_kernel(xn_ref, wq_ref, wr_ref, gain_ref, cw_ref, st_ref,
                        q_ref, kv0_ref, kv1_ref, kv2_ref, gates_ref, y_ref, h_ref, bcx_ref):
    n = pl.program_id(0)
    kv_refs = (kv0_ref, kv1_ref, kv2_ref)
    heads = [slice(h * HEAD_DIM, (h + 1) * HEAD_DIM) for h in range(N_SLOTS)]

    def project(w_ref):
        return jnp.dot(xn_ref[...], w_ref[...], preferred_element_type=F32)

    def store_heads(dst_ref, p, gain_row):
        for h, sl in enumerate(heads):
            ph = p[:, sl]
            if gain_row is not None:
                r = lax.rsqrt(jnp.mean(ph * ph, axis=-1, keepdims=True) + EPS)
                ph = (ph * r) * gain_ref[gain_row, :, sl]
            dst_ref[:, h, :] = ph

    for t in range(N_PROJ_TILES):
        @pl.when(n == t)
        def _(t=t):
            if t < N_QKV_TILES:
                kind, g = divmod(t, N_GROUPS)
                dst_ref = q_ref if kind == 0 else kv_refs[g]
                store_heads(dst_ref, project(wq_ref), kind if kind < 2 else None)
            elif t == TILE_GATE_A:
                p = project(wr_ref)
                gates_ref[...] = p * jax.nn.sigmoid(p)
            elif t in (TILE_B, TILE_C, TILE_XIN):
                bcx_ref[t - TILE_B] = project(wr_ref)
            elif t == TILE_GATE_B:
                pg = project(wr_ref)
                h = bcx_ref[1] * bcx_ref[2]
                z = (cw_ref[0:1, :] * st_ref[:, 0:D_CONV] + cw_ref[1:2, :] * st_ref[:, D_CONV:2 * D_CONV]
                     + cw_ref[2:3, :] * h)
                h_ref[...] = h
                y_ref[...] = (bcx_ref[0] * z * (pg * jax.nn.sigmoid(pg))).astype(y_ref.dtype)
            else:
                gates_ref[...] = jax.nn.sigmoid(project(wr_ref))


def _sample_proj(xn, w_qkv, w_rest, qk_gains, conv_w, state):
    rows = xn.shape[0]
    tn = PROJ_TILE
    assert w_qkv.col0 == 0 and w_rest.col0 == COL_GATE_A

    def const(shape):
        return pl.BlockSpec(shape, lambda n: (0,) * len(shape))

    def gate_block(n):
        return jnp.where(n >= TILE_MERGE_A, n - TILE_MERGE_A, SILU_TILE)

    head_block = (rows, None, N_SLOTS, HEAD_DIM)
    kv_specs = [pl.BlockSpec(head_block, lambda n, g=g: (0, jnp.where(n >= 2 * N_GROUPS + g, 1, 0), 0, 0))
                for g in range(N_GROUPS)]
    return pl.pallas_call(
        _sample_proj_kernel,
        grid=(N_PROJ_TILES,),
        in_specs=[const((rows, D_MODEL)),
                  pl.BlockSpec((D_MODEL, tn), lambda n: (0, jnp.minimum(n, N_QKV_TILES - 1))),
                  pl.BlockSpec((D_MODEL, tn), lambda n: (0, jnp.maximum(n - N_QKV_TILES, 0))),
                  const((2, 1, D_ATTN)), const((CONV_W, D_CONV)), const((rows, (CONV_W - 1) * D_CONV))],
        out_specs=[pl.BlockSpec(head_block, lambda n: (0, jnp.minimum(n, N_GROUPS - 1), 0, 0))] + kv_specs
                  + [pl.BlockSpec((rows, GATE_TILE), lambda n: (0, gate_block(n))),
                     const((rows, D_CONV)), const((rows, D_CONV))],
        out_shape=[jax.ShapeDtypeStruct((rows, N_GROUPS, N_SLOTS, HEAD_DIM), F32)]
                  + [jax.ShapeDtypeStruct((rows, 2, N_SLOTS, HEAD_DIM), F32)] * N_GROUPS
                  + [jax.ShapeDtypeStruct((rows, N_GATE_TILES * GATE_TILE), F32),
                     jax.ShapeDtypeStruct((rows, D_CONV), BF16), jax.ShapeDtypeStruct((rows, D_CONV), F32)],
        scratch_shapes=[pltpu.VMEM((3, rows, D_CONV), F32)],
        compiler_params=_params(("arbitrary",)),
        name="sample_proj",
    )(xn, w_qkv.arr, w_rest.arr, qk_gains, conv_w, state.reshape(rows, (CONV_W - 1) * D_CONV))


def _t5_bucket(dist):
    d = np.maximum(dist, 1).astype(np.float32)
    large = MAX_EXACT + (np.log(d / np.float32(MAX_EXACT)) / np.float32(math.log(MAX_DISTANCE / MAX_EXACT))
                         * np.float32(NUM_BUCKETS - MAX_EXACT)).astype(np.int32)
    large = np.minimum(large, NUM_BUCKETS - 1)
    return np.where(dist < MAX_EXACT, dist, large)


def _bias_by_offset(rel_bias):
    steps = N_BACK - np.arange(N_BACK + 1)
    idx = np.stack([_t5_bucket(d * steps) for d in DILATIONS])
    g_idx = np.arange(N_GROUPS)[:, None]
    vals = jnp.transpose(rel_bias[idx, g_idx], (0, 2, 1)).astype(F32)
    return jnp.pad(vals, ((0, 0), (0, 0), (0, 2 * BLOCK - N_BACK - 1)), constant_values=NEG_INF)


def _sample_bias(cvec):
    kv_head = np.arange(N_BACK * 2 * N_SLOTS) % (2 * N_SLOTS)
    own_k_row = kv_head[None, :] == np.arange(N_SLOTS)[:, None]
    per_row = jnp.repeat(cvec[:, :, :N_BACK], 2 * N_SLOTS, axis=2)
    return jnp.where(own_k_row[None], per_row, NEG_INF), cvec[:, :, N_BACK:N_BACK + 1]


class _SideStatic(NamedTuple):
    per_step: int
    nb: int
    steps: int


class _SidePlan(NamedTuple):
    static: _SideStatic
    in_specs: list
    args: list
    out_specs: list
    out_shape: list
    scratch: list
    finish: Callable


def _attn_side(side, step, cvec_ref, q_ref, kp_ref, kc_ref, vp_ref, vc_ref, o_ref, lse_ref, bias_ref, s_ref, p_ref):
    @pl.when(step == 0)
    def _():
        for h in range(N_SLOTS):
            row = jnp.broadcast_to(cvec_ref[h:h + 1, :], (BLOCK, 2 * BLOCK))
            full = pltpu.roll(row, 0, 1, stride=1, stride_axis=0)
            bias_ref[1, h] = full
            bias_ref[0, h, :, :BLOCK] = jnp.full((BLOCK, BLOCK), NEG_INF, F32)
            bias_ref[0, h, :, BLOCK:] = full[:, BLOCK:]

    nt = (((1,), (1,)), ((), ()))
    heads = [slice(h * HEAD_DIM, (h + 1) * HEAD_DIM) for h in range(N_SLOTS)]
    unit0 = jnp.minimum(step, side.steps - 1) * side.per_step
    lane = lax.broadcasted_iota(jnp.int32, (BLOCK, LANES), 1)
    for u in range(side.per_step):
        rows = slice(u * BLOCK, (u + 1) * BLOCK)
        prev = slice((u - 1) * BLOCK, u * BLOCK)
        for h, sl in enumerate(heads):
            q = q_ref[rows, sl]
            k_prev = kp_ref[:, sl] if u == 0 else kc_ref[prev, sl]
            s_ref[u, h, :, :BLOCK] = lax.dot_general(q, k_prev, nt, preferred_element_type=F32)
            s_ref[u, h, :, BLOCK:] = lax.dot_general(q, kc_ref[rows, sl], nt, preferred_element_type=F32)
    ls, lses = [], []
    for u in range(side.per_step):
        has_prev = jnp.minimum((unit0 + u) % side.nb, 1)
        sc = s_ref[u] * SCALE + bias_ref[has_prev]
        mx = jnp.max(jnp.maximum(sc[:, :, :BLOCK], sc[:, :, BLOCK:]), axis=-1, keepdims=True)
        p = jnp.exp(sc - mx)
        l = jnp.sum(p[:, :, :BLOCK] + p[:, :, BLOCK:], axis=-1, keepdims=True)
        p_ref[u] = p.astype(BF16)
        ls.append(l)
        lses.append(mx + jnp.log(l))
    for u in range(side.per_step):
        rows = slice(u * BLOCK, (u + 1) * BLOCK)
        prev = slice((u - 1) * BLOCK, u * BLOCK)
        lse_all = jnp.zeros((BLOCK, LANES), F32)
        for h, sl in enumerate(heads):
            v_prev = vp_ref[:, sl] if u == 0 else vc_ref[prev, sl]
            o = (jnp.dot(p_ref[u, h, :, :BLOCK], v_prev, preferred_element_type=F32)
                 + jnp.dot(p_ref[u, h, :, BLOCK:], vc_ref[rows, sl], preferred_element_type=F32))
            o_ref[rows, sl] = o / ls[u][h]
            lse_all = jnp.where(lane == h, lses[u][h], lse_all)
        lse_ref[rows, :] = lse_all


def _attn_side_plan(qkv, cvec, g, n_steps, step_of):
    _, dil, ls, _ = qkv.shape
    nb = ls // BLOCK
    units = dil * nb
    per_step = min(d for d in range(1, units + 1) if units % d == 0 and d * n_steps >= units)
    static = _SideStatic(per_step=per_step, nb=nb, steps=units // per_step)
    flat = qkv.reshape(3, dil * ls, D_ATTN)
    rows = per_step * BLOCK

    def cur(kind):
        return pl.BlockSpec((None, rows, D_ATTN), lambda *idx: (kind, jnp.minimum(step_of(*idx), static.steps - 1), 0))

    def prev(kind):
        def index(*idx):
            first_unit = jnp.minimum(step_of(*idx), static.steps - 1) * per_step
            return (kind, jnp.maximum(first_unit - 1, 0), 0)
        return pl.BlockSpec((None, BLOCK, D_ATTN), index)

    def out_spec(width):
        return pl.BlockSpec((rows, width), lambda *idx: (jnp.minimum(step_of(*idx), static.steps - 1), 0))

    return _SidePlan(
        static=static,
        in_specs=[pl.BlockSpec((None, N_SLOTS, 2 * BLOCK), lambda *idx: (g, 0, 0)),
                  cur(0), prev(1), cur(1), prev(2), cur(2)],
        args=[cvec, flat, flat, flat, flat, flat],
        out_specs=[out_spec(D_ATTN), out_spec(LANES)],
        out_shape=[jax.ShapeDtypeStruct((dil * ls, D_ATTN), F32), jax.ShapeDtypeStruct((dil * ls, LANES), F32)],
        scratch=[pltpu.VMEM((2, N_SLOTS, BLOCK, 2 * BLOCK), F32),
                 pltpu.VMEM((per_step, N_SLOTS, BLOCK, 2 * BLOCK), F32),
                 pltpu.VMEM((per_step, N_SLOTS, BLOCK, 2 * BLOCK), BF16)],
        finish=lambda o, lse: (o.reshape(dil, ls, D_ATTN), lse.reshape(dil, ls, LANES)),
    )


def _attn_sample_kernel(bias_ref, nbias_ref, q0_ref, q1_ref, q2_ref, kv0_ref, kv1_ref, kv2_ref,
                        c0_ref, c1_ref, c2_ref, *refs, tb, gate_tile):
    if gate_tile is None:
        o_ref, slab_ref, s_ref, p_ref, og_ref = refs
    else:
        xn_ref, w_ref, o_ref, gate_ref, slab_ref, s_ref, p_ref, og_ref = refs
        _gate_columns(xn_ref[...], w_ref, gate_ref, gate_tile == SILU_TILE)
    rows = pl.ds(pl.program_id(0) * tb, tb)
    nt = (((1,), (1,)), ((), ()))
    kv_rows = 2 * N_SLOTS
    slab_rows = N_BACK * kv_rows
    q_refs = (q0_ref, q1_ref, q2_ref)
    kv_refs = (kv0_ref, kv1_ref, kv2_ref)
    c_refs = (c0_ref, c1_ref, c2_ref)

    def rounded(x):
        return x.astype(BF16).astype(F32)

    qs = [q_ref[rows] for q_ref in q_refs]
    for g in range(N_GROUPS):
        for t in range(tb):
            slab_ref[g, t] = c_refs[g][t].reshape(slab_rows, HEAD_DIM).astype(BF16)
            s_ref[g, t] = lax.dot_general(qs[g][t].astype(BF16), slab_ref[g, t], nt, preferred_element_type=F32)
    p_news, ls, lses = [], [], []
    for g in range(N_GROUPS):
        s_n = jnp.sum(rounded(qs[g]) * rounded(kv_refs[g][rows, 0]), axis=-1, keepdims=True)
        bias = bias_ref[g][None]
        sc_c = jnp.where(bias > NEG_INF, s_ref[g] * SCALE + bias, NEG_INF)
        sc_n = s_n * SCALE + nbias_ref[g][None]
        mx = jnp.maximum(jnp.max(sc_c, axis=-1, keepdims=True), sc_n)
        p_c = jnp.exp(sc_c - mx)
        p_n = jnp.exp(sc_n - mx)
        l = jnp.sum(p_c, axis=-1, keepdims=True) + p_n
        p_ref[g] = pltpu.roll(p_c, N_SLOTS, 2).astype(BF16)
        p_news.append(p_n)
        ls.append(l)
        lses.append(mx + jnp.log(l))
    for g in range(N_GROUPS):
        for t in range(tb):
            og_ref[g, t] = jnp.dot(p_ref[g, t], slab_ref[g, t], preferred_element_type=F32)
    outs = [(og_ref[g] + rounded(p_news[g]) * rounded(kv_refs[g][rows, 1])) / ls[g] for g in range(N_GROUPS)]
    top = jnp.maximum(jnp.maximum(lses[0], lses[1]), lses[2])
    es = [jnp.exp(x - top) for x in lses]
    tot = es[0] + es[1] + es[2]
    o_ref[rows] = (es[0] / tot) * outs[0] + (es[1] / tot) * outs[1] + (es[2] / tot) * outs[2]


def _attn_sample(qs, kvs, caches, bias, nbias, gate_rider=None):
    db = qs.shape[0]
    tb = 4
    kv_rows = 2 * N_SLOTS
    n_steps = db // tb
    rider_in_specs, rider_args, rider_out_specs, rider_out_shape, gate_tile = [], [], [], [], None
    if gate_rider is not None:
        xn, w_in, gate_tile = gate_rider
        rows_per_step = xn.shape[0] // n_steps
        rider_in_specs = [pl.BlockSpec((rows_per_step, D_MODEL), lambda s: (s, 0)),
                          pl.BlockSpec((D_MODEL, GATE_TILE),
                                       lambda s: (0, w_in.block(_gate_weight_block(gate_tile), GATE_TILE)))]
        rider_args = [xn, w_in.arr]
        rider_out_specs = [pl.BlockSpec((rows_per_step, GATE_TILE), lambda s: (s, 0))]
        rider_out_shape = [jax.ShapeDtypeStruct((xn.shape[0], GATE_TILE), F32)]
    views, specs = [], []
    for g, c in enumerate(caches):
        dil = DILATIONS[g]
        lc = c.shape[1]
        views.append(c.reshape(db, lc // dil, dil * kv_rows, HEAD_DIM))
        specs.append(pl.BlockSpec((tb, N_BACK, kv_rows, HEAD_DIM), lambda s: (s, 0, 0, 0)))
    q_specs = [pl.BlockSpec((db, None, N_SLOTS, HEAD_DIM), lambda s, g=g: (0, g, 0, 0)) for g in range(N_GROUPS)]
    kv_spec = pl.BlockSpec((db, 2, N_SLOTS, HEAD_DIM), lambda s: (0, 0, 0, 0))
    outs = pl.pallas_call(
        functools.partial(_attn_sample_kernel, tb=tb, gate_tile=gate_tile),
        grid=(n_steps,),
        in_specs=[pl.BlockSpec((N_GROUPS, N_SLOTS, N_BACK * kv_rows), lambda s: (0, 0, 0)),
                  pl.BlockSpec((N_GROUPS, N_SLOTS, 1), lambda s: (0, 0, 0)),
                  *q_specs, kv_spec, kv_spec, kv_spec] + specs + rider_in_specs,
        out_specs=[pl.BlockSpec((db, N_SLOTS, HEAD_DIM), lambda s: (0, 0, 0))] + rider_out_specs,
        out_shape=[jax.ShapeDtypeStruct((db, N_SLOTS, HEAD_DIM), F32)] + rider_out_shape,
        scratch_shapes=[pltpu.VMEM((N_GROUPS, tb, N_BACK * kv_rows, HEAD_DIM), BF16),
                        pltpu.VMEM((N_GROUPS, tb, N_SLOTS, N_BACK * kv_rows), F32),
                        pltpu.VMEM((N_GROUPS, tb, N_SLOTS, N_BACK * kv_rows), BF16),
                        pltpu.VMEM((N_GROUPS, tb, N_SLOTS, HEAD_DIM), F32)],
        compiler_params=_params(("arbitrary",)),
        name="attn_sample",
    )(bias, nbias, qs, qs, qs, *kvs, *views, *rider_args)
    return outs[0] if gate_rider is None else outs


def _out_kernel(*refs, tm, n_groups):
    o_refs = refs[:n_groups]
    lse_refs = refs[n_groups:2 * n_groups] if n_groups > 1 else ()
    rest = refs[len(o_refs) + len(lse_refs):]
    def project(a):
        ya = jnp.dot(a, woa_ref[...], preferred_element_type=F32)
        yb = jnp.dot(yb_ref[...], wob_ref[...], preferred_element_type=F32)
        merged = sma_ref[...] * ya + smb_ref[...] * yb
        y_ref[...] = x_ref[...] + jnp.dot(merged.astype(BF16), wo_ref[...], preferred_element_type=F32)

    if n_groups > 1:
        (xn_ref, wga_ref, sma_ref, smb_ref, yb_ref, x_ref, woa_ref, wob_ref, wo_ref,
         y_ref, a2_ref, og_ref, lg_ref) = rest
        step = pl.program_id(0)

        @pl.when(step == 0)
        def _():
            a2_ref[1] = jnp.zeros(a2_ref.shape[1:], BF16)

        project(a2_ref[(step + 1) % 2])
        a_ref = a2_ref.at[step % 2]
        for g in range(n_groups):
            dil = DILATIONS[g]
            sub = tm // dil
            for r in range(dil):
                rows = slice(None) if dil == 1 else pl.ds(r, sub, stride=dil)
                lg_ref[g, rows, :] = lse_refs[g][r]
                for h in range(N_SLOTS):
                    og_ref[g, h, rows, :] = o_refs[g][r, :, h * HEAD_DIM:(h + 1) * HEAD_DIM]
        lses = [lg_ref[g] for g in range(n_groups)]
        top = functools.reduce(jnp.maximum, lses)
        es = [jnp.exp(x - top) for x in lses]
        tot = functools.reduce(lambda a, b: a + b, es)
        ws = [e / tot for e in es]
        gate_a = jnp.dot(xn_ref[...], wga_ref[...], preferred_element_type=F32)
        sga = gate_a * jax.nn.sigmoid(gate_a)
        for h in range(N_SLOTS):
            sl = slice(h * HEAD_DIM, (h + 1) * HEAD_DIM)
            o = functools.reduce(lambda a, b: a + b,
                                 [ws[g][:, h:h + 1] * og_ref[g, h] for g in range(n_groups)])
            a_ref[:, sl] = (o * sga[:, sl]).astype(BF16)
    else:
        sga_ref, sma_ref, smb_ref, yb_ref, x_ref, woa_ref, wob_ref, wo_ref, y_ref = rest
        project((o_refs[0][...] * sga_ref[...]).astype(BF16))


def _out_proj(os_, lses, gates, yb_in, x, w_out_a, w_out_b, w_o, gate_a_from=None):
    rows = x.shape[0]

    tm = min(rows, 256)
    n_groups = len(os_)
    n_blocks = rows // tm
    pipelined = n_groups > 1
    n_steps = n_blocks + 1 if pipelined else n_blocks

    def comb_block(s):
        return jnp.minimum(s, n_blocks - 1)

    def proj_block(s):
        return jnp.maximum(s - 1, 0) if pipelined else s

    def gate_operand(tile, width, block_of):
        arr, tile0 = next((a, t0) for a, t0 in gates if t0 <= tile < t0 + a.shape[1] // GATE_TILE)
        assert (tile - tile0) * GATE_TILE % width == 0
        return arr, pl.BlockSpec((tm, width), lambda s: (block_of(s), (tile - tile0) * GATE_TILE // width))

    def row_spec(width, block_of):
        return pl.BlockSpec((tm, width), lambda s: (block_of(s), 0))

    def sub_spec(dil, width):
        return pl.BlockSpec((dil, tm // dil, width), lambda s: (0, comb_block(s), 0))

    def const_spec(shape):
        return pl.BlockSpec(shape, lambda s: (0, 0), pipeline_mode=pl.Buffered(1))

    if pipelined:
        in_specs = ([sub_spec(DILATIONS[g], D_ATTN) for g in range(n_groups)]
                    + [sub_spec(DILATIONS[g], LANES) for g in range(n_groups)])
        scratch = [pltpu.VMEM((2, tm, D_ATTN), BF16),
                   pltpu.VMEM((n_groups, N_SLOTS, tm, HEAD_DIM), F32), pltpu.VMEM((n_groups, tm, LANES), F32)]
    else:
        in_specs = [row_spec(D_ATTN, comb_block)]
        scratch = []
    assert pipelined == (gate_a_from is not None)
    if pipelined:
        xn, w_in = gate_a_from
        gate_a_args = [xn, w_in.arr]
        gate_a_specs = [row_spec(D_MODEL, comb_block),
                        pl.BlockSpec((D_MODEL, GATE_TILE), lambda s: (0, w_in.block(COL_GATE_A // GATE_TILE, GATE_TILE)),
                                     pipeline_mode=pl.Buffered(1))]
    else:
        sga, sga_spec = gate_operand(SILU_TILE, D_ATTN, comb_block)
        gate_a_args, gate_a_specs = [sga], [sga_spec]
    sma, sma_spec = gate_operand(0, D_MODEL, proj_block)
    smb, smb_spec = gate_operand(D_MODEL // GATE_TILE, D_MODEL, proj_block)
    in_specs += gate_a_specs + [sma_spec, smb_spec, row_spec(D_CONV, proj_block), row_spec(D_MODEL, proj_block),
                                const_spec((D_ATTN, D_MODEL)), const_spec((D_CONV, D_MODEL)),
                                const_spec((D_MODEL, D_MODEL))]
    return pl.pallas_call(
        functools.partial(_out_kernel, tm=tm, n_groups=n_groups),
        grid=(n_steps,),
        in_specs=in_specs,
        out_specs=row_spec(D_MODEL, proj_block),
        out_shape=jax.ShapeDtypeStruct((rows, D_MODEL), F32),
        scratch_shapes=scratch,
        compiler_params=_params(("arbitrary",)),
        name="out_proj",
    )(*os_, *lses, *gate_a_args, sma, smb, yb_in, x, w_out_a, w_out_b, w_o)


def kernel(x_prompt, x_sample, cache_kv_w128, cache_kv_w512, cache_kv_w2048, state_conv, norm_gain, w_in,
           q_norm_gain, k_norm_gain, rel_bias, conv_w, w_out_a, w_out_b, w_o):
    seq = x_prompt.shape[1]
    db = x_sample.shape[0]
    xp = x_prompt.reshape(seq, D_MODEL)
    xs = x_sample.reshape(db, D_MODEL)
    caches = (cache_kv_w128, cache_kv_w512, cache_kv_w2048)
    qk_gains = jnp.stack([jnp.tile(q_norm_gain, N_SLOTS), jnp.tile(k_norm_gain, N_SLOTS)]).reshape(2, 1, D_ATTN)
    cvec = _bias_by_offset(rel_bias)

    xn = _pre_norm(xp, norm_gain)
    qkv0, kv_p0, w_qkv = _qkv_proj(xn, _Cols(w_in, 0), qk_gains, 0, DILATIONS[0], BF16, min(WINDOWS[0], seq),
                                   casts=((w_in, 0, COL_GATE_A),))
    w_qkv = _Cols(w_qkv, 0)
    qkv1, kv_p1, w_rest, woa, wob, wo = _qkv_proj(
        xn, w_qkv, qk_gains, 1, DILATIONS[1], BF16, min(WINDOWS[1], seq),
        casts=((w_in, COL_GATE_A, w_in.shape[1] - COL_GATE_A),
               (w_out_a, 0, D_MODEL), (w_out_b, 0, D_MODEL), (w_o, 0, D_MODEL)))
    w_rest = _Cols(w_rest, COL_GATE_A)
    qkv2, kv_p2 = _qkv_proj(xn, w_qkv, qk_gains, 2, DILATIONS[2], BF16, min(WINDOWS[2], seq))
    qkvs, kv_p = (qkv0, qkv1, qkv2), (kv_p0, kv_p1, kv_p2)
    xn_s = _pre_norm(xs, norm_gain)
    q_s, *kv_s, gates_s, yb_in_s, h_s = _sample_proj(xn_s, w_qkv, w_rest, qk_gains, conv_w, state_conv)
    o_s = _attn_sample(q_s, kv_s, caches, *_sample_bias(cvec))
    y_sample = _out_proj((o_s.reshape(db, D_ATTN),), (), [(gates_s, 0)], yb_in_s, xs, woa, wob, wo)
    y_sample = y_sample.reshape(db, 1, D_MODEL)

    gates_a, attn0 = _gates(xn, w_rest, 0, 2, attn=(qkvs[0], cvec, 0))
    gates_b, attn1 = _gates(xn, w_rest, 2, 2, attn=(qkvs[1], cvec, 1))
    yb_in, h_last, attn2 = _branch_b(xn, w_rest, conv_w, attn=(qkvs[2], cvec, 2))
    gates = [(gates_a, 0), (gates_b, 2)]
    os_, lses = zip(attn0, attn1, attn2)
    y_prompt = _out_proj(os_, lses, gates, yb_in, xp, woa, wob, wo, gate_a_from=(xn, w_rest))
    y_prompt = y_prompt.reshape(1, seq, D_MODEL)
    conv_p = h_last[SUBLANES - (CONV_W - 1):][None]
    conv_s = jnp.stack([state_conv[:, CONV_W - 2], h_s], axis=1)

    return (y_prompt, y_sample, kv_p[0][None], kv_p[1][None], kv_p[2][None], conv_p,
            kv_s[0][:, None], kv_s[1][:, None], kv_s[2][:, None], conv_s)
```

```python
import functools
import math
from typing import Callable, NamedTuple

import numpy as np
import jax
import jax.numpy as jnp
from jax import lax
from jax.experimental import pallas as pl
from jax.experimental.pallas import tpu as pltpu

D_MODEL = 2048
N_GROUPS = 3
DILATIONS = (1, 4, 16)
N_BACK = 128
WINDOWS = (128, 512, 2048)
N_SLOTS = 8
HEAD_DIM = 128
D_ATTN = N_SLOTS * HEAD_DIM
QKV_COLS = N_GROUPS * D_ATTN
D_CONV = D_MODEL // 2
CONV_W = 3
BLOCK = N_BACK
NUM_BUCKETS = 32
MAX_EXACT = NUM_BUCKETS // 2
MAX_DISTANCE = 2048
EPS = 1e-6
SCALE = HEAD_DIM ** -0.5

COL_K = QKV_COLS
COL_V = 2 * QKV_COLS
COL_GATE_A = 3 * QKV_COLS
COL_B = COL_GATE_A + D_ATTN
COL_C = COL_B + D_CONV
COL_XIN = COL_C + D_CONV
COL_GATE_B = COL_XIN + D_CONV
COL_MERGE_A = COL_GATE_B + D_CONV
COL_MERGE_B = COL_MERGE_A + D_MODEL

SUBLANES = 8
LANES = 128
MXU_COLS = 256
VMEM_LIMIT_BYTES = 56 * 1024 * 1024

BF16 = jnp.bfloat16
F32 = jnp.float32
NEG_INF = float("-inf")


def _params(semantics):
    return pltpu.CompilerParams(dimension_semantics=semantics, vmem_limit_bytes=VMEM_LIMIT_BYTES)


def _norm_kernel(x_ref, g_ref, o_ref):
    x = x_ref[...]
    r = lax.rsqrt(jnp.mean(x * x, axis=-1, keepdims=True) + EPS)
    o_ref[...] = ((x * r) * g_ref[...]).astype(o_ref.dtype)


def _pre_norm(x, gain):
    rows = x.shape[0]
    tm = min(rows, 512)
    return pl.pallas_call(
        _norm_kernel,
        grid=(rows // tm,),
        in_specs=[pl.BlockSpec((tm, D_MODEL), lambda m: (m, 0)),
                  pl.BlockSpec((1, D_MODEL), lambda m: (0, 0))],
        out_specs=pl.BlockSpec((tm, D_MODEL), lambda m: (m, 0)),
        out_shape=jax.ShapeDtypeStruct((rows, D_MODEL), BF16),
        compiler_params=_params(("arbitrary",)),
        name="pre_norm",
    )(x, gain.reshape(1, D_MODEL))


ROW_STRIDE = 4


def _qkv_kernel(xn_ref, w_ref, gain_ref, *refs, dil, tm, tr, m_tail0, n_casts, w_is_f32):
    two_stage = dil > ROW_STRIDE
    n_scratch = 2 if two_stage else 1
    cast_in, (o_ref, tail_ref), cast_out, own_scratch, w_scratch = (
        refs[:n_casts], refs[n_casts:n_casts + 2], refs[n_casts + 2:2 * n_casts + 2],
        refs[2 * n_casts + 2:2 * n_casts + 2 + n_scratch], refs[2 * n_casts + 2 + n_scratch:])
    res_ref = own_scratch[0]
    kind = pl.program_id(0)
    m = pl.program_id(1)
    if w_is_f32:
        w_f32_ref, (w_ref,) = w_ref, w_scratch

        @pl.when(m == 0)
        def _():
            w_ref[...] = w_f32_ref[...].astype(BF16)

    normed = kind < 2
    xn = xn_ref[...]
    sub = tm // dil
    part = tm // ROW_STRIDE
    heads_per_chunk = MXU_COLS // HEAD_DIM
    n_chunks = D_ATTN // MXU_COLS
    for j in range(n_chunks):
        p = jnp.dot(xn, w_ref[:, j * MXU_COLS:(j + 1) * MXU_COLS], preferred_element_type=F32)
        for src_ref, dst_ref in zip(cast_in, cast_out):
            share = src_ref.shape[0] // n_chunks
            dst_ref[j * share:(j + 1) * share, :] = src_ref[j * share:(j + 1) * share, :].astype(BF16)
        for hh in range(heads_per_chunk):
            h = j * heads_per_chunk + hh
            sl = slice(h * HEAD_DIM, (h + 1) * HEAD_DIM)
            ph = p[:, hh * HEAD_DIM:(hh + 1) * HEAD_DIM]
            r = lax.rsqrt(jnp.mean(ph * ph, axis=-1, keepdims=True) + EPS)
            res = (ph * jnp.where(normed, r, 1.0)) * jnp.where(normed, gain_ref[:, sl], 1.0)
            res_ref[h] = res
            if two_stage:
                stage_ref = own_scratch[1].at[h % 2]
                for r1 in range(ROW_STRIDE):
                    stage_ref[r1 * part:(r1 + 1) * part, :] = res_ref[h, pl.ds(r1, part, stride=ROW_STRIDE), :]
            for rr in range(dil):
                if dil == 1:
                    rows = res
                elif two_stage:
                    r1, r2 = rr % ROW_STRIDE, rr // ROW_STRIDE
                    rows = stage_ref[pl.ds(r1 * part + r2, sub, stride=ROW_STRIDE), :]
                else:
                    rows = res_ref[h, pl.ds(rr, sub, stride=dil), :]
                o_ref[rr, :, sl] = rows.astype(o_ref.dtype)

    @pl.when(jnp.logical_and(kind >= 1, m >= m_tail0))
    def _():
        for h in range(N_SLOTS):
            tail_ref[:, h, :] = res_ref[h, tm - tr:tm, :]


class _Cols(NamedTuple):
    arr: jax.Array
    col0: int

    def block(self, col_block, width):
        return col_block - self.col0 // width


def _qkv_proj(xn, w_in, qk_gains, g, dil, out_dtype, tail_rows, casts=()):
    rows = xn.shape[0]
    tm = min(rows, 1024)
    tn = D_ATTN
    tr = min(tail_rows, tm)
    m_tail0 = rows // tm - tail_rows // tr
    grid = (3, rows // tm)
    n_steps = grid[0] * grid[1]
    w_is_f32 = w_in.arr.dtype == F32

    def tail_map(kind, m):
        return (jnp.where(kind >= 1, jnp.maximum(m - m_tail0, 0), 0), jnp.maximum(kind - 1, 0), 0, 0)

    def cast_specs_of(w, col0, n_cols):
        n_blocks = max(d for d in range(1, n_steps + 1) if n_cols % (d * LANES) == 0)
        width = n_cols // n_blocks
        assert col0 % width == 0

        def block(kind, m):
            return jnp.minimum(kind * grid[1] + m, n_blocks - 1)

        return (pl.BlockSpec((w.shape[0], width), lambda kind, m: (0, col0 // width + block(kind, m))),
                pl.BlockSpec((w.shape[0], width), lambda kind, m: (0, block(kind, m))))

    cast_in_specs, cast_out_specs = zip(*[cast_specs_of(*c) for c in casts]) if casts else ((), ())
    assert dil <= ROW_STRIDE or dil == ROW_STRIDE * ROW_STRIDE
    scratch = [pltpu.VMEM((N_SLOTS, tm, HEAD_DIM), F32)]
    if dil > ROW_STRIDE:
        scratch.append(pltpu.VMEM((2, tm, HEAD_DIM), F32))
    if w_is_f32:
        scratch.append(pltpu.VMEM((D_MODEL, tn), BF16))
    return pl.pallas_call(
        functools.partial(_qkv_kernel, dil=dil, tm=tm, tr=tr, m_tail0=m_tail0, n_casts=len(casts), w_is_f32=w_is_f32),
        grid=grid,
        in_specs=[pl.BlockSpec((tm, D_MODEL), lambda kind, m: (m, 0)),
                  pl.BlockSpec((D_MODEL, tn), lambda kind, m: (0, w_in.block(kind * N_GROUPS + g, tn))),
                  pl.BlockSpec((None, 1, tn), lambda kind, m: (jnp.minimum(kind, 1), 0, 0))] + list(cast_in_specs),
        out_specs=[pl.BlockSpec((None, dil, tm // dil, tn), lambda kind, m: (kind, 0, m, 0)),
                   pl.BlockSpec((tr, None, N_SLOTS, HEAD_DIM), tail_map)] + list(cast_out_specs),
        out_shape=[jax.ShapeDtypeStruct((3, dil, rows // dil, tn), out_dtype),
                   jax.ShapeDtypeStruct((tail_rows, 2, N_SLOTS, HEAD_DIM), F32)]
                  + [jax.ShapeDtypeStruct((w.shape[0], n_cols), BF16) for w, _, n_cols in casts],
        scratch_shapes=scratch,
        compiler_params=_params(("arbitrary", "arbitrary")),
        name=f"qkv_proj_g{g}",
    )(xn, w_in.arr, qk_gains, *[w for w, _, _ in casts])


def _branch_b_kernel(xn_ref, wb_ref, wc_ref, wx_ref, wg_ref, cw_ref, *refs, tm, side):
    side_in, (y_ref, h_ref), side_out, (hbuf_ref,), side_scratch = (
        refs[:6], refs[6:8], refs[8:10], refs[10:11], refs[11:])

    @pl.when(pl.program_id(1) == 0)
    def _():
        hbuf_ref[0:SUBLANES, :] = jnp.zeros((SUBLANES, hbuf_ref.shape[1]), F32)

    _attn_side(side, pl.program_id(0) * pl.num_programs(1) + pl.program_id(1), *side_in, *side_out, *side_scratch)
    half = tm // 2
    for part in range(2):
        rows = slice(part * half, (part + 1) * half)
        xn = xn_ref[rows, :]
        pc = jnp.dot(xn, wc_ref[...], preferred_element_type=F32)
        px = jnp.dot(xn, wx_ref[...], preferred_element_type=F32)
        h = pc * px
        hbuf_ref[SUBLANES + part * half:SUBLANES + (part + 1) * half, :] = h
        hm2 = hbuf_ref[pl.ds(SUBLANES - 2 + part * half, half), :]
        hm1 = hbuf_ref[pl.ds(SUBLANES - 1 + part * half, half), :]
        z = cw_ref[0:1, :] * hm2 + cw_ref[1:2, :] * hm1 + cw_ref[2:3, :] * h
        pb = jnp.dot(xn, wb_ref[...], preferred_element_type=F32)
        pg = jnp.dot(xn, wg_ref[...], preferred_element_type=F32)
        y_ref[rows, :] = (pb * z * (pg * jax.nn.sigmoid(pg))).astype(y_ref.dtype)
    last = hbuf_ref[pl.ds(tm, SUBLANES), :]
    h_ref[...] = last
    hbuf_ref[0:SUBLANES, :] = last


def _branch_b(xn, w_in, conv_w, attn):
    rows = xn.shape[0]
    tm = min(rows, 1024)
    tnb = 256
    grid = (D_CONV // tnb, rows // tm)
    side = _attn_side_plan(*attn, n_steps=grid[0] * grid[1], step_of=lambda j, m: j * grid[1] + m)

    def wspec(col0):
        return pl.BlockSpec((D_MODEL, tnb), lambda j, m: (0, w_in.block(col0 // tnb + j, tnb)))

    y, h_last, o, lse = pl.pallas_call(
        functools.partial(_branch_b_kernel, tm=tm, side=side.static),
        grid=grid,
        in_specs=[pl.BlockSpec((tm, D_MODEL), lambda j, m: (m, 0)),
                  wspec(COL_B), wspec(COL_C), wspec(COL_XIN), wspec(COL_GATE_B),
                  pl.BlockSpec((CONV_W, tnb), lambda j, m: (0, j))] + side.in_specs,
        out_specs=[pl.BlockSpec((tm, tnb), lambda j, m: (m, j)),
                   pl.BlockSpec((SUBLANES, tnb), lambda j, m: (0, j))] + side.out_specs,
        out_shape=[jax.ShapeDtypeStruct((rows, D_CONV), BF16),
                   jax.ShapeDtypeStruct((SUBLANES, D_CONV), F32)] + side.out_shape,
        scratch_shapes=[pltpu.VMEM((tm + SUBLANES, tnb), F32)] + side.scratch,
        compiler_params=_params(("arbitrary", "arbitrary")),
        name="branch_b",
    )(xn, w_in.arr, w_in.arr, w_in.arr, w_in.arr, conv_w, *side.args)
    return y, h_last, side.finish(o, lse)


GATE_TILE = 1024
N_GATE_TILES = (2 * D_MODEL + D_ATTN) // GATE_TILE
SILU_TILE = N_GATE_TILES - 1


def _gate_columns(xn, w_ref, o_ref, is_silu):
    for j in range(w_ref.shape[1] // MXU_COLS):
        cs = slice(j * MXU_COLS, (j + 1) * MXU_COLS)
        p = jnp.dot(xn, w_ref[:, cs], preferred_element_type=F32)
        o_ref[:, cs] = jax.nn.sigmoid(p) * jnp.where(is_silu, p, 1.0)


def _gate_weight_block(tile):
    return jnp.where(tile == SILU_TILE, COL_GATE_A // GATE_TILE, COL_MERGE_A // GATE_TILE + tile)


def _gates_kernel(xn_ref, w_ref, *refs, tile0, side):
    side_in, o_ref, side_out, side_scratch = refs[:6], refs[6], refs[7:9], refs[9:]
    _attn_side(side, pl.program_id(0) * pl.num_programs(1) + pl.program_id(1), *side_in, *side_out, *side_scratch)
    _gate_columns(xn_ref[...], w_ref, o_ref, tile0 + pl.program_id(0) == SILU_TILE)


def _gates(xn, w_in, tile0, n_tiles, attn):
    rows = xn.shape[0]
    tm = min(rows, 1024)
    tn = GATE_TILE
    grid = (n_tiles, rows // tm)
    side = _attn_side_plan(*attn, n_steps=grid[0] * grid[1], step_of=lambda n, m: n * grid[1] + m)

    def wmap(n, m):
        return (0, w_in.block(_gate_weight_block(tile0 + n), tn))

    gates, o, lse = pl.pallas_call(
        functools.partial(_gates_kernel, tile0=tile0, side=side.static),
        grid=grid,
        in_specs=[pl.BlockSpec((tm, D_MODEL), lambda n, m: (m, 0)), pl.BlockSpec((D_MODEL, tn), wmap)] + side.in_specs,
        out_specs=[pl.BlockSpec((tm, tn), lambda n, m: (m, n))] + side.out_specs,
        out_shape=[jax.ShapeDtypeStruct((rows, n_tiles * tn), F32)] + side.out_shape,
        scratch_shapes=side.scratch,
        compiler_params=_params(("arbitrary", "arbitrary")),
        name="gates",
    )(xn, w_in.arr, *side.args)
    return gates, side.finish(o, lse)


PROJ_TILE = 1024
N_QKV_TILES = COL_GATE_A // PROJ_TILE
N_PROJ_TILES = COL_MERGE_B // PROJ_TILE + D_MODEL // PROJ_TILE
TILE_GATE_A, TILE_B, TILE_C, TILE_XIN, TILE_GATE_B, TILE_MERGE_A = (
    c // PROJ_TILE for c in (COL_GATE_A, COL_B, COL_C, COL_XIN, COL_GATE_B, COL_MERGE_A))


def _sample_proj_kernel(xn_ref, wq_ref, wr_ref, gain_ref, cw_ref, st_ref,
                        q_ref, kv0_ref, kv1_ref, kv2_ref, gates_ref, y_ref, h_ref, bcx_ref):
    n = pl.program_id(0)
    kv_refs = (kv0_ref, kv1_ref, kv2_ref)
    heads = [slice(h * HEAD_DIM, (h + 1) * HEAD_DIM) for h in range(N_SLOTS)]

    def project(w_ref):
        return jnp.dot(xn_ref[...], w_ref[...], preferred_element_type=F32)

    def store_heads(dst_ref, p, gain_row):
        for h, sl in enumerate(heads):
            ph = p[:, sl]
            if gain_row is not None:
                r = lax.rsqrt(jnp.mean(ph * ph, axis=-1, keepdims=True) + EPS)
                ph = (ph * r) * gain_ref[gain_row, :, sl]
            dst_ref[:, h, :] = ph

    for t in range(N_PROJ_TILES):
        @pl.when(n == t)
        def _(t=t):
            if t < N_QKV_TILES:
                kind, g = divmod(t, N_GROUPS)
                dst_ref = q_ref if kind == 0 else kv_refs[g]
                store_heads(dst_ref, project(wq_ref), kind if kind < 2 else None)
            elif t == TILE_GATE_A:
                p = project(wr_ref)
                gates_ref[...] = p * jax.nn.sigmoid(p)
            elif t in (TILE_B, TILE_C, TILE_XIN):
                bcx_ref[t - TILE_B] = project(wr_ref)
            elif t == TILE_GATE_B:
                pg = project(wr_ref)
                h = bcx_ref[1] * bcx_ref[2]
                z = (cw_ref[0:1, :] * st_ref[:, 0:D_CONV] + cw_ref[1:2, :] * st_ref[:, D_CONV:2 * D_CONV]
                     + cw_ref[2:3, :] * h)
                h_ref[...] = h
                y_ref[...] = (bcx_ref[0] * z * (pg * jax.nn.sigmoid(pg))).astype(y_ref.dtype)
            else:
                gates_ref[...] = jax.nn.sigmoid(project(wr_ref))


def _sample_proj(xn, w_qkv, w_rest, qk_gains, conv_w, state):
    rows = xn.shape[0]
    tn = PROJ_TILE
    assert w_qkv.col0 == 0 and w_rest.col0 == COL_GATE_A

    def const(shape):
        return pl.BlockSpec(shape, lambda n: (0,) * len(shape))

    def gate_block(n):
        return jnp.where(n >= TILE_MERGE_A, n - TILE_MERGE_A, SILU_TILE)

    head_block = (rows, None, N_SLOTS, HEAD_DIM)
    kv_specs = [pl.BlockSpec(head_block, lambda n, g=g: (0, jnp.where(n >= 2 * N_GROUPS + g, 1, 0), 0, 0))
                for g in range(N_GROUPS)]
    return pl.pallas_call(
        _sample_proj_kernel,
        grid=(N_PROJ_TILES,),
        in_specs=[const((rows, D_MODEL)),
                  pl.BlockSpec((D_MODEL, tn), lambda n: (0, jnp.minimum(n, N_QKV_TILES - 1))),
                  pl.BlockSpec((D_MODEL, tn), lambda n: (0, jnp.maximum(n - N_QKV_TILES, 0))),
                  const((2, 1, D_ATTN)), const((CONV_W, D_CONV)), const((rows, (CONV_W - 1) * D_CONV))],
        out_specs=[pl.BlockSpec(head_block, lambda n: (0, jnp.minimum(n, N_GROUPS - 1), 0, 0))] + kv_specs
                  + [pl.BlockSpec((rows, GATE_TILE), lambda n: (0, gate_block(n))),
                     const((rows, D_CONV)), const((rows, D_CONV))],
        out_shape=[jax.ShapeDtypeStruct((rows, N_GROUPS, N_SLOTS, HEAD_DIM), F32)]
                  + [jax.ShapeDtypeStruct((rows, 2, N_SLOTS, HEAD_DIM), F32)] * N_GROUPS
                  + [jax.ShapeDtypeStruct((rows, N_GATE_TILES * GATE_TILE), F32),
                     jax.ShapeDtypeStruct((rows, D_CONV), BF16), jax.ShapeDtypeStruct((rows, D_CONV), F32)],
        scratch_shapes=[pltpu.VMEM((3, rows, D_CONV), F32)],
        compiler_params=_params(("arbitrary",)),
        name="sample_proj",
    )(xn, w_qkv.arr, w_rest.arr, qk_gains, conv_w, state.reshape(rows, (CONV_W - 1) * D_CONV))


def _t5_bucket(dist):
    d = np.maximum(dist, 1).astype(np.float32)
    large = MAX_EXACT + (np.log(d / np.float32(MAX_EXACT)) / np.float32(math.log(MAX_DISTANCE / MAX_EXACT))
                         * np.float32(NUM_BUCKETS - MAX_EXACT)).astype(np.int32)
    large = np.minimum(large, NUM_BUCKETS - 1)
    return np.where(dist < MAX_EXACT, dist, large)


def _bias_by_offset(rel_bias):
    steps = N_BACK - np.arange(N_BACK + 1)
    idx = np.stack([_t5_bucket(d * steps) for d in DILATIONS])
    g_idx = np.arange(N_GROUPS)[:, None]
    vals = jnp.transpose(rel_bias[idx, g_idx], (0, 2, 1)).astype(F32)
    return jnp.pad(vals, ((0, 0), (0, 0), (0, 2 * BLOCK - N_BACK - 1)), constant_values=NEG_INF)


def _sample_bias(cvec):
    kv_head = np.arange(N_BACK * 2 * N_SLOTS) % (2 * N_SLOTS)
    own_k_row = kv_head[None, :] == np.arange(N_SLOTS)[:, None]
    per_row = jnp.repeat(cvec[:, :, :N_BACK], 2 * N_SLOTS, axis=2)
    return jnp.where(own_k_row[None], per_row, NEG_INF), cvec[:, :, N_BACK:N_BACK + 1]


class _SideStatic(NamedTuple):
    per_step: int
    nb: int
    steps: int


class _SidePlan(NamedTuple):
    static: _SideStatic
    in_specs: list
    args: list
    out_specs: list
    out_shape: list
    scratch: list
    finish: Callable


def _attn_side(side, step, cvec_ref, q_ref, kp_ref, kc_ref, vp_ref, vc_ref, o_ref, lse_ref, bias_ref, s_ref, p_ref):
    @pl.when(step == 0)
    def _():
        for h in range(N_SLOTS):
            row = jnp.broadcast_to(cvec_ref[h:h + 1, :], (BLOCK, 2 * BLOCK))
            full = pltpu.roll(row, 0, 1, stride=1, stride_axis=0)
            bias_ref[1, h] = full
            bias_ref[0, h, :, :BLOCK] = jnp.full((BLOCK, BLOCK), NEG_INF, F32)
            bias_ref[0, h, :, BLOCK:] = full[:, BLOCK:]

    nt = (((1,), (1,)), ((), ()))
    heads = [slice(h * HEAD_DIM, (h + 1) * HEAD_DIM) for h in range(N_SLOTS)]
    unit0 = jnp.minimum(step, side.steps - 1) * side.per_step
    lane = lax.broadcasted_iota(jnp.int32, (BLOCK, LANES), 1)
    for u in range(side.per_step):
        rows = slice(u * BLOCK, (u + 1) * BLOCK)
        prev = slice((u - 1) * BLOCK, u * BLOCK)
        for h, sl in enumerate(heads):
            q = q_ref[rows, sl]
            k_prev = kp_ref[:, sl] if u == 0 else kc_ref[prev, sl]
            s_ref[u, h, :, :BLOCK] = lax.dot_general(q, k_prev, nt, preferred_element_type=F32)
            s_ref[u, h, :, BLOCK:] = lax.dot_general(q, kc_ref[rows, sl], nt, preferred_element_type=F32)
    ls, lses = [], []
    for u in range(side.per_step):
        has_prev = jnp.minimum((unit0 + u) % side.nb, 1)
        sc = s_ref[u] * SCALE + bias_ref[has_prev]
        mx = jnp.max(jnp.maximum(sc[:, :, :BLOCK], sc[:, :, BLOCK:]), axis=-1, keepdims=True)
        p = jnp.exp(sc - mx)
        l = jnp.sum(p[:, :, :BLOCK] + p[:, :, BLOCK:], axis=-1, keepdims=True)
        p_ref[u] = p.astype(BF16)
        ls.append(l)
        lses.append(mx + jnp.log(l))
    for u in range(side.per_step):
        rows = slice(u * BLOCK, (u + 1) * BLOCK)
        prev = slice((u - 1) * BLOCK, u * BLOCK)
        lse_all = jnp.zeros((BLOCK, LANES), F32)
        for h, sl in enumerate(heads):
            v_prev = vp_ref[:, sl] if u == 0 else vc_ref[prev, sl]
            o = (jnp.dot(p_ref[u, h, :, :BLOCK], v_prev, preferred_element_type=F32)
                 + jnp.dot(p_ref[u, h, :, BLOCK:], vc_ref[rows, sl], preferred_element_type=F32))
            o_ref[rows, sl] = o / ls[u][h]
            lse_all = jnp.where(lane == h, lses[u][h], lse_all)
        lse_ref[rows, :] = lse_all


def _attn_side_plan(qkv, cvec, g, n_steps, step_of):
    _, dil, ls, _ = qkv.shape
    nb = ls // BLOCK
    units = dil * nb
    per_step = min(d for d in range(1, units + 1) if units % d == 0 and d * n_steps >= units)
    static = _SideStatic(per_step=per_step, nb=nb, steps=units // per_step)
    flat = qkv.reshape(3, dil * ls, D_ATTN)
    rows = per_step * BLOCK

    def cur(kind):
        return pl.BlockSpec((None, rows, D_ATTN), lambda *idx: (kind, jnp.minimum(step_of(*idx), static.steps - 1), 0))

    def prev(kind):
        def index(*idx):
            first_unit = jnp.minimum(step_of(*idx), static.steps - 1) * per_step
            return (kind, jnp.maximum(first_unit - 1, 0), 0)
        return pl.BlockSpec((None, BLOCK, D_ATTN), index)

    def out_spec(width):
        return pl.BlockSpec((rows, width), lambda *idx: (jnp.minimum(step_of(*idx), static.steps - 1), 0))

    return _SidePlan(
        static=static,
        in_specs=[pl.BlockSpec((None, N_SLOTS, 2 * BLOCK), lambda *idx: (g, 0, 0)),
                  cur(0), prev(1), cur(1), prev(2), cur(2)],
        args=[cvec, flat, flat, flat, flat, flat],
        out_specs=[out_spec(D_ATTN), out_spec(LANES)],
        out_shape=[jax.ShapeDtypeStruct((dil * ls, D_ATTN), F32), jax.ShapeDtypeStruct((dil * ls, LANES), F32)],
        scratch=[pltpu.VMEM((2, N_SLOTS, BLOCK, 2 * BLOCK), F32),
                 pltpu.VMEM((per_step, N_SLOTS, BLOCK, 2 * BLOCK), F32),
                 pltpu.VMEM((per_step, N_SLOTS, BLOCK, 2 * BLOCK), BF16)],
        finish=lambda o, lse: (o.reshape(dil, ls, D_ATTN), lse.reshape(dil, ls, LANES)),
    )


def _attn_sample_kernel(bias_ref, nbias_ref, q0_ref, q1_ref, q2_ref, kv0_ref, kv1_ref, kv2_ref,
                        c0_ref, c1_ref, c2_ref, *refs, tb, gate_tile):
    if gate_tile is None:
        o_ref, slab_ref, s_ref, p_ref, og_ref = refs
    else:
        xn_ref, w_ref, o_ref, gate_ref, slab_ref, s_ref, p_ref, og_ref = refs
        _gate_columns(xn_ref[...], w_ref, gate_ref, gate_tile == SILU_TILE)
    rows = pl.ds(pl.program_id(0) * tb, tb)
    nt = (((1,), (1,)), ((), ()))
    kv_rows = 2 * N_SLOTS
    slab_rows = N_BACK * kv_rows
    q_refs = (q0_ref, q1_ref, q2_ref)
    kv_refs = (kv0_ref, kv1_ref, kv2_ref)
    c_refs = (c0_ref, c1_ref, c2_ref)

    def rounded(x):
        return x.astype(BF16).astype(F32)

    qs = [q_ref[rows] for q_ref in q_refs]
    for g in range(N_GROUPS):
        for t in range(tb):
            slab_ref[g, t] = c_refs[g][t].reshape(slab_rows, HEAD_DIM).astype(BF16)
            s_ref[g, t] = lax.dot_general(qs[g][t].astype(BF16), slab_ref[g, t], nt, preferred_element_type=F32)
    p_news, ls, lses = [], [], []
    for g in range(N_GROUPS):
        s_n = jnp.sum(rounded(qs[g]) * rounded(kv_refs[g][rows, 0]), axis=-1, keepdims=True)
        bias = bias_ref[g][None]
        sc_c = jnp.where(bias > NEG_INF, s_ref[g] * SCALE + bias, NEG_INF)
        sc_n = s_n * SCALE + nbias_ref[g][None]
        mx = jnp.maximum(jnp.max(sc_c, axis=-1, keepdims=True), sc_n)
        p_c = jnp.exp(sc_c - mx)
        p_n = jnp.exp(sc_n - mx)
        l = jnp.sum(p_c, axis=-1, keepdims=True) + p_n
        p_ref[g] = pltpu.roll(p_c, N_SLOTS, 2).astype(BF16)
        p_news.append(p_n)
        ls.append(l)
        lses.append(mx + jnp.log(l))
    for g in range(N_GROUPS):
        for t in range(tb):
            og_ref[g, t] = jnp.dot(p_ref[g, t], slab_ref[g, t], preferred_element_type=F32)
    outs = [(og_ref[g] + rounded(p_news[g]) * rounded(kv_refs[g][rows, 1])) / ls[g] for g in range(N_GROUPS)]
    top = jnp.maximum(jnp.maximum(lses[0], lses[1]), lses[2])
    es = [jnp.exp(x - top) for x in lses]
    tot = es[0] + es[1] + es[2]
    o_ref[rows] = (es[0] / tot) * outs[0] + (es[1] / tot) * outs[1] + (es[2] / tot) * outs[2]


def _attn_sample(qs, kvs, caches, bias, nbias, gate_rider=None):
    db = qs.shape[0]
    tb = 4
    kv_rows = 2 * N_SLOTS
    n_steps = db // tb
    rider_in_specs, rider_args, rider_out_specs, rider_out_shape, gate_tile = [], [], [], [], None
    if gate_rider is not None:
        xn, w_in, gate_tile = gate_rider
        rows_per_step = xn.shape[0] // n_steps
        rider_in_specs = [pl.BlockSpec((rows_per_step, D_MODEL), lambda s: (s, 0)),
                          pl.BlockSpec((D_MODEL, GATE_TILE),
                                       lambda s: (0, w_in.block(_gate_weight_block(gate_tile), GATE_TILE)))]
        rider_args = [xn, w_in.arr]
        rider_out_specs = [pl.BlockSpec((rows_per_step, GATE_TILE), lambda s: (s, 0))]
        rider_out_shape = [jax.ShapeDtypeStruct((xn.shape[0], GATE_TILE), F32)]
    views, specs = [], []
    for g, c in enumerate(caches):
        dil = DILATIONS[g]
        lc = c.shape[1]
        views.append(c.reshape(db, lc // dil, dil * kv_rows, HEAD_DIM))
        specs.append(pl.BlockSpec((tb, N_BACK, kv_rows, HEAD_DIM), lambda s: (s, 0, 0, 0)))
    q_specs = [pl.BlockSpec((db, None, N_SLOTS, HEAD_DIM), lambda s, g=g: (0, g, 0, 0)) for g in range(N_GROUPS)]
    kv_spec = pl.BlockSpec((db, 2, N_SLOTS, HEAD_DIM), lambda s: (0, 0, 0, 0))
    outs = pl.pallas_call(
        functools.partial(_attn_sample_kernel, tb=tb, gate_tile=gate_tile),
        grid=(n_steps,),
        in_specs=[pl.BlockSpec((N_GROUPS, N_SLOTS, N_BACK * kv_rows), lambda s: (0, 0, 0)),
                  pl.BlockSpec((N_GROUPS, N_SLOTS, 1), lambda s: (0, 0, 0)),
                  *q_specs, kv_spec, kv_spec, kv_spec] + specs + rider_in_specs,
        out_specs=[pl.BlockSpec((db, N_SLOTS, HEAD_DIM), lambda s: (0, 0, 0))] + rider_out_specs,
        out_shape=[jax.ShapeDtypeStruct((db, N_SLOTS, HEAD_DIM), F32)] + rider_out_shape,
        scratch_shapes=[pltpu.VMEM((N_GROUPS, tb, N_BACK * kv_rows, HEAD_DIM), BF16),
                        pltpu.VMEM((N_GROUPS, tb, N_SLOTS, N_BACK * kv_rows), F32),
                        pltpu.VMEM((N_GROUPS, tb, N_SLOTS, N_BACK * kv_rows), BF16),
                        pltpu.VMEM((N_GROUPS, tb, N_SLOTS, HEAD_DIM), F32)],
        compiler_params=_params(("arbitrary",)),
        name="attn_sample",
    )(bias, nbias, qs, qs, qs, *kvs, *views, *rider_args)
    return outs[0] if gate_rider is None else outs


def _out_kernel(*refs, tm, n_groups):
    o_refs = refs[:n_groups]
    lse_refs = refs[n_groups:2 * n_groups] if n_groups > 1 else ()
    rest = refs[len(o_refs) + len(lse_refs):]
    def project(a):
        ya = jnp.dot(a, woa_ref[...], preferred_element_type=F32)
        yb = jnp.dot(yb_ref[...], wob_ref[...], preferred_element_type=F32)
        merged = sma_ref[...] * ya + smb_ref[...] * yb
        y_ref[...] = x_ref[...] + jnp.dot(merged.astype(BF16), wo_ref[...], preferred_element_type=F32)

    if n_groups > 1:
        sga_ref, sma_ref, smb_ref, yb_ref, x_ref, woa_ref, wob_ref, wo_ref, y_ref, a2_ref, og_ref, lg_ref = rest
        step = pl.program_id(0)

        @pl.when(step == 0)
        def _():
            a2_ref[1] = jnp.zeros(a2_ref.shape[1:], BF16)

        project(a2_ref[(step + 1) % 2])
        a_ref = a2_ref.at[step % 2]
        for g in range(n_groups):
            dil = DILATIONS[g]
            sub = tm // dil
            for r in range(dil):
                rows = slice(None) if dil == 1 else pl.ds(r, sub, stride=dil)
                lg_ref[g, rows, :] = lse_refs[g][r]
                for h in range(N_SLOTS):
                    og_ref[g, h, rows, :] = o_refs[g][r, :, h * HEAD_DIM:(h + 1) * HEAD_DIM]
        lses = [lg_ref[g] for g in range(n_groups)]
        top = functools.reduce(jnp.maximum, lses)
        es = [jnp.exp(x - top) for x in lses]
        tot = functools.reduce(lambda a, b: a + b, es)
        ws = [e / tot for e in es]
        for h in range(N_SLOTS):
            sl = slice(h * HEAD_DIM, (h + 1) * HEAD_DIM)
            o = functools.reduce(lambda a, b: a + b,
                                 [ws[g][:, h:h + 1] * og_ref[g, h] for g in range(n_groups)])
            a_ref[:, sl] = (o * sga_ref[:, sl]).astype(BF16)
    else:
        sga_ref, sma_ref, smb_ref, yb_ref, x_ref, woa_ref, wob_ref, wo_ref, y_ref = rest
        project((o_refs[0][...] * sga_ref[...]).astype(BF16))


def _out_proj(os_, lses, gates, yb_in, x, w_out_a, w_out_b, w_o):
    rows = x.shape[0]

    tm = min(rows, 256)
    n_groups = len(os_)
    n_blocks = rows // tm
    pipelined = n_groups > 1
    n_steps = n_blocks + 1 if pipelined else n_blocks

    def comb_block(s):
        return jnp.minimum(s, n_blocks - 1)

    def proj_block(s):
        return jnp.maximum(s - 1, 0) if pipelined else s

    def gate_operand(tile, width, block_of):
        arr, tile0 = next((a, t0) for a, t0 in gates if t0 <= tile < t0 + a.shape[1] // GATE_TILE)
        assert (tile - tile0) * GATE_TILE % width == 0
        return arr, pl.BlockSpec((tm, width), lambda s: (block_of(s), (tile - tile0) * GATE_TILE // width))

    def row_spec(width, block_of):
        return pl.BlockSpec((tm, width), lambda s: (block_of(s), 0))

    def sub_spec(dil, width):
        return pl.BlockSpec((dil, tm // dil, width), lambda s: (0, comb_block(s), 0))

    def const_spec(shape):
        return pl.BlockSpec(shape, lambda s: (0, 0), pipeline_mode=pl.Buffered(1))

    if pipelined:
        in_specs = ([sub_spec(DILATIONS[g], D_ATTN) for g in range(n_groups)]
                    + [sub_spec(DILATIONS[g], LANES) for g in range(n_groups)])
        scratch = [pltpu.VMEM((2, tm, D_ATTN), BF16),
                   pltpu.VMEM((n_groups, N_SLOTS, tm, HEAD_DIM), F32), pltpu.VMEM((n_groups, tm, LANES), F32)]
    else:
        in_specs = [row_spec(D_ATTN, comb_block)]
        scratch = []
    sga, sga_spec = gate_operand(SILU_TILE, D_ATTN, comb_block)
    sma, sma_spec = gate_operand(0, D_MODEL, proj_block)
    smb, smb_spec = gate_operand(D_MODEL // GATE_TILE, D_MODEL, proj_block)
    in_specs += [sga_spec, sma_spec, smb_spec, row_spec(D_CONV, proj_block), row_spec(D_MODEL, proj_block),
                 const_spec((D_ATTN, D_MODEL)), const_spec((D_CONV, D_MODEL)), const_spec((D_MODEL, D_MODEL))]
    return pl.pallas_call(
        functools.partial(_out_kernel, tm=tm, n_groups=n_groups),
        grid=(n_steps,),
        in_specs=in_specs,
        out_specs=row_spec(D_MODEL, proj_block),
        out_shape=jax.ShapeDtypeStruct((rows, D_MODEL), F32),
        scratch_shapes=scratch,
        compiler_params=_params(("arbitrary",)),
        name="out_proj",
    )(*os_, *lses, sga, sma, smb, yb_in, x, w_out_a, w_out_b, w_o)


def kernel(x_prompt, x_sample, cache_kv_w128, cache_kv_w512, cache_kv_w2048, state_conv, norm_gain, w_in,
           q_norm_gain, k_norm_gain, rel_bias, conv_w, w_out_a, w_out_b, w_o):
    seq = x_prompt.shape[1]
    db = x_sample.shape[0]
    xp = x_prompt.reshape(seq, D_MODEL)
    xs = x_sample.reshape(db, D_MODEL)
    caches = (cache_kv_w128, cache_kv_w512, cache_kv_w2048)
    qk_gains = jnp.stack([jnp.tile(q_norm_gain, N_SLOTS), jnp.tile(k_norm_gain, N_SLOTS)]).reshape(2, 1, D_ATTN)
    cvec = _bias_by_offset(rel_bias)

    xn = _pre_norm(xp, norm_gain)
    qkv0, kv_p0, w_qkv = _qkv_proj(xn, _Cols(w_in, 0), qk_gains, 0, DILATIONS[0], BF16, min(WINDOWS[0], seq),
                                   casts=((w_in, 0, COL_GATE_A),))
    w_qkv = _Cols(w_qkv, 0)
    qkv1, kv_p1, w_rest, woa, wob, wo = _qkv_proj(
        xn, w_qkv, qk_gains, 1, DILATIONS[1], BF16, min(WINDOWS[1], seq),
        casts=((w_in, COL_GATE_A, w_in.shape[1] - COL_GATE_A),
               (w_out_a, 0, D_MODEL), (w_out_b, 0, D_MODEL), (w_o, 0, D_MODEL)))
    w_rest = _Cols(w_rest, COL_GATE_A)
    qkv2, kv_p2 = _qkv_proj(xn, w_qkv, qk_gains, 2, DILATIONS[2], BF16, min(WINDOWS[2], seq))
    qkvs, kv_p = (qkv0, qkv1, qkv2), (kv_p0, kv_p1, kv_p2)
    xn_s = _pre_norm(xs, norm_gain)
    q_s, *kv_s, gates_s, yb_in_s, h_s = _sample_proj(xn_s, w_qkv, w_rest, qk_gains, conv_w, state_conv)
    o_s, gates_c = _attn_sample(q_s, kv_s, caches, *_sample_bias(cvec), gate_rider=(xn, w_rest, SILU_TILE))
    y_sample = _out_proj((o_s.reshape(db, D_ATTN),), (), [(gates_s, 0)], yb_in_s, xs, woa, wob, wo)
    y_sample = y_sample.reshape(db, 1, D_MODEL)

    gates_a, attn0 = _gates(xn, w_rest, 0, 2, attn=(qkvs[0], cvec, 0))
    gates_b, attn1 = _gates(xn, w_rest, 2, 2, attn=(qkvs[1], cvec, 1))
    yb_in, h_last, attn2 = _branch_b(xn, w_rest, conv_w, attn=(qkvs[2], cvec, 2))
    gates = [(gates_a, 0), (gates_b, 2), (gates_c, SILU_TILE)]
    os_, lses = zip(attn0, attn1, attn2)
    y_prompt = _out_proj(os_, lses, gates, yb_in, xp, woa, wob, wo).reshape(1, seq, D_MODEL)
    conv_p = h_last[SUBLANES - (CONV_W - 1):][None]
    conv_s = jnp.stack([state_conv[:, CONV_W - 2], h_s], axis=1)

    return (y_prompt, y_sample, kv_p[0][None], kv_p[1][None], kv_p[2][None], conv_p,
            kv_s[0][:, None], kv_s[1][:, None], kv_s[2][:, None], conv_s)
```
